```python
import jax, jax.numpy as jnp
from jax import lax
import numpy as np

D_MODEL = 2048
BATCH = 16
SEQ = 256
DEPTH = 1
DEC_BATCH = 8
DEC_SEQ = 4096
PAST_LEN = 512

GRID_W = 64
M_HEADS = 8
M_DK = 128
M_DV = 128
M_CHUNK = 128
A_HEADS = 16
A_KV = 4
A_HD = 128
A_GROUPS = A_HEADS // A_KV
WINDOW = 128
Q_BLOCK = 128
ROPE_BASE = 10000.0
N_EXPERTS = 64
TOP_K = 6
N_GROUPS = 8
TOPK_GROUPS = 4
D_EXPERT = 1408
ROUTED_SCALE = 2.5
EXPERT_BLOCK = 128
N_MOD = 6
EPS = 1e-6
IN_SIZES = (M_HEADS * M_DK, M_HEADS * M_DK, M_HEADS * M_DV, M_HEADS * M_DV, 2 * M_HEADS, 2 * M_HEADS,
            A_HEADS * A_HD, A_KV * A_HD, A_KV * A_HD, 2 * D_MODEL)
D_IN = 4 * M_HEADS * M_DK + 4 * M_HEADS + A_HEADS * A_HD + 2 * A_KV * A_HD + 2 * D_MODEL

kernel_name = 'hybrid_mlstm_swa_moe_diffusion_step'


def rmsnorm(x, w):
    xf = x.astype(jnp.float32)
    y = xf * lax.rsqrt(jnp.mean(xf * xf, axis=-1, keepdims=True) + EPS)
    return y.astype(x.dtype) * w


def modulation(cond, w_mod, b_mod):
    m = jax.nn.silu(cond) @ w_mod + b_mod
    return jnp.split(m[..., None, :], N_MOD, axis=-1)


def mlstm_scan(q, k, v, ig, fg, C0, n0, m0):
    B, S, H, _ = q.shape
    nc = S // M_CHUNK
    f32 = jnp.float32

    def chunks(a):
        a = a.astype(f32).reshape((B, nc, M_CHUNK) + a.shape[2:])
        return jnp.moveaxis(a, (1, 3), (0, 2))

    tril = jnp.tril(jnp.ones((M_CHUNK, M_CHUNK), dtype=bool))

    def step(carry, xs):
        C, n, m = carry
        qc, kc, vc, ic, lfc = xs
        b = jnp.cumsum(lfc, axis=-1)
        d = jnp.where(tril, b[..., :, None] - b[..., None, :] + ic[..., None, :], -jnp.inf)
        inter = b + m[..., None]
        m_t = jnp.maximum(inter, jnp.max(d, axis=-1))
        s = jnp.einsum('bhtd,bhsd->bhts', qc, kc) * jnp.exp(d - m_t[..., None])
        w_inter = jnp.exp(inter - m_t)
        num = jnp.einsum('bhts,bhsv->bhtv', s, vc) + w_inter[..., None] * jnp.einsum('bhtd,bhdv->bhtv', qc, C)
        den = jnp.sum(s, axis=-1) + w_inter * jnp.einsum('bhtd,bhd->bht', qc, n)
        h = num / jnp.maximum(jnp.abs(den), jnp.exp(-m_t))[..., None]
        b_last = b[..., -1]
        g = b_last[..., None] - b + ic
        m_new = jnp.maximum(b_last + m, jnp.max(g, axis=-1))
        wk = jnp.exp(g - m_new[..., None])
        wc = jnp.exp(b_last + m - m_new)
        C_new = wc[..., None, None] * C + jnp.einsum('bhs,bhsd,bhsv->bhdv', wk, kc, vc)
        n_new = wc[..., None] * n + jnp.einsum('bhs,bhsd->bhd', wk, kc)
        return (C_new, n_new, m_new), h

    xs = (chunks(q), chunks(k), chunks(v), chunks(ig), chunks(jax.nn.log_sigmoid(fg.astype(f32))))
    state, h = lax.scan(step, (C0.astype(f32), n0.astype(f32), m0.astype(f32)), xs)
    h = jnp.moveaxis(h, (0, 2), (1, 3)).reshape(B, S, H, -1)
    return h, state


def mlstm_bidir(q, k, v, ig, fg, state):
    C0, n0, m0 = state
    hf, (Cf, nf, mf) = mlstm_scan(q, k, v, ig[:, :, 0], fg[:, :, 0], C0[:, 0], n0[:, 0], m0[:, 0])
    rev = lambda a: jnp.flip(a, axis=1)
    hb, (Cb, nb, mb) = mlstm_scan(rev(q), rev(k), rev(v), rev(ig[:, :, 1]), rev(fg[:, :, 1]),
                                  C0[:, 1], n0[:, 1], m0[:, 1])
    h = hf + rev(hb)
    return h, (jnp.stack([Cf, Cb], axis=1), jnp.stack([nf, nb], axis=1), jnp.stack([mf, mb], axis=1))


def axial_rope(x):
    B, S, H, HD = x.shape
    rows = S // GRID_W
    row = jnp.repeat(jnp.arange(rows), GRID_W)
    col = jnp.tile(jnp.arange(GRID_W), rows)
    nf = HD // 4
    inv = ROPE_BASE ** (-jnp.arange(nf, dtype=jnp.float32) / nf)

    def rot(xp, pos):
        ang = pos.astype(jnp.float32)[:, None] * inv
        cos = jnp.cos(ang)[None, :, None, :].astype(x.dtype)
        sin = jnp.sin(ang)[None, :, None, :].astype(x.dtype)
        x1, x2 = xp[..., :nf], xp[..., nf:]
        return jnp.concatenate([x1 * cos - x2 * sin, x1 * sin + x2 * cos], axis=-1)

    return jnp.concatenate([rot(x[..., :HD // 2], row), rot(x[..., HD // 2:], col)], axis=-1)


def sink_probs(s, sink):
    sk = jnp.broadcast_to(sink.astype(jnp.float32).reshape(1, A_KV, A_GROUPS, 1, 1), s.shape[:-1] + (1,))
    return jax.nn.softmax(jnp.concatenate([s, sk], axis=-1), axis=-1)[..., :-1]


def context_attention(q, k, v, sink):
    B, S = q.shape[:2]
    nb = S // Q_BLOCK
    qb = (q * (A_HD ** -0.5)).reshape(B, nb, Q_BLOCK, A_KV, A_GROUPS, A_HD).swapaxes(0, 1)

    def block(qc):
        s = jnp.einsum('bqkgd,bskd->bkgqs', qc, k).astype(jnp.float32)
        p = sink_probs(s, sink).astype(v.dtype)
        return jnp.einsum('bkgqs,bskd->bqkgd', p, v)

    o = lax.map(block, qb)
    return o.swapaxes(0, 1).reshape(B, S, A_HEADS * A_HD)


def latent_attention(q, k, v, k_ctx, v_ctx, sink):
    B, S = q.shape[:2]
    nb = S // Q_BLOCK
    span = Q_BLOCK + 2 * WINDOW
    pad = ((0, 0), (WINDOW, WINDOW), (0, 0), (0, 0))
    kp, vp = jnp.pad(k, pad), jnp.pad(v, pad)
    qs = q * (A_HD ** -0.5)
    r_q = jnp.arange(Q_BLOCK)
    r_k = jnp.arange(span)

    def block(j):
        start = j * Q_BLOCK
        qc = lax.dynamic_slice_in_dim(qs, start, Q_BLOCK, axis=1).reshape(B, Q_BLOCK, A_KV, A_GROUPS, A_HD)
        kl = lax.dynamic_slice_in_dim(kp, start, span, axis=1)
        vl = lax.dynamic_slice_in_dim(vp, start, span, axis=1)
        qpos = start + r_q
        kpos = start - WINDOW + r_k
        valid = (kpos[None, :] >= 0) & (kpos[None, :] < S) & (jnp.abs(qpos[:, None] - kpos[None, :]) <= WINDOW)
        s_loc = jnp.where(valid, jnp.einsum('bqkgd,bskd->bkgqs', qc, kl).astype(jnp.float32), -jnp.inf)
        s_ctx = jnp.einsum('bqkgd,bskd->bkgqs', qc, k_ctx).astype(jnp.float32)
        p = sink_probs(jnp.concatenate([s_loc, s_ctx], axis=-1), sink).astype(v.dtype)
        return (jnp.einsum('bkgqs,bskd->bqkgd', p[..., :span], vl)
                + jnp.einsum('bkgqs,bskd->bqkgd', p[..., span:], v_ctx))

    o = lax.map(block, jnp.arange(nb))
    return o.swapaxes(0, 1).reshape(B, S, A_HEADS * A_HD)


def routed_experts(xf, e_idx, wts, e_w1, e_w3, e_w2):
    T, D = xf.shape
    A = T * TOP_K
    flat_e = e_idx.reshape(-1).astype(jnp.int32)
    flat_tok = jnp.arange(A, dtype=jnp.int32) // TOP_K
    flat_w = wts.reshape(-1)
    order = jnp.argsort(flat_e)
    se = flat_e[order]
    counts = jnp.bincount(flat_e, length=N_EXPERTS).astype(jnp.int32)
    padded = (counts + EXPERT_BLOCK - 1) // EXPERT_BLOCK * EXPERT_BLOCK
    start_sorted = jnp.cumsum(counts) - counts
    end_pad = jnp.cumsum(padded)
    start_pad = end_pad - padded
    dest = start_pad[se] + jnp.arange(A, dtype=jnp.int32) - start_sorted[se]
    nblk = -(-(A + N_EXPERTS * (EXPERT_BLOCK - 1)) // EXPERT_BLOCK)
    npad = nblk * EXPERT_BLOCK
    slot_tok = jnp.full((npad,), T, dtype=jnp.int32).at[dest].set(flat_tok[order])
    slot_w = jnp.zeros((npad,), dtype=flat_w.dtype).at[dest].set(flat_w[order])
    blk_e = jnp.minimum(jnp.searchsorted(end_pad, jnp.arange(nblk, dtype=jnp.int32) * EXPERT_BLOCK, side='right'),
                        N_EXPERTS - 1)
    x_pad = jnp.concatenate([xf, jnp.zeros((1, D), xf.dtype)], axis=0)

    def body(acc, xs):
        tok, wt, e = xs
        xb = x_pad[tok]
        hb = jax.nn.silu(xb @ e_w1[e]) * (xb @ e_w3[e])
        yb = (hb @ e_w2[e]) * wt[:, None].astype(xf.dtype)
        return acc.at[tok].add(yb), None

    acc, _ = lax.scan(body, jnp.zeros((T + 1, D), xf.dtype),
                      (slot_tok.reshape(nblk, EXPERT_BLOCK), slot_w.reshape(nblk, EXPERT_BLOCK), blk_e))
    return acc[:T]


def moe_ffn(h, router_w, router_b, e_w1, e_w3, e_w2, s_w1, s_w3, s_w2):
    B, S, D = h.shape
    T = B * S
    xf = h.reshape(T, D)
    scores = jax.nn.sigmoid((xf @ router_w).astype(jnp.float32))
    biased = scores + router_b.astype(jnp.float32)
    grp = lax.top_k(biased.reshape(T, N_GROUPS, N_EXPERTS // N_GROUPS), 2)[0].sum(-1)
    _, g_idx = lax.top_k(grp, TOPK_GROUPS)
    g_mask = jax.nn.one_hot(g_idx, N_GROUPS, dtype=jnp.float32).sum(1) > 0
    e_mask = jnp.repeat(g_mask, N_EXPERTS // N_GROUPS, axis=1)
    _, e_idx = lax.top_k(jnp.where(e_mask, biased, -jnp.inf), TOP_K)
    w = jnp.take_along_axis(scores, e_idx, axis=1)
    w = w / jnp.sum(w, axis=-1, keepdims=True) * ROUTED_SCALE
    routed = routed_experts(xf, e_idx, w, e_w1, e_w3, e_w2)
    shared = (jax.nn.silu(xf @ s_w1) * (xf @ s_w3)) @ s_w2
    return (routed + shared).reshape(B, S, D)


def trunk_layer(x, mods, p, ctx=None):
    (n1, n2, w_in, ig_b, fg_b, m_norm, sink, w_br_m, w_br_a, w_out,
     router_w, router_b, e_w1, e_w3, e_w2, s_w1, s_w3, s_w2) = p
    sh1, sc1, g1, sh2, sc2, g2 = mods
    B, S, _ = x.shape
    h = rmsnorm(x, n1) * (1 + sc1) + sh1
    points = [int(o) for o in np.cumsum(IN_SIZES)[:-1]]
    qm, km, vm, om, im, fm, qa, ka, va, gb = jnp.split(h @ w_in, points, axis=-1)
    qm = qm.reshape(B, S, M_HEADS, M_DK)
    km = km.reshape(B, S, M_HEADS, M_DK) * (M_DK ** -0.5)
    vm = vm.reshape(B, S, M_HEADS, M_DV)
    im = im.reshape(B, S, 2, M_HEADS) + ig_b
    fm = fm.reshape(B, S, 2, M_HEADS) + fg_b
    qa = qa.reshape(B, S, A_HEADS, A_HD)
    ka = ka.reshape(B, S, A_KV, A_HD)
    va = va.reshape(B, S, A_KV, A_HD)
    if ctx is None:
        f32 = jnp.float32
        init = (jnp.zeros((B, 2, M_HEADS, M_DK, M_DV), f32), jnp.zeros((B, 2, M_HEADS, M_DK), f32),
                jnp.zeros((B, 2, M_HEADS), f32))
        hm, (C_new, n_new, m_new) = mlstm_bidir(qm, km, vm, im, fm, init)
        ha = context_attention(qa, ka, va, sink)
        new_ctx = (ka, va, C_new, n_new, m_new)
    else:
        k_ctx, v_ctx, C0, n0, m0 = ctx
        hm, _ = mlstm_bidir(qm, km, vm, im, fm, (C0, n0, m0))
        ha = latent_attention(axial_rope(qa), axial_rope(ka), va, k_ctx, v_ctx, sink)
        new_ctx = None
    hm = rmsnorm(hm, m_norm.reshape(M_HEADS, M_DV)).reshape(B, S, M_HEADS * M_DV).astype(x.dtype) * jax.nn.sigmoid(om)
    g_m, g_a = jnp.split(jax.nn.sigmoid(gb), 2, axis=-1)
    x = x + g1 * ((g_m * (hm @ w_br_m) + g_a * (ha @ w_br_a)) @ w_out)
    h2 = rmsnorm(x, n2) * (1 + sc2) + sh2
    x = x + g2 * moe_ffn(h2, router_w, router_b, e_w1, e_w3, e_w2, s_w1, s_w3, s_w2)
    return x, new_ctx


def setup_inputs(seed: int = 0) -> dict:
    key = jax.random.key(seed)
    ks = jax.random.split(key, 32)
    f32 = jnp.float32
    D = D_MODEL

    def nrm(k, shape, s):
        return s * jax.random.normal(k, shape, f32)

    return {
        'x_prompt': nrm(ks[0], (BATCH, SEQ, D), 1.0),
        'x_sample': nrm(ks[1], (DEC_BATCH, DEC_SEQ, D), 1.0),
        'cache_k': nrm(ks[2], (DEC_BATCH, DEPTH, PAST_LEN, A_KV, A_HD), 1.0),
        'cache_v': nrm(ks[3], (DEC_BATCH, DEPTH, PAST_LEN, A_KV, A_HD), 1.0),
        'state_mlstm_C': nrm(ks[4], (DEC_BATCH, DEPTH, 2, M_HEADS, M_DK, M_DV), 0.1),
        'state_mlstm_n': nrm(ks[5], (DEC_BATCH, DEPTH, 2, M_HEADS, M_DK), 0.1),
        'state_mlstm_m': nrm(ks[6], (DEC_BATCH, DEPTH, 2, M_HEADS), 1.0),
        'c': nrm(ks[7], (DEC_BATCH, D), 1.0),
        'c_ctx': nrm(ks[8], (D,), 1.0),
        'w_mod': nrm(ks[9], (DEPTH, D, N_MOD * D), 0.5 * D ** -0.5),
        'b_mod': nrm(ks[10], (DEPTH, N_MOD * D), 0.02),
        'norm1_w': 1.0 + nrm(ks[11], (DEPTH, D), 0.02),
        'norm2_w': 1.0 + nrm(ks[12], (DEPTH, D), 0.02),
        'w_in': nrm(ks[13], (DEPTH, D, D_IN), D ** -0.5),
        'igate_b': nrm(ks[14], (DEPTH, 2, M_HEADS), 0.1),
        'fgate_b': 3.0 + nrm(ks[15], (DEPTH, 2, M_HEADS), 0.5),
        'mlstm_norm_w': 1.0 + nrm(ks[16], (DEPTH, M_HEADS * M_DV), 0.02),
        'attn_sink': nrm(ks[17], (DEPTH, A_HEADS), 0.5),
        'w_branch_m': nrm(ks[18], (DEPTH, M_HEADS * M_DV, D), (M_HEADS * M_DV) ** -0.5),
        'w_branch_a': nrm(ks[19], (DEPTH, A_HEADS * A_HD, D), (A_HEADS * A_HD) ** -0.5),
        'w_out': nrm(ks[20], (DEPTH, D, D), D ** -0.5),
        'router_w': nrm(ks[21], (DEPTH, D, N_EXPERTS), D ** -0.5),
        'router_b': nrm(ks[22], (DEPTH, N_EXPERTS), 0.01),
        'expert_w1': nrm(ks[23], (DEPTH, N_EXPERTS, D, D_EXPERT), D ** -0.5),
        'expert_w3': nrm(ks[24], (DEPTH, N_EXPERTS, D, D_EXPERT), D ** -0.5),
        'expert_w2': nrm(ks[25], (DEPTH, N_EXPERTS, D_EXPERT, D), D_EXPERT ** -0.5),
        'shared_w1': nrm(ks[26], (DEPTH, D, D_EXPERT), D ** -0.5),
        'shared_w3': nrm(ks[27], (DEPTH, D, D_EXPERT), D ** -0.5),
        'shared_w2': nrm(ks[28], (DEPTH, D_EXPERT, D), D_EXPERT ** -0.5),
        'final_norm_w': 1.0 + nrm(ks[29], (D,), 0.02),
    }


def reference(x_prompt, x_sample, cache_k, cache_v, state_mlstm_C, state_mlstm_n, state_mlstm_m, c, c_ctx,
              w_mod, b_mod, norm1_w, norm2_w, w_in, igate_b, fgate_b, mlstm_norm_w, attn_sink,
              w_branch_m, w_branch_a, w_out, router_w, router_b, expert_w1, expert_w3, expert_w2,
              shared_w1, shared_w3, shared_w2, final_norm_w):
    xp, xs = x_prompt, x_sample
    new_k, new_v, new_C, new_n, new_m = [], [], [], [], []
    for l in range(DEPTH):
        p = (norm1_w[l], norm2_w[l], w_in[l], igate_b[l], fgate_b[l], mlstm_norm_w[l], attn_sink[l],
             w_branch_m[l], w_branch_a[l], w_out[l], router_w[l], router_b[l],
             expert_w1[l], expert_w3[l], expert_w2[l], shared_w1[l], shared_w3[l], shared_w2[l])
        xp, (k_l, v_l, C_l, n_l, m_l) = trunk_layer(xp, modulation(c_ctx, w_mod[l], b_mod[l]), p)
        new_k.append(k_l)
        new_v.append(v_l)
        new_C.append(C_l)
        new_n.append(n_l)
        new_m.append(m_l)
        ctx = (cache_k[:, l], cache_v[:, l], state_mlstm_C[:, l], state_mlstm_n[:, l], state_mlstm_m[:, l])
        xs, _ = trunk_layer(xs, modulation(c, w_mod[l], b_mod[l]), p, ctx)
    y_prompt = rmsnorm(xp, final_norm_w)
    y_sample = rmsnorm(xs, final_norm_w)
    return (y_prompt, y_sample, jnp.stack(new_k, axis=1), jnp.stack(new_v, axis=1),
            jnp.stack(new_C, axis=1), jnp.stack(new_n, axis=1), jnp.stack(new_m, axis=1))
```

```python
import functools

import numpy as np
import jax
import jax.numpy as jnp
from jax import lax
from jax.experimental import pallas as pl
from jax.experimental.pallas import tpu as pltpu

TOP_K = 6
N_GROUPS = 8
TOPK_GROUPS = 4
ROUTED_SCALE = 2.5
WINDOW = 128
Q_BLOCK = 128
GRID_W = 64
ROPE_BASE = 10000.0
M_CHUNK = 128
N_MOD = 6
EPS = 1e-6

LANES_V7X = 128
MXU_COLS_V7X = 256
VMEM_LIMIT_V7X = 56 * 1024 * 1024

TM_INPROJ = 512
TM_OUTPROJ = 256
TM_FINAL = 256
TN_MOD = 1024
EXPERT_ROWS = 256

F32 = jnp.float32
BF16 = jnp.bfloat16
_NT = (((1,), (1,)), ((), ()))


def _params(sem):
    return pltpu.CompilerParams(dimension_semantics=sem, vmem_limit_bytes=VMEM_LIMIT_V7X)


def _dot(a, b):
    return jnp.dot(a, b, preferred_element_type=F32)


def _dot_nt(a, b):
    return lax.dot_general(a, b, _NT, preferred_element_type=F32)


def _mod_kernel(c_ref, w_ref, b_ref, o_ref):
    s = jax.nn.silu(c_ref[...]).astype(BF16)
    o_ref[...] = _dot(s, w_ref[...].astype(BF16)) + b_ref[...]


def _modulation(cond, w_mod, b_mod):
    R, D = cond.shape
    N = w_mod.shape[1]
    tn = min(TN_MOD, N)
    return pl.pallas_call(
        _mod_kernel,
        out_shape=jax.ShapeDtypeStruct((R, N), F32),
        grid=(N // tn,),
        in_specs=[pl.BlockSpec((R, D), lambda n: (0, 0)),
                  pl.BlockSpec((D, tn), lambda n: (0, n)),
                  pl.BlockSpec((1, tn), lambda n: (0, n))],
        out_specs=pl.BlockSpec((R, tn), lambda n: (0, n)),
        compiler_params=_params(("arbitrary",)),
        name="modulation",
    )(cond, w_mod, b_mod.reshape(1, N))


def _rope_slice(x, cos, sin_signed, first_half):
    swap = jnp.where(first_half, pltpu.roll(x, 96, 1), pltpu.roll(x, 32, 1))
    return x * cos + swap * sin_signed


def _inproj_kernel(xp_ref, xs_ref, mod_ref, n1_ref, w_ref, wg_ref, cos_ref, sin_ref,
                   z_ref, kv_ref, g_ref, gt_ref, h_scr, *, nctx_tiles, n_rope_tiles, kv_tile, n_gates):
    i = pl.program_id(0)
    n = pl.program_id(1)
    is_ctx = i < nctx_tiles
    tn = z_ref.shape[1]

    @pl.when(n == 0)
    def _():
        x = jnp.where(is_ctx, xp_ref[...], xs_ref[...])
        y = x * lax.rsqrt(jnp.mean(x * x, axis=-1, keepdims=True) + EPS) * n1_ref[...]
        h = (y * (1.0 + mod_ref[0, 1:2, :]) + mod_ref[0, 0:1, :]).astype(BF16)
        h_scr[...] = h
        g = _dot(h, wg_ref[...])
        g_ref[...] = g[:, :n_gates]
        gt_ref[...] = g.T[:n_gates, :]

    acc = _dot(h_scr[...], w_ref[...])

    def rope_cols(ncols):
        cos = jnp.where(is_ctx, 1.0, cos_ref[...])
        sin = jnp.where(is_ctx, 0.0, sin_ref[...])
        lane = lax.broadcasted_iota(jnp.int32, cos.shape, 1)
        first_half = (lane % 64) < 32
        return [_rope_slice(acc[:, c:c + LANES_V7X], cos, sin, first_half)
                for c in range(0, ncols, LANES_V7X)]

    @pl.when(n < n_rope_tiles)
    def _():
        z_ref[...] = jnp.concatenate(rope_cols(tn), axis=1).astype(BF16)

    @pl.when(n == kv_tile)
    def _():
        r = jnp.concatenate(rope_cols(tn // 2) + [acc[:, tn // 2:]], axis=1)
        z_ref[...] = r.astype(BF16)
        kv_ref[...] = r

    @pl.when(jnp.logical_and(n >= n_rope_tiles, n != kv_tile))
    def _():
        z_ref[...] = acc.astype(BF16)


def _inproj(x_prompt2, x_sample2, mods, n1, w_main, w_gate, cos, sin, *, t_ctx, dec_seq, tn, n_rope_tiles, n_gates):
    t_lat, D = x_sample2.shape
    t_all = t_ctx + t_lat
    tm = min(TM_INPROJ, t_ctx, dec_seq)
    nctx = t_ctx // tm
    per_seq = dec_seq // tm
    ncols = w_main.shape[1]
    ntile = ncols // tn
    kv_tile = ntile - 1

    def mod_row(i):
        return jnp.where(i < nctx, 0, 1 + (i - nctx) // per_seq)

    def pos_blk(i):
        return jnp.where(i < nctx, 0, (i - nctx) % per_seq)

    kernel = functools.partial(_inproj_kernel, nctx_tiles=nctx, n_rope_tiles=n_rope_tiles, kv_tile=kv_tile,
                               n_gates=n_gates)
    return pl.pallas_call(
        kernel,
        out_shape=(jax.ShapeDtypeStruct((t_all, ncols), BF16),
                   jax.ShapeDtypeStruct((t_all, tn), F32),
                   jax.ShapeDtypeStruct((t_all, n_gates), F32),
                   jax.ShapeDtypeStruct((n_gates, t_all), F32)),
        grid=(t_all // tm, ntile),
        in_specs=[pl.BlockSpec((tm, D), lambda i, n: (jnp.minimum(i, nctx - 1), 0)),
                  pl.BlockSpec((tm, D), lambda i, n: (jnp.maximum(i - nctx, 0), 0)),
                  pl.BlockSpec((1, N_MOD, D), lambda i, n: (mod_row(i), 0, 0)),
                  pl.BlockSpec((1, D), lambda i, n: (0, 0)),
                  pl.BlockSpec((D, tn), lambda i, n: (0, n)),
                  pl.BlockSpec((D, LANES_V7X), lambda i, n: (0, 0)),
                  pl.BlockSpec((tm, LANES_V7X), lambda i, n: (pos_blk(i), 0)),
                  pl.BlockSpec((tm, LANES_V7X), lambda i, n: (pos_blk(i), 0))],
        out_specs=(pl.BlockSpec((tm, tn), lambda i, n: (i, n)),
                   pl.BlockSpec((tm, tn), lambda i, n: (i, 0)),
                   pl.BlockSpec((tm, n_gates), lambda i, n: (i, 0)),
                   pl.BlockSpec((n_gates, tm), lambda i, n: (0, i))),
        scratch_shapes=[pltpu.VMEM((tm, D), BF16)],
        compiler_params=_params(("arbitrary", "arbitrary")),
        name="inproj",
    )(x_prompt2, x_sample2, mods, n1, w_main, w_gate, cos, sin)


def _mlstm_kernel(fb, bb, sq, fi, la,
                  qf, kf, vf, qb, kb, vb, gf, gb, gtf, gtb, brow, bcol, s0, m0,
                  hf, hb, s_out, m_out, s_scr, m_scr, *, mh, dk, chunk):
    s = pl.program_id(0)
    L = chunk
    scale = dk ** -0.5

    @pl.when(fi[s] == 1)
    def _():
        s_scr[...] = s0[0]
        m_scr[...] = m0[0]

    ri = lax.broadcasted_iota(jnp.int32, (L, L), 0)
    ci = lax.broadcasted_iota(jnp.int32, (L, L), 1)
    low = ri >= ci
    upp = ri <= ci
    low_f = low.astype(F32)
    upp_f = upp.astype(F32)
    ones_col = (lax.broadcasted_iota(jnp.int32, (L, dk), 1) == 0).astype(BF16)
    hi = lax.Precision.HIGHEST

    for dr, (q_ref, k_ref, v_ref, g_ref, gt_ref, h_ref) in enumerate(
            ((qf, kf, vf, gf, gtf, hf), (qb, kb, vb, gb, gtb, hb))):
        G = g_ref[...] + brow[...]
        GT = gt_ref[...] + bcol[...]
        ic_col = G[:, dr * mh:(dr + 1) * mh]
        lf_col = jax.nn.log_sigmoid(G[:, (2 + dr) * mh:(3 + dr) * mh])
        ic_row = GT[dr * mh:(dr + 1) * mh, :]
        lf_row = jax.nn.log_sigmoid(GT[(2 + dr) * mh:(3 + dr) * mh, :])
        causal = low if dr == 0 else upp
        b_col = jnp.dot(low_f if dr == 0 else upp_f, lf_col, precision=hi, preferred_element_type=F32)
        b_row = jnp.dot(lf_row, upp_f if dr == 0 else low_f, precision=hi, preferred_element_type=F32)
        for h in range(mh):
            r = dr * mh + h
            q = q_ref[:, h * dk:(h + 1) * dk]
            k = k_ref[:, h * dk:(h + 1) * dk]
            v = v_ref[:, h * dk:(h + 1) * dk]
            bc = b_col[:, h:h + 1]
            d = jnp.where(causal, bc - b_row[h:h + 1, :] + ic_row[h:h + 1, :], -jnp.inf)
            m_prev = m_scr[r:r + 1, 0:1]
            inter = bc + m_prev
            m_t = jnp.maximum(inter, jnp.max(d, axis=-1, keepdims=True))
            sm = _dot_nt(q, k) * scale * jnp.exp(d - m_t)
            w_inter = jnp.exp(inter - m_t)
            v_aug = jnp.concatenate([v, ones_col], axis=1)
            S = s_scr[r]
            R = _dot(sm.astype(BF16), v_aug) + w_inter * _dot(q, S.astype(BF16))
            den = jnp.maximum(jnp.abs(R[:, dk:dk + 1]), jnp.exp(-m_t))
            h_ref[:, h * dk:(h + 1) * dk] = (R[:, :dk] / den).astype(BF16)
            b_last = bc[L - 1:L, :] if dr == 0 else bc[0:1, :]
            g = b_last - bc + ic_col[:, h:h + 1]
            m_new = jnp.maximum(b_last + m_prev, jnp.max(g, axis=0, keepdims=True))
            wk = jnp.exp(g - m_new) * scale
            wc = jnp.exp(b_last + m_prev - m_new)
            kw_t = (k.astype(F32) * wk).T.astype(BF16)
            s_scr[r] = wc * S + _dot(kw_t, v_aug)
            m_scr[r:r + 1, :] = jnp.broadcast_to(m_new, (1, m_scr.shape[1]))

    @pl.when(la[s] == 1)
    def _():
        s_out[0] = s_scr[...]
        m_out[0] = m_scr[...]


def _mlstm(z, gates, gates_t, brow, bcol, s0, m0, steps, *, mh, dk, qcol, kcol, vcol):
    t_all = z.shape[0]
    L = M_CHUNK
    mw = mh * dk
    ng = gates.shape[1]
    nseq = s0.shape[0]
    fb, bb, sq, fi, la = steps
    nsteps = fb.shape[0]

    def zspec(which, col):
        return pl.BlockSpec((L, mw), lambda s, fb, bb, sq, fi, la: ((fb, bb)[which][s], col))

    def gspec(which):
        return pl.BlockSpec((L, ng), lambda s, fb, bb, sq, fi, la: ((fb, bb)[which][s], 0))

    def gtspec(which):
        return pl.BlockSpec((ng, L), lambda s, fb, bb, sq, fi, la: (0, (fb, bb)[which][s]))

    grid_spec = pltpu.PrefetchScalarGridSpec(
        num_scalar_prefetch=5,
        grid=(nsteps,),
        in_specs=[zspec(0, qcol), zspec(0, kcol), zspec(0, vcol),
                  zspec(1, qcol), zspec(1, kcol), zspec(1, vcol),
                  gspec(0), gspec(1), gtspec(0), gtspec(1),
                  pl.BlockSpec((1, ng), lambda s, *_: (0, 0)),
                  pl.BlockSpec((ng, 1), lambda s, *_: (0, 0)),
                  pl.BlockSpec((1, 2 * mh, dk, 2 * dk), lambda s, fb, bb, sq, fi, la: (sq[s], 0, 0, 0)),
                  pl.BlockSpec((1, 2 * mh, LANES_V7X), lambda s, fb, bb, sq, fi, la: (sq[s], 0, 0))],
        out_specs=(pl.BlockSpec((L, mw), lambda s, fb, bb, sq, fi, la: (fb[s], 0)),
                   pl.BlockSpec((L, mw), lambda s, fb, bb, sq, fi, la: (bb[s], 0)),
                   pl.BlockSpec((1, 2 * mh, dk, 2 * dk), lambda s, fb, bb, sq, fi, la: (sq[s], 0, 0, 0)),
                   pl.BlockSpec((1, 2 * mh, LANES_V7X), lambda s, fb, bb, sq, fi, la: (sq[s], 0, 0))),
        scratch_shapes=[pltpu.VMEM((2 * mh, dk, 2 * dk), F32), pltpu.VMEM((2 * mh, LANES_V7X), F32)],
    )
    return pl.pallas_call(
        functools.partial(_mlstm_kernel, mh=mh, dk=dk, chunk=L),
        out_shape=(jax.ShapeDtypeStruct((t_all, mw), BF16),
                   jax.ShapeDtypeStruct((t_all, mw), BF16),
                   jax.ShapeDtypeStruct((nseq, 2 * mh, dk, 2 * dk), F32),
                   jax.ShapeDtypeStruct((nseq, 2 * mh, LANES_V7X), F32)),
        grid_spec=grid_spec,
        compiler_params=_params(("arbitrary",)),
        name="mlstm",
    )(fb, bb, sq, fi, la, z, z, z, z, z, z, gates, gates, gates_t, gates_t, brow, bcol, s0, m0)


def _sink_column(sink_ref, kv, groups, rows_per_group):
    row_g = lax.broadcasted_iota(jnp.int32, (groups * rows_per_group, 1), 0) // rows_per_group
    col = jnp.full((groups * rows_per_group, 1), sink_ref[kv * groups], F32)
    for g in range(1, groups):
        col = jnp.where(row_g == g, sink_ref[kv * groups + g], col)
    return col


def _softmax_av(scores, values, sink_col):
    mx = sink_col
    for s in scores:
        mx = jnp.maximum(mx, jnp.max(s, axis=-1, keepdims=True))
    den = jnp.exp(sink_col - mx)
    out = None
    for s, v in zip(scores, values):
        p = jnp.exp(s - mx)
        den = den + jnp.sum(p, axis=-1, keepdims=True)
        pv = _dot(p.astype(BF16), v)
        out = pv if out is None else out + pv
    return out / den


def _ctx_attn_kernel(sink_ref, q_ref, k_ref, v_ref, o_ref, *, kvh, groups, hd):
    S = q_ref.shape[0]
    scale = hd ** -0.5
    for kv in range(kvh):
        k = k_ref[:, kv * hd:(kv + 1) * hd]
        v = v_ref[:, kv * hd:(kv + 1) * hd]
        q = jnp.concatenate([q_ref[:, (kv * groups + g) * hd:(kv * groups + g + 1) * hd]
                             for g in range(groups)], axis=0)
        o = _softmax_av([_dot_nt(q, k) * scale], [v], _sink_column(sink_ref, kv, groups, S))
        for g in range(groups):
            o_ref[:, (kv * groups + g) * hd:(kv * groups + g + 1) * hd] = o[g * S:(g + 1) * S].astype(BF16)


def _ctx_attention(sink, z, *, batch, seq, kvh, groups, hd, kcol, vcol):
    qw = kvh * groups * hd
    kw = kvh * hd
    return pl.pallas_call(
        functools.partial(_ctx_attn_kernel, kvh=kvh, groups=groups, hd=hd),
        out_shape=jax.ShapeDtypeStruct((batch * seq, qw), BF16),
        grid=(batch,),
        in_specs=[pl.BlockSpec(memory_space=pltpu.SMEM),
                  pl.BlockSpec((seq, qw), lambda b: (b, 0)),
                  pl.BlockSpec((seq, kw), lambda b: (b, kcol)),
                  pl.BlockSpec((seq, kw), lambda b: (b, vcol))],
        out_specs=pl.BlockSpec((seq, qw), lambda b: (b, 0)),
        compiler_params=_params(("arbitrary",)),
        name="ctx_attention",
    )(sink, z, z, z)


def _lat_attn_kernel(sink_ref, q_ref, kp_ref, kc_ref, kn_ref, vp_ref, vc_ref, vn_ref, ck_ref, cv_ref, o_ref,
                     *, kvh, groups, hd):
    j = pl.program_id(1)
    nb = pl.num_programs(1)
    Q = q_ref.shape[0]
    scale = hd ** -0.5
    R = groups * Q
    rq = lax.broadcasted_iota(jnp.int32, (R, Q), 0) % Q
    cc = lax.broadcasted_iota(jnp.int32, (R, Q), 1)
    mask_prev = jnp.logical_and(cc >= rq, j > 0)
    mask_next = jnp.logical_and(cc <= rq, j < nb - 1)
    for kv in range(kvh):
        sl = slice(kv * hd, (kv + 1) * hd)
        q = jnp.concatenate([q_ref[:, (kv * groups + g) * hd:(kv * groups + g + 1) * hd]
                             for g in range(groups)], axis=0)
        s_prev = jnp.where(mask_prev, _dot_nt(q, kp_ref[:, sl]) * scale, -jnp.inf)
        s_cur = _dot_nt(q, kc_ref[:, sl]) * scale
        s_next = jnp.where(mask_next, _dot_nt(q, kn_ref[:, sl]) * scale, -jnp.inf)
        s_ctx = _dot_nt(q, ck_ref[0, 0, :, sl].astype(BF16)) * scale
        o = _softmax_av([s_prev, s_cur, s_next, s_ctx],
                        [vp_ref[:, sl], vc_ref[:, sl], vn_ref[:, sl], cv_ref[0, 0, :, sl].astype(BF16)],
                        _sink_column(sink_ref, kv, groups, Q))
        for g in range(groups):
            o_ref[:, (kv * groups + g) * hd:(kv * groups + g + 1) * hd] = o[g * Q:(g + 1) * Q].astype(BF16)


def _lat_attention(sink, z, cache_k, cache_v, *, t_ctx, dec_batch, dec_seq, kvh, groups, hd, kcol, vcol):
    assert WINDOW == Q_BLOCK
    Q = Q_BLOCK
    nb = dec_seq // Q
    base = t_ctx // Q
    qw = kvh * groups * hd
    kw = kvh * hd
    past = cache_k.shape[2]

    def kvspec(col, shift):
        return pl.BlockSpec((Q, kw), lambda b, j: (base + b * nb + jnp.clip(j + shift, 0, nb - 1), col))

    cspec = pl.BlockSpec((1, 1, past, kw), lambda b, j: (b, 0, 0, 0))
    return pl.pallas_call(
        functools.partial(_lat_attn_kernel, kvh=kvh, groups=groups, hd=hd),
        out_shape=jax.ShapeDtypeStruct((dec_batch * dec_seq, qw), BF16),
        grid=(dec_batch, nb),
        in_specs=[pl.BlockSpec(memory_space=pltpu.SMEM),
                  pl.BlockSpec((Q, qw), lambda b, j: (base + b * nb + j, 0)),
                  kvspec(kcol, -1), kvspec(kcol, 0), kvspec(kcol, 1),
                  kvspec(vcol, -1), kvspec(vcol, 0), kvspec(vcol, 1),
                  cspec, cspec],
        out_specs=pl.BlockSpec((Q, qw), lambda b, j: (b * nb + j, 0)),
        compiler_params=_params(("arbitrary", "arbitrary")),
        name="lat_attention",
    )(sink, z, z, z, z, z, z, z, cache_k, cache_v)


def _outproj_kernel(hf_ref, hb_ref, om_ref, hac_ref, hal_ref, gm_ref, ga_ref, xp_ref, xs_ref, mod_ref,
                    mn_ref, n2_ref, wm_ref, wa_ref, wo_ref, rw_ref,
                    x1_ref, h2_ref, lg_ref, *, nctx_tiles, mh, dv):
    i = pl.program_id(0)
    is_ctx = i < nctx_tiles
    hm = hf_ref[...].astype(F32) + hb_ref[...].astype(F32)
    parts = []
    for h in range(mh):
        sl = hm[:, h * dv:(h + 1) * dv]
        parts.append(sl * lax.rsqrt(jnp.mean(sl * sl, axis=-1, keepdims=True) + EPS))
    hmn = jnp.concatenate(parts, axis=1) * mn_ref[...] * jax.nn.sigmoid(om_ref[...].astype(F32))
    ha = jnp.where(is_ctx, hac_ref[...], hal_ref[...])
    y = (jax.nn.sigmoid(gm_ref[...].astype(F32)) * _dot(hmn.astype(BF16), wm_ref[...])
         + jax.nn.sigmoid(ga_ref[...].astype(F32)) * _dot(ha, wa_ref[...]))
    x = jnp.where(is_ctx, xp_ref[...], xs_ref[...])
    x1 = x + mod_ref[0, 2:3, :] * _dot(y.astype(BF16), wo_ref[...])
    x1_ref[...] = x1
    n = x1 * lax.rsqrt(jnp.mean(x1 * x1, axis=-1, keepdims=True) + EPS) * n2_ref[...]
    h2 = n * (1.0 + mod_ref[0, 4:5, :]) + mod_ref[0, 3:4, :]
    h2_ref[...] = h2
    lg_ref[...] = lax.dot_general(rw_ref[...], h2, _NT, precision=lax.Precision.HIGHEST,
                                  preferred_element_type=F32)


def _outproj(hf, hb, z, ha_ctx, ha_lat, x_prompt2, x_sample2, mods, mnorm, n2, wm, wa, wo, rw_t,
             *, t_ctx, dec_seq, mh, dv, omcol, gmcol, gacol):
    t_all = hf.shape[0]
    D = x_prompt2.shape[1]
    mw = mh * dv
    qw = ha_ctx.shape[1]
    E = rw_t.shape[0]
    tm = min(TM_OUTPROJ, t_ctx, dec_seq)
    nctx = t_ctx // tm
    per_seq = dec_seq // tm

    def ctx_blk(i):
        return (jnp.minimum(i, nctx - 1), 0)

    def lat_blk(i):
        return (jnp.maximum(i - nctx, 0), 0)

    def mod_row(i):
        return (jnp.where(i < nctx, 0, 1 + (i - nctx) // per_seq), 0, 0)

    const = lambda i: (0, 0)
    single = pl.Buffered(1)
    return pl.pallas_call(
        functools.partial(_outproj_kernel, nctx_tiles=nctx, mh=mh, dv=dv),
        out_shape=(jax.ShapeDtypeStruct((t_all, D), F32),
                   jax.ShapeDtypeStruct((t_all, D), F32),
                   jax.ShapeDtypeStruct((E, t_all), F32)),
        grid=(t_all // tm,),
        in_specs=[pl.BlockSpec((tm, mw), lambda i: (i, 0)),
                  pl.BlockSpec((tm, mw), lambda i: (i, 0)),
                  pl.BlockSpec((tm, mw), lambda i: (i, omcol)),
                  pl.BlockSpec((tm, qw), ctx_blk),
                  pl.BlockSpec((tm, qw), lat_blk),
                  pl.BlockSpec((tm, D), lambda i: (i, gmcol)),
                  pl.BlockSpec((tm, D), lambda i: (i, gacol)),
                  pl.BlockSpec((tm, D), ctx_blk),
                  pl.BlockSpec((tm, D), lat_blk),
                  pl.BlockSpec((1, N_MOD, D), mod_row),
                  pl.BlockSpec((1, mw), const),
                  pl.BlockSpec((1, D), const),
                  pl.BlockSpec((mw, D), const, pipeline_mode=single),
                  pl.BlockSpec((qw, D), const, pipeline_mode=single),
                  pl.BlockSpec((D, D), const, pipeline_mode=single),
                  pl.BlockSpec((E, D), const, pipeline_mode=single)],
        out_specs=(pl.BlockSpec((tm, D), lambda i: (i, 0)),
                   pl.BlockSpec((tm, D), lambda i: (i, 0)),
                   pl.BlockSpec((E, tm), lambda i: (0, i))),
        compiler_params=_params(("arbitrary",)),
        name="outproj",
    )(hf, hb, z, ha_ctx, ha_lat, z, z, x_prompt2, x_sample2, mods, mnorm, n2, wm, wa, wo, rw_t)


def _moe_kernel(blk_e, nact_ref, tok_ref, tokn_ref, dst_ref, wt_ref, w1_ref, w3_ref, w2_ref, h2_hbm,
                y_hbm, xbuf, ybuf, gsem, ssem, *, rows):
    i = pl.program_id(0)
    nact = nact_ref[0]
    slot = i % 2

    def gather(idx_ref, s):
        for r in range(rows):
            pltpu.make_async_copy(h2_hbm.at[pl.ds(idx_ref[0, 0, r], 1)], xbuf.at[s, pl.ds(r, 1)],
                                  gsem.at[s]).start()

    def gather_wait(s):
        for r in range(rows):
            pltpu.make_async_copy(h2_hbm.at[pl.ds(0, 1)], xbuf.at[s, pl.ds(r, 1)], gsem.at[s]).wait()

    def scatter_wait():
        for r in range(rows):
            pltpu.make_async_copy(ybuf.at[pl.ds(r, 1)], y_hbm.at[pl.ds(0, 1)], ssem.at[0]).wait()

    @pl.when(i == 0)
    def _():
        gather(tok_ref, 0)

    @pl.when(i < nact)
    def _():
        gather_wait(slot)

        @pl.when(i + 1 < nact)
        def _():
            gather(tokn_ref, 1 - slot)

        x = xbuf[slot].astype(BF16)
        hmid = (jax.nn.silu(_dot(x, w1_ref[0])) * _dot(x, w3_ref[0])).astype(BF16)
        y = _dot(hmid, w2_ref[0]) * wt_ref[...]

        @pl.when(i > 0)
        def _():
            scatter_wait()

        ybuf[...] = y
        for r in range(rows):
            pltpu.make_async_copy(ybuf.at[pl.ds(r, 1)], y_hbm.at[pl.ds(dst_ref[0, 0, r], 1)], ssem.at[0]).start()

        @pl.when(i == nact - 1)
        def _():
            scatter_wait()

    @pl.when(i >= nact)
    def _():
        ybuf[...] = jnp.zeros_like(ybuf)
        fill = pltpu.make_async_copy(ybuf, y_hbm.at[pl.ds(i * rows, rows)], ssem.at[0])
        fill.start()
        fill.wait()


def _moe(blk_e, nact, slot_tok, slot_dst, slot_w, w1, w3, w2, h2, *, rows):
    nblk = blk_e.shape[0]
    npad = nblk * rows
    E, D, F = w1.shape
    tok3 = slot_tok.reshape(nblk, 1, rows)
    dst3 = slot_dst.reshape(nblk, 1, rows)
    smem_blk = lambda f: pl.BlockSpec((1, 1, rows), f, memory_space=pltpu.SMEM)
    grid_spec = pltpu.PrefetchScalarGridSpec(
        num_scalar_prefetch=2,
        grid=(nblk,),
        in_specs=[smem_blk(lambda i, be, na: (i, 0, 0)),
                  smem_blk(lambda i, be, na: (jnp.minimum(i + 1, nblk - 1), 0, 0)),
                  smem_blk(lambda i, be, na: (i, 0, 0)),
                  pl.BlockSpec((rows, 1), lambda i, be, na: (i, 0)),
                  pl.BlockSpec((1, D, F), lambda i, be, na: (be[i], 0, 0)),
                  pl.BlockSpec((1, D, F), lambda i, be, na: (be[i], 0, 0)),
                  pl.BlockSpec((1, F, D), lambda i, be, na: (be[i], 0, 0)),
                  pl.BlockSpec(memory_space=pl.ANY)],
        out_specs=pl.BlockSpec(memory_space=pl.ANY),
        scratch_shapes=[pltpu.VMEM((2, rows, D), F32), pltpu.VMEM((rows, D), F32),
                        pltpu.SemaphoreType.DMA((2,)), pltpu.SemaphoreType.DMA((1,))],
    )
    return pl.pallas_call(
        functools.partial(_moe_kernel, rows=rows),
        out_shape=jax.ShapeDtypeStruct((npad, D), F32),
        grid_spec=grid_spec,
        compiler_params=_params(("arbitrary",)),
        name="routed_experts",
    )(blk_e, nact, tok3, tok3, dst3, slot_w.reshape(npad, 1), w1, w3, w2, h2)


def _route(logits_t, router_b):
    scores = jax.nn.sigmoid(logits_t.T)
    T, E = scores.shape
    biased = scores + router_b.astype(F32)
    grp = lax.top_k(biased.reshape(T, N_GROUPS, E // N_GROUPS), 2)[0].sum(-1)
    _, g_idx = lax.top_k(grp, TOPK_GROUPS)
    g_mask = jax.nn.one_hot(g_idx, N_GROUPS, dtype=F32).sum(1) > 0
    e_mask = jnp.repeat(g_mask, E // N_GROUPS, axis=1)
    _, e_idx = lax.top_k(jnp.where(e_mask, biased, -jnp.inf), TOP_K)
    w = jnp.take_along_axis(scores, e_idx, axis=1)
    w = w / jnp.sum(w, axis=-1, keepdims=True) * ROUTED_SCALE
    return e_idx.astype(jnp.int32), w


def _dispatch(e_idx, wts, n_experts, rows):
    T = e_idx.shape[0]
    A = T * TOP_K
    flat_e = e_idx.reshape(-1)
    order = jnp.argsort(flat_e, stable=True).astype(jnp.int32)
    counts = jnp.bincount(flat_e, length=n_experts).astype(jnp.int32)
    padded = (counts + rows - 1) // rows * rows
    start_sorted = jnp.cumsum(counts) - counts
    end_pad = jnp.cumsum(padded)
    start_pad = end_pad - padded
    nblk = -(-(A + n_experts * (rows - 1)) // rows)
    npad = nblk * rows
    nact = (end_pad[-1] // rows).astype(jnp.int32)
    blk_start = jnp.arange(nblk, dtype=jnp.int32) * rows
    blk_e = jnp.minimum(jnp.searchsorted(end_pad, blk_start, side='right'), n_experts - 1).astype(jnp.int32)
    blk_e = jnp.where(jnp.arange(nblk) < nact, blk_e, blk_e[jnp.maximum(nact - 1, 0)])
    slot = jnp.arange(npad, dtype=jnp.int32)
    e_s = jnp.repeat(blk_e, rows)
    r = slot - start_pad[e_s]
    valid = jnp.logical_and(r < counts[e_s], slot < end_pad[-1])
    a = order[jnp.clip(start_sorted[e_s] + r, 0, A - 1)]
    tok = a // TOP_K
    kk = a % TOP_K
    n_valid_before = jnp.where(slot < end_pad[-1], start_sorted[e_s] + jnp.minimum(r, counts[e_s]), A)
    slot_tok = jnp.where(valid, tok, 0)
    slot_dst = jnp.where(valid, kk * T + tok, A + slot - n_valid_before)
    slot_w = jnp.where(valid, wts.reshape(-1)[a], 0.0)
    return blk_e, nact.reshape(1), slot_tok, slot_dst, slot_w


def _final_kernel(*refs, top_k):
    h2_ref, x1_ref = refs[0], refs[1]
    y_refs = refs[2:2 + top_k]
    mod_ref, w1_ref, w3_ref, w2_ref, fn_ref, o_ref = refs[2 + top_k:]
    x = h2_ref[...].astype(BF16)
    hmid = (jax.nn.silu(_dot(x, w1_ref[...])) * _dot(x, w3_ref[...])).astype(BF16)
    moe = y_refs[0][...]
    for yr in y_refs[1:]:
        moe = moe + yr[...]
    moe = moe + _dot(hmid, w2_ref[...])
    x2 = x1_ref[...] + mod_ref[0, 5:6, :] * moe
    o_ref[...] = x2 * lax.rsqrt(jnp.mean(x2 * x2, axis=-1, keepdims=True) + EPS) * fn_ref[...]


def _final(h2, x1, y6, mods, sw1, sw3, sw2, fnorm, *, t_ctx, dec_seq):
    t_all, D = h2.shape
    F = sw1.shape[1]
    tm = min(TM_FINAL, t_ctx, dec_seq)
    nctx = t_ctx // tm
    per_seq = dec_seq // tm
    nt = t_all // tm

    def mod_row(i):
        return (jnp.where(i < nctx, 0, 1 + (i - nctx) // per_seq), 0, 0)

    const = lambda i: (0, 0)
    single = pl.Buffered(1)
    yspecs = [pl.BlockSpec((tm, D), functools.partial(lambda i, k: (k * nt + i, 0), k=k)) for k in range(TOP_K)]
    return pl.pallas_call(
        functools.partial(_final_kernel, top_k=TOP_K),
        out_shape=jax.ShapeDtypeStruct((t_all, D), F32),
        grid=(nt,),
        in_specs=[pl.BlockSpec((tm, D), lambda i: (i, 0)),
                  pl.BlockSpec((tm, D), lambda i: (i, 0))] + yspecs + [
                  pl.BlockSpec((1, N_MOD, D), mod_row),
                  pl.BlockSpec((D, F), const, pipeline_mode=single),
                  pl.BlockSpec((D, F), const, pipeline_mode=single),
                  pl.BlockSpec((F, D), const, pipeline_mode=single),
                  pl.BlockSpec((1, D), const)],
        out_specs=pl.BlockSpec((tm, D), lambda i: (i, 0)),
        compiler_params=_params(("arbitrary",)),
        name="shared_combine_final",
    )(h2, x1, *([y6] * TOP_K), mods, sw1, sw3, sw2, fnorm)


def _rope_tables(dec_seq, hd):
    nf = hd // 4
    t = jnp.arange(dec_seq)
    inv = ROPE_BASE ** (-jnp.arange(nf, dtype=F32) / nf)
    ang_r = (t // GRID_W).astype(F32)[:, None] * inv
    ang_c = (t % GRID_W).astype(F32)[:, None] * inv
    cos = jnp.concatenate([jnp.cos(ang_r)] * 2 + [jnp.cos(ang_c)] * 2, axis=1)
    sin = jnp.concatenate([-jnp.sin(ang_r), jnp.sin(ang_r), -jnp.sin(ang_c), jnp.sin(ang_c)], axis=1)
    return cos, sin


def _scan_steps(batch, seq, dec_batch, dec_seq, L):
    fb, bb, sq, fi, la = [], [], [], [], []
    base = 0
    for sid, S in enumerate([seq] * batch + [dec_seq] * dec_batch):
        nc = S // L
        for c in range(nc):
            fb.append(base + c)
            bb.append(base + nc - 1 - c)
            sq.append(sid)
            fi.append(int(c == 0))
            la.append(int(c == nc - 1))
        base += nc
    return tuple(jnp.asarray(np.asarray(a, dtype=np.int32)) for a in (fb, bb, sq, fi, la))


def kernel(x_prompt, x_sample, cache_k, cache_v, state_mlstm_C, state_mlstm_n, state_mlstm_m, c, c_ctx,
           w_mod, b_mod, norm1_w, norm2_w, w_in, igate_b, fgate_b, mlstm_norm_w, attn_sink,
           w_branch_m, w_branch_a, w_out, router_w, router_b, expert_w1, expert_w3, expert_w2,
           shared_w1, shared_w3, shared_w2, final_norm_w):
    batch, seq, D = x_prompt.shape
    dec_batch, dec_seq, _ = x_sample.shape
    depth = w_in.shape[0]
    assert depth == 1, "single trunk layer"
    _, _, past, kvh, hd = cache_k.shape
    mh, dk, dv = state_mlstm_C.shape[3:]
    ah = attn_sink.shape[1]
    groups = ah // kvh
    E = router_w.shape[2]
    assert dk == dv == hd == LANES_V7X
    t_ctx, t_lat = batch * seq, dec_batch * dec_seq
    mw, qw, kw = mh * dk, ah * hd, kvh * hd
    ng = 4 * mh

    wi = w_in[0]
    o = 0
    seg = {}
    for name, width in (("qm", mw), ("km", mw), ("vm", mw), ("om", mw), ("im", 2 * mh), ("fm", 2 * mh),
                        ("qa", qw), ("ka", kw), ("va", kw), ("gm", D), ("ga", D)):
        seg[name] = wi[:, o:o + width]
        o += width
    order = ("qa", "gm", "ga", "qm", "km", "vm", "om", "ka", "va")
    w_main = jnp.concatenate([seg[nm] for nm in order], axis=1).astype(BF16)
    col = {}
    o = 0
    for nm in order:
        col[nm] = o
        o += seg[nm].shape[1]
    tn = 2 * kw
    for nm in order[:-2]:
        assert col[nm] % tn == 0 and seg[nm].shape[1] % tn == 0
    for nm, width in (("gm", D), ("ga", D), ("qm", mw), ("km", mw), ("vm", mw), ("om", mw), ("ka", kw), ("va", kw)):
        assert col[nm] % width == 0
    w_gate = jnp.pad(jnp.concatenate([seg["im"], seg["fm"]], axis=1), ((0, 0), (0, LANES_V7X - ng))).astype(BF16)

    R = -(-(1 + dec_batch) // 8) * 8
    cond = jnp.concatenate([c_ctx[None, :], c, jnp.zeros((R - 1 - dec_batch, D), F32)], axis=0)
    mods = _modulation(cond, w_mod[0], b_mod[0]).reshape(R, N_MOD, D)

    xp2 = x_prompt.reshape(t_ctx, D)
    xs2 = x_sample.reshape(t_lat, D)
    cos, sin = _rope_tables(dec_seq, hd)
    z, kv32, gates, gates_t = _inproj(xp2, xs2, mods, norm1_w, w_main, w_gate, cos, sin, t_ctx=t_ctx,
                                      dec_seq=dec_seq, tn=tn, n_rope_tiles=qw // tn, n_gates=ng)

    nseq = batch + dec_batch
    C0 = jnp.concatenate([jnp.zeros((batch, 2, mh, dk, dv), F32), state_mlstm_C[:, 0]], axis=0)
    n0 = jnp.concatenate([jnp.zeros((batch, 2, mh, dk), F32), state_mlstm_n[:, 0]], axis=0)
    m0 = jnp.concatenate([jnp.zeros((batch, 2, mh), F32), state_mlstm_m[:, 0]], axis=0)
    s0 = jnp.concatenate([C0, n0[..., None], jnp.zeros((nseq, 2, mh, dk, dv - 1), F32)], axis=-1)
    s0 = s0.reshape(nseq, 2 * mh, dk, 2 * dv)
    m0 = jnp.broadcast_to(m0.reshape(nseq, 2 * mh, 1), (nseq, 2 * mh, LANES_V7X))
    gate_b = jnp.concatenate([igate_b[0].reshape(-1), fgate_b[0].reshape(-1)])
    steps = _scan_steps(batch, seq, dec_batch, dec_seq, M_CHUNK)
    hf, hb, s_fin, m_fin = _mlstm(z, gates, gates_t, gate_b.reshape(1, ng), gate_b.reshape(ng, 1), s0, m0, steps,
                                  mh=mh, dk=dk, qcol=col["qm"] // mw, kcol=col["km"] // mw, vcol=col["vm"] // mw)

    sink = attn_sink[0]
    ha_ctx = _ctx_attention(sink, z, batch=batch, seq=seq, kvh=kvh, groups=groups, hd=hd,
                            kcol=col["ka"] // kw, vcol=col["va"] // kw)
    ha_lat = _lat_attention(sink, z, cache_k.reshape(dec_batch, depth, past, kw),
                            cache_v.reshape(dec_batch, depth, past, kw), t_ctx=t_ctx, dec_batch=dec_batch,
                            dec_seq=dec_seq, kvh=kvh, groups=groups, hd=hd, kcol=col["ka"] // kw,
                            vcol=col["va"] // kw)

    x1, h2, logits_t = _outproj(hf, hb, z, ha_ctx, ha_lat, xp2, xs2, mods, mlstm_norm_w, norm2_w,
                                w_branch_m[0].astype(BF16), w_branch_a[0].astype(BF16), w_out[0].astype(BF16),
                                router_w[0].T, t_ctx=t_ctx, dec_seq=dec_seq, mh=mh, dv=dv,
                                omcol=col["om"] // mw, gmcol=col["gm"] // D, gacol=col["ga"] // D)

    e_idx, wts = _route(logits_t, router_b[0])
    blk_e, nact, slot_tok, slot_dst, slot_w = _dispatch(e_idx, wts, E, EXPERT_ROWS)
    y6 = _moe(blk_e, nact, slot_tok, slot_dst, slot_w, expert_w1[0].astype(BF16), expert_w3[0].astype(BF16),
              expert_w2[0].astype(BF16), h2, rows=EXPERT_ROWS)

    y = _final(h2, x1, y6, mods, shared_w1[0].astype(BF16), shared_w3[0].astype(BF16), shared_w2[0].astype(BF16),
               final_norm_w.reshape(1, D), t_ctx=t_ctx, dec_seq=dec_seq)

    y_prompt = y[:t_ctx].reshape(batch, seq, D)
    y_sample = y[t_ctx:].reshape(dec_batch, dec_seq, D)
    new_k = kv32[:t_ctx, :kw].reshape(batch, 1, seq, kvh, hd)
    new_v = kv32[:t_ctx, kw:].reshape(batch, 1, seq, kvh, hd)
    s_ctx = s_fin[:batch].reshape(batch, 1, 2, mh, dk, 2 * dv)
    new_C = s_ctx[..., :dv]
    new_n = s_ctx[..., dv]
    new_m = m_fin[:batch, :, 0].reshape(batch, 1, 2, mh)
    return y_prompt, y_sample, new_k, new_v, new_C, new_n, new_m
```

```python
import functools

import numpy as np
import jax
import jax.numpy as jnp
from jax import lax
from jax.experimental import pallas as pl
from jax.experimental.pallas import tpu as pltpu

TOP_K = 6
N_GROUPS = 8
TOPK_GROUPS = 4
ROUTED_SCALE = 2.5
WINDOW = 128
Q_BLOCK = 128
GRID_W = 64
ROPE_BASE = 10000.0
M_CHUNK = 128
N_MOD = 6
EPS = 1e-6

LANES_V7X = 128
MXU_COLS_V7X = 256
VMEM_LIMIT_V7X = 56 * 1024 * 1024

TM_INPROJ = 512
TM_OUTPROJ = 256
TM_FINAL = 128
TN_MOD = 1024
EXPERT_ROWS = 256
TR_ROUTER = 512
TL_SLOTPOS = 2048

F32 = jnp.float32
BF16 = jnp.bfloat16
_NT = (((1,), (1,)), ((), ()))


def _params(sem):
    return pltpu.CompilerParams(dimension_semantics=sem, vmem_limit_bytes=VMEM_LIMIT_V7X)


def _dot(a, b):
    return jnp.dot(a, b, preferred_element_type=F32)


def _dot_nt(a, b):
    return lax.dot_general(a, b, _NT, preferred_element_type=F32)


def _mod_kernel(c_ref, w_ref, b_ref, o_ref):
    s = jax.nn.silu(c_ref[...]).astype(BF16)
    o_ref[...] = _dot(s, w_ref[...].astype(BF16)) + b_ref[...]


def _modulation(cond, w_mod, b_mod):
    R, D = cond.shape
    N = w_mod.shape[1]
    tn = min(TN_MOD, N)
    return pl.pallas_call(
        _mod_kernel,
        out_shape=jax.ShapeDtypeStruct((R, N), F32),
        grid=(N // tn,),
        in_specs=[pl.BlockSpec((R, D), lambda n: (0, 0)),
                  pl.BlockSpec((D, tn), lambda n: (0, n)),
                  pl.BlockSpec((1, tn), lambda n: (0, n))],
        out_specs=pl.BlockSpec((R, tn), lambda n: (0, n)),
        compiler_params=_params(("arbitrary",)),
        name="modulation",
    )(cond, w_mod, b_mod.reshape(1, N))


def _rope_slice(x, cos, sin_signed, first_half):
    swap = jnp.where(first_half, pltpu.roll(x, 96, 1), pltpu.roll(x, 32, 1))
    return x * cos + swap * sin_signed


def _inproj_kernel(xp_ref, xs_ref, mod_ref, n1_ref, w_ref, wg_ref, cos_ref, sin_ref,
                   z_ref, kv_ref, g_ref, gt_ref, h_scr, *, nctx_tiles, n_rope_tiles, kv_tile, n_gates):
    i = pl.program_id(0)
    n = pl.program_id(1)
    is_ctx = i < nctx_tiles
    tn = z_ref.shape[1]

    @pl.when(n == 0)
    def _():
        x = jnp.where(is_ctx, xp_ref[...], xs_ref[...])
        y = x * lax.rsqrt(jnp.mean(x * x, axis=-1, keepdims=True) + EPS) * n1_ref[...]
        h = (y * (1.0 + mod_ref[0, 1:2, :]) + mod_ref[0, 0:1, :]).astype(BF16)
        h_scr[...] = h
        g = _dot(h, wg_ref[...])
        g_ref[...] = g[:, :n_gates]
        gt_ref[...] = g.T[:n_gates, :]

    acc = _dot(h_scr[...], w_ref[...])

    def rope_cols(ncols):
        cos = jnp.where(is_ctx, 1.0, cos_ref[...])
        sin = jnp.where(is_ctx, 0.0, sin_ref[...])
        lane = lax.broadcasted_iota(jnp.int32, cos.shape, 1)
        first_half = (lane % 64) < 32
        return [_rope_slice(acc[:, c:c + LANES_V7X], cos, sin, first_half)
                for c in range(0, ncols, LANES_V7X)]

    @pl.when(n < n_rope_tiles)
    def _():
        z_ref[...] = jnp.concatenate(rope_cols(tn), axis=1).astype(BF16)

    @pl.when(n == kv_tile)
    def _():
        r = jnp.concatenate(rope_cols(tn // 2) + [acc[:, tn // 2:]], axis=1)
        z_ref[...] = r.astype(BF16)
        kv_ref[...] = r

    @pl.when(jnp.logical_and(n >= n_rope_tiles, n != kv_tile))
    def _():
        z_ref[...] = acc.astype(BF16)


def _inproj(x_prompt2, x_sample2, mods, n1, w_main, w_gate, cos, sin, *, t_ctx, dec_seq, tn, n_rope_tiles, n_gates):
    t_lat, D = x_sample2.shape
    t_all = t_ctx + t_lat
    tm = min(TM_INPROJ, t_ctx, dec_seq)
    nctx = t_ctx // tm
    per_seq = dec_seq // tm
    ncols = w_main.shape[1]
    ntile = ncols // tn
    kv_tile = ntile - 1

    def mod_row(i):
        return jnp.where(i < nctx, 0, 1 + (i - nctx) // per_seq)

    def pos_blk(i):
        return jnp.where(i < nctx, 0, (i - nctx) % per_seq)

    kernel = functools.partial(_inproj_kernel, nctx_tiles=nctx, n_rope_tiles=n_rope_tiles, kv_tile=kv_tile,
                               n_gates=n_gates)
    return pl.pallas_call(
        kernel,
        out_shape=(jax.ShapeDtypeStruct((t_all, ncols), BF16),
                   jax.ShapeDtypeStruct((t_all, tn), F32),
                   jax.ShapeDtypeStruct((t_all, n_gates), F32),
                   jax.ShapeDtypeStruct((n_gates, t_all), F32)),
        grid=(t_all // tm, ntile),
        in_specs=[pl.BlockSpec((tm, D), lambda i, n: (jnp.minimum(i, nctx - 1), 0)),
                  pl.BlockSpec((tm, D), lambda i, n: (jnp.maximum(i - nctx, 0), 0)),
                  pl.BlockSpec((1, N_MOD, D), lambda i, n: (mod_row(i), 0, 0)),
                  pl.BlockSpec((1, D), lambda i, n: (0, 0)),
                  pl.BlockSpec((D, tn), lambda i, n: (0, n)),
                  pl.BlockSpec((D, LANES_V7X), lambda i, n: (0, 0)),
                  pl.BlockSpec((tm, LANES_V7X), lambda i, n: (pos_blk(i), 0)),
                  pl.BlockSpec((tm, LANES_V7X), lambda i, n: (pos_blk(i), 0))],
        out_specs=(pl.BlockSpec((tm, tn), lambda i, n: (i, n)),
                   pl.BlockSpec((tm, tn), lambda i, n: (i, 0)),
                   pl.BlockSpec((tm, n_gates), lambda i, n: (i, 0)),
                   pl.BlockSpec((n_gates, tm), lambda i, n: (0, i))),
        scratch_shapes=[pltpu.VMEM((tm, D), BF16)],
        compiler_params=_params(("arbitrary", "arbitrary")),
        name="inproj",
    )(x_prompt2, x_sample2, mods, n1, w_main, w_gate, cos, sin)


def _mlstm_kernel(fb, bb, sq, fi, la,
                  qf, kf, vf, qb, kb, vb, gf, gb, gtf, gtb, brow, bcol, s0, m0,
                  hf, hb, s_out, m_out, s_scr, m_scr, *, mh, dk, chunk):
    s = pl.program_id(0)
    L = chunk
    scale = dk ** -0.5

    @pl.when(fi[s] == 1)
    def _():
        s_scr[...] = s0[0]
        m_scr[...] = m0[0]

    ri = lax.broadcasted_iota(jnp.int32, (L, L), 0)
    ci = lax.broadcasted_iota(jnp.int32, (L, L), 1)
    low = ri >= ci
    upp = ri <= ci
    low_f = low.astype(F32)
    upp_f = upp.astype(F32)
    ones_col = (lax.broadcasted_iota(jnp.int32, (L, dk), 1) == 0).astype(BF16)
    hi = lax.Precision.HIGHEST

    for dr, (q_ref, k_ref, v_ref, g_ref, gt_ref, h_ref) in enumerate(
            ((qf, kf, vf, gf, gtf, hf), (qb, kb, vb, gb, gtb, hb))):
        G = g_ref[...] + brow[...]
        GT = gt_ref[...] + bcol[...]
        ic_col = G[:, dr * mh:(dr + 1) * mh]
        lf_col = jax.nn.log_sigmoid(G[:, (2 + dr) * mh:(3 + dr) * mh])
        ic_row = GT[dr * mh:(dr + 1) * mh, :]
        lf_row = jax.nn.log_sigmoid(GT[(2 + dr) * mh:(3 + dr) * mh, :])
        causal = low if dr == 0 else upp
        b_col = jnp.dot(low_f if dr == 0 else upp_f, lf_col, precision=hi, preferred_element_type=F32)
        b_row = jnp.dot(lf_row, upp_f if dr == 0 else low_f, precision=hi, preferred_element_type=F32)
        for h in range(mh):
            r = dr * mh + h
            q = q_ref[:, h * dk:(h + 1) * dk]
            k = k_ref[:, h * dk:(h + 1) * dk]
            v = v_ref[:, h * dk:(h + 1) * dk]
            bc = b_col[:, h:h + 1]
            d = jnp.where(causal, bc - b_row[h:h + 1, :] + ic_row[h:h + 1, :], -jnp.inf)
            m_prev = m_scr[r:r + 1, 0:1]
            inter = bc + m_prev
            m_t = jnp.maximum(inter, jnp.max(d, axis=-1, keepdims=True))
            sm = _dot_nt(q, k) * scale * jnp.exp(d - m_t)
            w_inter = jnp.exp(inter - m_t)
            v_aug = jnp.concatenate([v, ones_col], axis=1)
            S = s_scr[r]
            R = _dot(sm.astype(BF16), v_aug) + w_inter * _dot(q, S.astype(BF16))
            den = jnp.maximum(jnp.abs(R[:, dk:dk + 1]), jnp.exp(-m_t))
            h_ref[:, h * dk:(h + 1) * dk] = (R[:, :dk] / den).astype(BF16)
            b_last = bc[L - 1:L, :] if dr == 0 else bc[0:1, :]
            g = b_last - bc + ic_col[:, h:h + 1]
            m_new = jnp.maximum(b_last + m_prev, jnp.max(g, axis=0, keepdims=True))
            wk = jnp.exp(g - m_new) * scale
            wc = jnp.exp(b_last + m_prev - m_new)
            kw_t = (k.astype(F32) * wk).T.astype(BF16)
            s_scr[r] = wc * S + _dot(kw_t, v_aug)
            m_scr[r:r + 1, :] = jnp.broadcast_to(m_new, (1, m_scr.shape[1]))

    @pl.when(la[s] == 1)
    def _():
        s_out[0] = s_scr[...]
        m_out[0] = m_scr[...]


def _mlstm(z, gates, gates_t, brow, bcol, s0, m0, steps, *, mh, dk, qcol, kcol, vcol):
    t_all = z.shape[0]
    L = M_CHUNK
    mw = mh * dk
    ng = gates.shape[1]
    nseq = s0.shape[0]
    fb, bb, sq, fi, la = steps
    nsteps = fb.shape[0]

    def zspec(which, col):
        return pl.BlockSpec((L, mw), lambda s, fb, bb, sq, fi, la: ((fb, bb)[which][s], col))

    def gspec(which):
        return pl.BlockSpec((L, ng), lambda s, fb, bb, sq, fi, la: ((fb, bb)[which][s], 0))

    def gtspec(which):
        return pl.BlockSpec((ng, L), lambda s, fb, bb, sq, fi, la: (0, (fb, bb)[which][s]))

    grid_spec = pltpu.PrefetchScalarGridSpec(
        num_scalar_prefetch=5,
        grid=(nsteps,),
        in_specs=[zspec(0, qcol), zspec(0, kcol), zspec(0, vcol),
                  zspec(1, qcol), zspec(1, kcol), zspec(1, vcol),
                  gspec(0), gspec(1), gtspec(0), gtspec(1),
                  pl.BlockSpec((1, ng), lambda s, *_: (0, 0)),
                  pl.BlockSpec((ng, 1), lambda s, *_: (0, 0)),
                  pl.BlockSpec((1, 2 * mh, dk, 2 * dk), lambda s, fb, bb, sq, fi, la: (sq[s], 0, 0, 0)),
                  pl.BlockSpec((1, 2 * mh, LANES_V7X), lambda s, fb, bb, sq, fi, la: (sq[s], 0, 0))],
        out_specs=(pl.BlockSpec((L, mw), lambda s, fb, bb, sq, fi, la: (fb[s], 0)),
                   pl.BlockSpec((L, mw), lambda s, fb, bb, sq, fi, la: (bb[s], 0)),
                   pl.BlockSpec((1, 2 * mh, dk, 2 * dk), lambda s, fb, bb, sq, fi, la: (sq[s], 0, 0, 0)),
                   pl.BlockSpec((1, 2 * mh, LANES_V7X), lambda s, fb, bb, sq, fi, la: (sq[s], 0, 0))),
        scratch_shapes=[pltpu.VMEM((2 * mh, dk, 2 * dk), F32), pltpu.VMEM((2 * mh, LANES_V7X), F32)],
    )
    return pl.pallas_call(
        functools.partial(_mlstm_kernel, mh=mh, dk=dk, chunk=L),
        out_shape=(jax.ShapeDtypeStruct((t_all, mw), BF16),
                   jax.ShapeDtypeStruct((t_all, mw), BF16),
                   jax.ShapeDtypeStruct((nseq, 2 * mh, dk, 2 * dk), F32),
                   jax.ShapeDtypeStruct((nseq, 2 * mh, LANES_V7X), F32)),
        grid_spec=grid_spec,
        compiler_params=_params(("arbitrary",)),
        name="mlstm",
    )(fb, bb, sq, fi, la, z, z, z, z, z, z, gates, gates, gates_t, gates_t, brow, bcol, s0, m0)


def _sink_column(sink_ref, kv, groups, rows_per_group):
    row_g = lax.broadcasted_iota(jnp.int32, (groups * rows_per_group, 1), 0) // rows_per_group
    col = jnp.full((groups * rows_per_group, 1), sink_ref[kv * groups], F32)
    for g in range(1, groups):
        col = jnp.where(row_g == g, sink_ref[kv * groups + g], col)
    return col


def _softmax_av(scores, values, sink_col):
    mx = sink_col
    for s in scores:
        mx = jnp.maximum(mx, jnp.max(s, axis=-1, keepdims=True))
    den = jnp.exp(sink_col - mx)
    out = None
    for s, v in zip(scores, values):
        p = jnp.exp(s - mx)
        den = den + jnp.sum(p, axis=-1, keepdims=True)
        pv = _dot(p.astype(BF16), v)
        out = pv if out is None else out + pv
    return out / den


def _ctx_attn_kernel(sink_ref, q_ref, k_ref, v_ref, o_ref, *, kvh, groups, hd):
    S = q_ref.shape[0]
    scale = hd ** -0.5
    for kv in range(kvh):
        k = k_ref[:, kv * hd:(kv + 1) * hd]
        v = v_ref[:, kv * hd:(kv + 1) * hd]
        q = jnp.concatenate([q_ref[:, (kv * groups + g) * hd:(kv * groups + g + 1) * hd]
                             for g in range(groups)], axis=0)
        o = _softmax_av([_dot_nt(q, k) * scale], [v], _sink_column(sink_ref, kv, groups, S))
        for g in range(groups):
            o_ref[:, (kv * groups + g) * hd:(kv * groups + g + 1) * hd] = o[g * S:(g + 1) * S].astype(BF16)


def _ctx_attention(sink, z, *, batch, seq, kvh, groups, hd, kcol, vcol):
    qw = kvh * groups * hd
    kw = kvh * hd
    return pl.pallas_call(
        functools.partial(_ctx_attn_kernel, kvh=kvh, groups=groups, hd=hd),
        out_shape=jax.ShapeDtypeStruct((batch * seq, qw), BF16),
        grid=(batch,),
        in_specs=[pl.BlockSpec(memory_space=pltpu.SMEM),
                  pl.BlockSpec((seq, qw), lambda b: (b, 0)),
                  pl.BlockSpec((seq, kw), lambda b: (b, kcol)),
                  pl.BlockSpec((seq, kw), lambda b: (b, vcol))],
        out_specs=pl.BlockSpec((seq, qw), lambda b: (b, 0)),
        compiler_params=_params(("arbitrary",)),
        name="ctx_attention",
    )(sink, z, z, z)


def _lat_attn_kernel(sink_ref, q_ref, kp_ref, kc_ref, kn_ref, vp_ref, vc_ref, vn_ref, ck_ref, cv_ref, o_ref,
                     *, kvh, groups, hd):
    j = pl.program_id(1)
    nb = pl.num_programs(1)
    Q = q_ref.shape[0]
    scale = hd ** -0.5
    R = groups * Q
    rq = lax.broadcasted_iota(jnp.int32, (R, Q), 0) % Q
    cc = lax.broadcasted_iota(jnp.int32, (R, Q), 1)
    mask_prev = jnp.logical_and(cc >= rq, j > 0)
    mask_next = jnp.logical_and(cc <= rq, j < nb - 1)
    for kv in range(kvh):
        sl = slice(kv * hd, (kv + 1) * hd)
        q = jnp.concatenate([q_ref[:, (kv * groups + g) * hd:(kv * groups + g + 1) * hd]
                             for g in range(groups)], axis=0)
        s_prev = jnp.where(mask_prev, _dot_nt(q, kp_ref[:, sl]) * scale, -jnp.inf)
        s_cur = _dot_nt(q, kc_ref[:, sl]) * scale
        s_next = jnp.where(mask_next, _dot_nt(q, kn_ref[:, sl]) * scale, -jnp.inf)
        s_ctx = _dot_nt(q, ck_ref[0, 0, :, sl].astype(BF16)) * scale
        o = _softmax_av([s_prev, s_cur, s_next, s_ctx],
                        [vp_ref[:, sl], vc_ref[:, sl], vn_ref[:, sl], cv_ref[0, 0, :, sl].astype(BF16)],
                        _sink_column(sink_ref, kv, groups, Q))
        for g in range(groups):
            o_ref[:, (kv * groups + g) * hd:(kv * groups + g + 1) * hd] = o[g * Q:(g + 1) * Q].astype(BF16)


def _lat_attention(sink, z, cache_k, cache_v, *, t_ctx, dec_batch, dec_seq, kvh, groups, hd, kcol, vcol):
    assert WINDOW == Q_BLOCK
    Q = Q_BLOCK
    nb = dec_seq // Q
    base = t_ctx // Q
    qw = kvh * groups * hd
    kw = kvh * hd
    past = cache_k.shape[2]

    def kvspec(col, shift):
        return pl.BlockSpec((Q, kw), lambda b, j: (base + b * nb + jnp.clip(j + shift, 0, nb - 1), col))

    cspec = pl.BlockSpec((1, 1, past, kw), lambda b, j: (b, 0, 0, 0))
    return pl.pallas_call(
        functools.partial(_lat_attn_kernel, kvh=kvh, groups=groups, hd=hd),
        out_shape=jax.ShapeDtypeStruct((dec_batch * dec_seq, qw), BF16),
        grid=(dec_batch, nb),
        in_specs=[pl.BlockSpec(memory_space=pltpu.SMEM),
                  pl.BlockSpec((Q, qw), lambda b, j: (base + b * nb + j, 0)),
                  kvspec(kcol, -1), kvspec(kcol, 0), kvspec(kcol, 1),
                  kvspec(vcol, -1), kvspec(vcol, 0), kvspec(vcol, 1),
                  cspec, cspec],
        out_specs=pl.BlockSpec((Q, qw), lambda b, j: (b * nb + j, 0)),
        compiler_params=_params(("arbitrary", "arbitrary")),
        name="lat_attention",
    )(sink, z, z, z, z, z, z, z, cache_k, cache_v)


def _outproj_kernel(hf_ref, hb_ref, om_ref, hac_ref, hal_ref, gm_ref, ga_ref, xp_ref, xs_ref, mod_ref,
                    mn_ref, n2_ref, wm_ref, wa_ref, wo_ref, rw_ref,
                    x1_ref, h2_ref, lg_ref, *, nctx_tiles, mh, dv):
    i = pl.program_id(0)
    is_ctx = i < nctx_tiles
    hm = hf_ref[...].astype(F32) + hb_ref[...].astype(F32)
    parts = []
    for h in range(mh):
        sl = hm[:, h * dv:(h + 1) * dv]
        parts.append(sl * lax.rsqrt(jnp.mean(sl * sl, axis=-1, keepdims=True) + EPS))
    hmn = jnp.concatenate(parts, axis=1) * mn_ref[...] * jax.nn.sigmoid(om_ref[...].astype(F32))
    ha = jnp.where(is_ctx, hac_ref[...], hal_ref[...])
    y = (jax.nn.sigmoid(gm_ref[...].astype(F32)) * _dot(hmn.astype(BF16), wm_ref[...])
         + jax.nn.sigmoid(ga_ref[...].astype(F32)) * _dot(ha, wa_ref[...]))
    x = jnp.where(is_ctx, xp_ref[...], xs_ref[...])
    x1 = x + mod_ref[0, 2:3, :] * _dot(y.astype(BF16), wo_ref[...])
    x1_ref[...] = x1
    n = x1 * lax.rsqrt(jnp.mean(x1 * x1, axis=-1, keepdims=True) + EPS) * n2_ref[...]
    h2 = n * (1.0 + mod_ref[0, 4:5, :]) + mod_ref[0, 3:4, :]
    h2_ref[...] = h2
    lg_ref[...] = lax.dot_general(rw_ref[...], h2, _NT, precision=lax.Precision.HIGHEST,
                                  preferred_element_type=F32)


def _outproj(hf, hb, z, ha_ctx, ha_lat, x_prompt2, x_sample2, mods, mnorm, n2, wm, wa, wo, rw_t,
             *, t_ctx, dec_seq, mh, dv, omcol, gmcol, gacol):
    t_all = hf.shape[0]
    D = x_prompt2.shape[1]
    mw = mh * dv
    qw = ha_ctx.shape[1]
    E = rw_t.shape[0]
    tm = min(TM_OUTPROJ, t_ctx, dec_seq)
    nctx = t_ctx // tm
    per_seq = dec_seq // tm

    def ctx_blk(i):
        return (jnp.minimum(i, nctx - 1), 0)

    def lat_blk(i):
        return (jnp.maximum(i - nctx, 0), 0)

    def mod_row(i):
        return (jnp.where(i < nctx, 0, 1 + (i - nctx) // per_seq), 0, 0)

    const = lambda i: (0, 0)
    single = pl.Buffered(1)
    return pl.pallas_call(
        functools.partial(_outproj_kernel, nctx_tiles=nctx, mh=mh, dv=dv),
        out_shape=(jax.ShapeDtypeStruct((t_all, D), F32),
                   jax.ShapeDtypeStruct((t_all, D), F32),
                   jax.ShapeDtypeStruct((E, t_all), F32)),
        grid=(t_all // tm,),
        in_specs=[pl.BlockSpec((tm, mw), lambda i: (i, 0)),
                  pl.BlockSpec((tm, mw), lambda i: (i, 0)),
                  pl.BlockSpec((tm, mw), lambda i: (i, omcol)),
                  pl.BlockSpec((tm, qw), ctx_blk),
                  pl.BlockSpec((tm, qw), lat_blk),
                  pl.BlockSpec((tm, D), lambda i: (i, gmcol)),
                  pl.BlockSpec((tm, D), lambda i: (i, gacol)),
                  pl.BlockSpec((tm, D), ctx_blk),
                  pl.BlockSpec((tm, D), lat_blk),
                  pl.BlockSpec((1, N_MOD, D), mod_row),
                  pl.BlockSpec((1, mw), const),
                  pl.BlockSpec((1, D), const),
                  pl.BlockSpec((mw, D), const, pipeline_mode=single),
                  pl.BlockSpec((qw, D), const, pipeline_mode=single),
                  pl.BlockSpec((D, D), const, pipeline_mode=single),
                  pl.BlockSpec((E, D), const, pipeline_mode=single)],
        out_specs=(pl.BlockSpec((tm, D), lambda i: (i, 0)),
                   pl.BlockSpec((tm, D), lambda i: (i, 0)),
                   pl.BlockSpec((E, tm), lambda i: (0, i))),
        compiler_params=_params(("arbitrary",)),
        name="outproj",
    )(hf, hb, z, ha_ctx, ha_lat, z, z, x_prompt2, x_sample2, mods, mnorm, n2, wm, wa, wo, rw_t)


def _moe_kernel(blk_e, nact_ref, tok_ref, tokn_ref, dst_ref, w1_ref, w3_ref, w2_ref, h2_hbm,
                y_hbm, xbuf0, xbuf1, ybuf, gsem, ssem, *, rows):
    i = pl.program_id(0)
    nact = nact_ref[0]

    def gather(idx_ref, xbuf, sem):
        for r in range(rows):
            pltpu.make_async_copy(h2_hbm.at[pl.ds(idx_ref[0, 0, r], 1)], xbuf.at[pl.ds(r, 1)], sem).start()

    def gather_wait(xbuf, sem):
        pltpu.make_async_copy(h2_hbm.at[pl.ds(0, rows)], xbuf, sem).wait()

    def scatter_wait():
        pltpu.make_async_copy(ybuf, y_hbm.at[pl.ds(0, rows)], ssem.at[0]).wait()

    @pl.when(i == 0)
    def _():
        gather(tok_ref, xbuf0, gsem.at[0])

    def block(xcur, sem_cur, xnext, sem_next):
        gather_wait(xcur, sem_cur)

        @pl.when(i + 1 < nact)
        def _():
            gather(tokn_ref, xnext, sem_next)

        x = xcur[...].astype(BF16)
        hmid = (jax.nn.silu(_dot(x, w1_ref[0])) * _dot(x, w3_ref[0])).astype(BF16)
        y = _dot(hmid, w2_ref[0])

        @pl.when(i > 0)
        def _():
            scatter_wait()

        ybuf[...] = y
        for r in range(rows):
            pltpu.make_async_copy(ybuf.at[pl.ds(r, 1)], y_hbm.at[pl.ds(dst_ref[0, 0, r], 1)], ssem.at[0]).start()

        @pl.when(i == nact - 1)
        def _():
            scatter_wait()

    @pl.when(jnp.logical_and(i < nact, i % 2 == 0))
    def _():
        block(xbuf0, gsem.at[0], xbuf1, gsem.at[1])

    @pl.when(jnp.logical_and(i < nact, i % 2 == 1))
    def _():
        block(xbuf1, gsem.at[1], xbuf0, gsem.at[0])

    @pl.when(i >= nact)
    def _():
        ybuf[...] = jnp.zeros_like(ybuf)
        fill = pltpu.make_async_copy(ybuf, y_hbm.at[pl.ds(i * rows, rows)], ssem.at[0])
        fill.start()
        fill.wait()


def _moe(blk_e, nact, slot_tok, slot_dst, w1, w3, w2, h2, *, rows):
    nblk = blk_e.shape[0]
    npad = nblk * rows
    E, D, F = w1.shape
    tok3 = slot_tok.reshape(nblk, 1, rows)
    dst3 = slot_dst.reshape(nblk, 1, rows)
    smem_blk = lambda f: pl.BlockSpec((1, 1, rows), f, memory_space=pltpu.SMEM)
    grid_spec = pltpu.PrefetchScalarGridSpec(
        num_scalar_prefetch=2,
        grid=(nblk,),
        in_specs=[smem_blk(lambda i, be, na: (i, 0, 0)),
                  smem_blk(lambda i, be, na: (jnp.minimum(i + 1, nblk - 1), 0, 0)),
                  smem_blk(lambda i, be, na: (i, 0, 0)),
                  pl.BlockSpec((1, D, F), lambda i, be, na: (be[i], 0, 0)),
                  pl.BlockSpec((1, D, F), lambda i, be, na: (be[i], 0, 0)),
                  pl.BlockSpec((1, F, D), lambda i, be, na: (be[i], 0, 0)),
                  pl.BlockSpec(memory_space=pl.ANY)],
        out_specs=pl.BlockSpec(memory_space=pl.ANY),
        scratch_shapes=[pltpu.VMEM((rows, D), F32), pltpu.VMEM((rows, D), F32), pltpu.VMEM((rows, D), F32),
                        pltpu.SemaphoreType.DMA((2,)), pltpu.SemaphoreType.DMA((1,))],
    )
    return pl.pallas_call(
        functools.partial(_moe_kernel, rows=rows),
        out_shape=jax.ShapeDtypeStruct((npad, D), F32),
        grid_spec=grid_spec,
        compiler_params=_params(("arbitrary",)),
        name="routed_experts",
    )(blk_e, nact, tok3, tok3, dst3, w1, w3, w2, h2)


def _router_kernel(lg_ref, rb_ref, eidx_ref, rank_ref, wtok_ref, cnt_ref, carry, *, top_k, n_groups, topk_groups):
    i = pl.program_id(0)

    @pl.when(i == 0)
    def _():
        carry[...] = jnp.zeros_like(carry)

    E, tr = lg_ref.shape
    gs = E // n_groups
    scores = jax.nn.sigmoid(lg_ref[...])
    biased = scores + rb_ref[...]
    b3 = biased.reshape(n_groups, gs, tr)
    io3 = lax.broadcasted_iota(jnp.int32, b3.shape, 1)
    m1 = jnp.max(b3, axis=1, keepdims=True)
    i1 = jnp.min(jnp.where(b3 == m1, io3, gs), axis=1, keepdims=True)
    m2 = jnp.max(jnp.where(io3 == i1, -jnp.inf, b3), axis=1, keepdims=True)
    grp = (m1 + m2).reshape(n_groups, tr)
    iog = lax.broadcasted_iota(jnp.int32, grp.shape, 0)
    sel = jnp.zeros(grp.shape, jnp.bool_)
    for _ in range(topk_groups):
        mx = jnp.max(grp, axis=0, keepdims=True)
        hit = iog == jnp.min(jnp.where(grp == mx, iog, n_groups), axis=0, keepdims=True)
        sel = jnp.logical_or(sel, hit)
        grp = jnp.where(hit, -jnp.inf, grp)
    masked = jnp.where(sel.reshape(n_groups, 1, tr), b3, -jnp.inf).reshape(E, tr)
    ioe = lax.broadcasted_iota(jnp.int32, (E, tr), 0)
    onehot = jnp.zeros((E, tr), F32)
    hits, idxs, ws = [], [], []
    for _ in range(top_k):
        mx = jnp.max(masked, axis=0, keepdims=True)
        ix = jnp.min(jnp.where(masked == mx, ioe, E), axis=0, keepdims=True)
        hit = ioe == ix
        hits.append(hit)
        idxs.append(ix)
        ws.append(jnp.sum(jnp.where(hit, scores, 0.0), axis=0, keepdims=True))
        onehot = onehot + hit.astype(F32)
        masked = jnp.where(hit, -jnp.inf, masked)
    wsum = ws[0]
    for w in ws[1:]:
        wsum = wsum + w
    ri = lax.broadcasted_iota(jnp.int32, (tr, tr), 0)
    ci = lax.broadcasted_iota(jnp.int32, (tr, tr), 1)
    before = _dot(onehot.astype(BF16), (ri < ci).astype(BF16)) + carry[...]
    ranks = [jnp.sum(jnp.where(hit, before, 0.0), axis=0, keepdims=True) for hit in hits]
    carry[...] = carry[...] + jnp.sum(onehot, axis=1, keepdims=True)
    cnt_ref[...] = carry[...].astype(jnp.int32)
    eidx_ref[...] = jnp.concatenate(idxs, axis=0)
    rank_ref[...] = jnp.concatenate(ranks, axis=0).astype(jnp.int32)
    wrows = jnp.concatenate([w / wsum * ROUTED_SCALE for w in ws]
                            + [jnp.zeros((LANES_V7X - top_k, tr), F32)], axis=0)
    wtok_ref[...] = wrows.T


def _router(logits_t, router_b):
    E, T = logits_t.shape
    tr = min(TR_ROUTER, T)
    return pl.pallas_call(
        functools.partial(_router_kernel, top_k=TOP_K, n_groups=N_GROUPS, topk_groups=TOPK_GROUPS),
        out_shape=(jax.ShapeDtypeStruct((TOP_K, T), jnp.int32),
                   jax.ShapeDtypeStruct((TOP_K, T), jnp.int32),
                   jax.ShapeDtypeStruct((T, LANES_V7X), F32),
                   jax.ShapeDtypeStruct((E, 1), jnp.int32)),
        grid=(T // tr,),
        in_specs=[pl.BlockSpec((E, tr), lambda i: (0, i)),
                  pl.BlockSpec((E, 1), lambda i: (0, 0))],
        out_specs=(pl.BlockSpec((TOP_K, tr), lambda i: (0, i)),
                   pl.BlockSpec((TOP_K, tr), lambda i: (0, i)),
                   pl.BlockSpec((tr, LANES_V7X), lambda i: (i, 0)),
                   pl.BlockSpec((E, 1), lambda i: (0, 0))),
        scratch_shapes=[pltpu.VMEM((E, 1), F32)],
        compiler_params=_params(("arbitrary",)),
        name="router",
    )(logits_t, router_b.reshape(E, 1))


def _slot_pos_kernel(start_ref, eidx_ref, rank_ref, pos_ref, *, n_experts):
    eidx = eidx_ref[...]
    pos = rank_ref[...]
    for e in range(n_experts):
        pos = pos + jnp.where(eidx == e, start_ref[e], 0)
    pos_ref[...] = pos


def _slot_pos(start_pad, eidx, rank):
    K, T = eidx.shape
    tl = min(TL_SLOTPOS, T)
    return pl.pallas_call(
        functools.partial(_slot_pos_kernel, n_experts=start_pad.shape[0]),
        out_shape=jax.ShapeDtypeStruct((K, T), jnp.int32),
        grid=(T // tl,),
        in_specs=[pl.BlockSpec(memory_space=pltpu.SMEM),
                  pl.BlockSpec((K, tl), lambda i: (0, i)),
                  pl.BlockSpec((K, tl), lambda i: (0, i))],
        out_specs=pl.BlockSpec((K, tl), lambda i: (0, i)),
        compiler_params=_params(("arbitrary",)),
        name="slot_pos",
    )(start_pad, eidx, rank)


def _dispatch(eidx, rank, counts, rows):
    K, T = eidx.shape
    A = K * T
    E = counts.shape[0]
    padded = (counts + rows - 1) // rows * rows
    end_sorted = jnp.cumsum(counts)
    end_pad = jnp.cumsum(padded)
    start_pad = end_pad - padded
    nblk = -(-(A + E * (rows - 1)) // rows)
    npad = nblk * rows
    nact = end_pad[-1] // rows
    blk = jnp.arange(nblk, dtype=jnp.int32)
    blk_e = jnp.sum(end_pad[None, :] <= (blk * rows)[:, None], axis=1).astype(jnp.int32)
    last_e = jnp.sum(end_pad <= (nact - 1) * rows).astype(jnp.int32)
    blk_e = jnp.where(blk < nact, blk_e, last_e)
    pos = _slot_pos(start_pad.astype(jnp.int32), eidx, rank)
    slot_a = jnp.full((npad,), -1, jnp.int32).at[pos.reshape(-1)].set(
        jnp.arange(A, dtype=jnp.int32), unique_indices=True).reshape(nblk, rows)
    valid = slot_a >= 0
    slot = blk[:, None] * rows + jnp.arange(rows, dtype=jnp.int32)[None, :]
    real_before = jnp.where(blk < nact, end_sorted[blk_e], A)[:, None]
    slot_tok = jnp.where(valid, slot_a % T, 0)
    slot_dst = jnp.where(valid, slot_a, A + slot - real_before)
    return blk_e, nact.astype(jnp.int32).reshape(1), slot_tok, slot_dst


def _final_kernel(*refs, top_k, nctx_tiles):
    h2_ref, x1_ref, wt_ref = refs[:3]
    y_refs = refs[3:3 + top_k]
    mod_ref, w1_ref, w3_ref, w2_ref, fn_ref, oc_ref, ol_ref = refs[3 + top_k:]
    i = pl.program_id(0)
    x = h2_ref[...].astype(BF16)
    hmid = (jax.nn.silu(_dot(x, w1_ref[...])) * _dot(x, w3_ref[...])).astype(BF16)
    wt = wt_ref[...]
    moe = _dot(hmid, w2_ref[...])
    for k, yr in enumerate(y_refs):
        moe = moe + yr[...] * wt[:, k:k + 1]
    x2 = x1_ref[...] + mod_ref[0, 5:6, :] * moe
    out = x2 * lax.rsqrt(jnp.mean(x2 * x2, axis=-1, keepdims=True) + EPS) * fn_ref[...]

    @pl.when(i < nctx_tiles)
    def _():
        oc_ref[...] = out

    @pl.when(i >= nctx_tiles)
    def _():
        ol_ref[...] = out


def _final(h2, x1, wtok, y6, mods, sw1, sw3, sw2, fnorm, *, t_ctx, dec_seq):
    t_all, D = h2.shape
    F = sw1.shape[1]
    tm = min(TM_FINAL, t_ctx, dec_seq)
    nctx = t_ctx // tm
    per_seq = dec_seq // tm
    nt = t_all // tm

    def mod_row(i):
        return (jnp.where(i < nctx, 0, 1 + (i - nctx) // per_seq), 0, 0)

    const = lambda i: (0, 0)
    single = pl.Buffered(1)
    yspecs = [pl.BlockSpec((tm, D), functools.partial(lambda i, k: (k * nt + i, 0), k=k)) for k in range(TOP_K)]
    return pl.pallas_call(
        functools.partial(_final_kernel, top_k=TOP_K, nctx_tiles=nctx),
        out_shape=(jax.ShapeDtypeStruct((t_ctx, D), F32), jax.ShapeDtypeStruct((t_all - t_ctx, D), F32)),
        grid=(nt,),
        in_specs=[pl.BlockSpec((tm, D), lambda i: (i, 0)),
                  pl.BlockSpec((tm, D), lambda i: (i, 0)),
                  pl.BlockSpec((tm, LANES_V7X), lambda i: (i, 0))] + yspecs + [
                  pl.BlockSpec((1, N_MOD, D), mod_row),
                  pl.BlockSpec((D, F), const, pipeline_mode=single),
                  pl.BlockSpec((D, F), const, pipeline_mode=single),
                  pl.BlockSpec((F, D), const, pipeline_mode=single),
                  pl.BlockSpec((1, D), const)],
        out_specs=(pl.BlockSpec((tm, D), lambda i: (jnp.minimum(i, nctx - 1), 0)),
                   pl.BlockSpec((tm, D), lambda i: (jnp.maximum(i - nctx, 0), 0))),
        compiler_params=_params(("arbitrary",)),
        name="shared_combine_final",
    )(h2, x1, wtok, *([y6] * TOP_K), mods, sw1, sw3, sw2, fnorm)


def _rope_tables(dec_seq, hd):
    nf = hd // 4
    t = jnp.arange(dec_seq)
    inv = ROPE_BASE ** (-jnp.arange(nf, dtype=F32) / nf)
    ang_r = (t // GRID_W).astype(F32)[:, None] * inv
    ang_c = (t % GRID_W).astype(F32)[:, None] * inv
    cos = jnp.concatenate([jnp.cos(ang_r)] * 2 + [jnp.cos(ang_c)] * 2, axis=1)
    sin = jnp.concatenate([-jnp.sin(ang_r), jnp.sin(ang_r), -jnp.sin(ang_c), jnp.sin(ang_c)], axis=1)
    return cos, sin


def _scan_steps(batch, seq, dec_batch, dec_seq, L):
    fb, bb, sq, fi, la = [], [], [], [], []
    base = 0
    for sid, S in enumerate([seq] * batch + [dec_seq] * dec_batch):
        nc = S // L
        for c in range(nc):
            fb.append(base + c)
            bb.append(base + nc - 1 - c)
            sq.append(sid)
            fi.append(int(c == 0))
            la.append(int(c == nc - 1))
        base += nc
    return tuple(jnp.asarray(np.asarray(a, dtype=np.int32)) for a in (fb, bb, sq, fi, la))


def kernel(x_prompt, x_sample, cache_k, cache_v, state_mlstm_C, state_mlstm_n, state_mlstm_m, c, c_ctx,
           w_mod, b_mod, norm1_w, norm2_w, w_in, igate_b, fgate_b, mlstm_norm_w, attn_sink,
           w_branch_m, w_branch_a, w_out, router_w, router_b, expert_w1, expert_w3, expert_w2,
           shared_w1, shared_w3, shared_w2, final_norm_w):
    batch, seq, D = x_prompt.shape
    dec_batch, dec_seq, _ = x_sample.shape
    depth = w_in.shape[0]
    assert depth == 1, "single trunk layer"
    _, _, past, kvh, hd = cache_k.shape
    mh, dk, dv = state_mlstm_C.shape[3:]
    ah = attn_sink.shape[1]
    groups = ah // kvh
    E = router_w.shape[2]
    assert dk == dv == hd == LANES_V7X
    t_ctx, t_lat = batch * seq, dec_batch * dec_seq
    mw, qw, kw = mh * dk, ah * hd, kvh * hd
    ng = 4 * mh

    wi = w_in[0]
    o = 0
    seg = {}
    for name, width in (("qm", mw), ("km", mw), ("vm", mw), ("om", mw), ("im", 2 * mh), ("fm", 2 * mh),
                        ("qa", qw), ("ka", kw), ("va", kw), ("gm", D), ("ga", D)):
        seg[name] = wi[:, o:o + width]
        o += width
    order = ("qa", "gm", "ga", "qm", "km", "vm", "om", "ka", "va")
    w_main = jnp.concatenate([seg[nm] for nm in order], axis=1).astype(BF16)
    col = {}
    o = 0
    for nm in order:
        col[nm] = o
        o += seg[nm].shape[1]
    tn = 2 * kw
    for nm in order[:-2]:
        assert col[nm] % tn == 0 and seg[nm].shape[1] % tn == 0
    for nm, width in (("gm", D), ("ga", D), ("qm", mw), ("km", mw), ("vm", mw), ("om", mw), ("ka", kw), ("va", kw)):
        assert col[nm] % width == 0
    w_gate = jnp.pad(jnp.concatenate([seg["im"], seg["fm"]], axis=1), ((0, 0), (0, LANES_V7X - ng))).astype(BF16)

    R = -(-(1 + dec_batch) // 8) * 8
    cond = jnp.concatenate([c_ctx[None, :], c, jnp.zeros((R - 1 - dec_batch, D), F32)], axis=0)
    mods = _modulation(cond, w_mod[0], b_mod[0]).reshape(R, N_MOD, D)

    xp2 = x_prompt.reshape(t_ctx, D)
    xs2 = x_sample.reshape(t_lat, D)
    cos, sin = _rope_tables(dec_seq, hd)
    z, kv32, gates, gates_t = _inproj(xp2, xs2, mods, norm1_w, w_main, w_gate, cos, sin, t_ctx=t_ctx,
                                      dec_seq=dec_seq, tn=tn, n_rope_tiles=qw // tn, n_gates=ng)

    nseq = batch + dec_batch
    C0 = jnp.concatenate([jnp.zeros((batch, 2, mh, dk, dv), F32), state_mlstm_C[:, 0]], axis=0)
    n0 = jnp.concatenate([jnp.zeros((batch, 2, mh, dk), F32), state_mlstm_n[:, 0]], axis=0)
    m0 = jnp.concatenate([jnp.zeros((batch, 2, mh), F32), state_mlstm_m[:, 0]], axis=0)
    s0 = jnp.concatenate([C0, n0[..., None], jnp.zeros((nseq, 2, mh, dk, dv - 1), F32)], axis=-1)
    s0 = s0.reshape(nseq, 2 * mh, dk, 2 * dv)
    m0 = jnp.broadcast_to(m0.reshape(nseq, 2 * mh, 1), (nseq, 2 * mh, LANES_V7X))
    gate_b = jnp.concatenate([igate_b[0].reshape(-1), fgate_b[0].reshape(-1)])
    steps = _scan_steps(batch, seq, dec_batch, dec_seq, M_CHUNK)
    hf, hb, s_fin, m_fin = _mlstm(z, gates, gates_t, gate_b.reshape(1, ng), gate_b.reshape(ng, 1), s0, m0, steps,
                                  mh=mh, dk=dk, qcol=col["qm"] // mw, kcol=col["km"] // mw, vcol=col["vm"] // mw)

    sink = attn_sink[0]
    ha_ctx = _ctx_attention(sink, z, batch=batch, seq=seq, kvh=kvh, groups=groups, hd=hd,
                            kcol=col["ka"] // kw, vcol=col["va"] // kw)
    ha_lat = _lat_attention(sink, z, cache_k.reshape(dec_batch, depth, past, kw),
                            cache_v.reshape(dec_batch, depth, past, kw), t_ctx=t_ctx, dec_batch=dec_batch,
                            dec_seq=dec_seq, kvh=kvh, groups=groups, hd=hd, kcol=col["ka"] // kw,
                            vcol=col["va"] // kw)

    x1, h2, logits_t = _outproj(hf, hb, z, ha_ctx, ha_lat, xp2, xs2, mods, mlstm_norm_w, norm2_w,
                                w_branch_m[0].astype(BF16), w_branch_a[0].astype(BF16), w_out[0].astype(BF16),
                                router_w[0].T, t_ctx=t_ctx, dec_seq=dec_seq, mh=mh, dv=dv,
                                omcol=col["om"] // mw, gmcol=col["gm"] // D, gacol=col["ga"] // D)

    eidx, rank, wtok, counts = _router(logits_t, router_b[0])
    blk_e, nact, slot_tok, slot_dst = _dispatch(eidx, rank, counts[:, 0], EXPERT_ROWS)
    y6 = _moe(blk_e, nact, slot_tok, slot_dst, expert_w1[0].astype(BF16), expert_w3[0].astype(BF16),
              expert_w2[0].astype(BF16), h2, rows=EXPERT_ROWS)

    y_ctx, y_lat = _final(h2, x1, wtok, y6, mods, shared_w1[0].astype(BF16), shared_w3[0].astype(BF16),
                          shared_w2[0].astype(BF16), final_norm_w.reshape(1, D), t_ctx=t_ctx, dec_seq=dec_seq)

    y_prompt = y_ctx.reshape(batch, seq, D)
    y_sample = y_lat.reshape(dec_batch, dec_seq, D)
    new_k = kv32[:t_ctx, :kw].reshape(batch, 1, seq, kvh, hd)
    new_v = kv32[:t_ctx, kw:].reshape(batch, 1, seq, kvh, hd)
    s_ctx = s_fin[:batch].reshape(batch, 1, 2, mh, dk, 2 * dv)
    new_C = s_ctx[..., :dv]
    new_n = s_ctx[..., dv]
    new_m = m_fin[:batch, :, 0].reshape(batch, 1, 2, mh)
    return y_prompt, y_sample, new_k, new_v, new_C, new_n, new_m
```

```python
import functools

import numpy as np
import jax
import jax.numpy as jnp
from jax import lax
from jax.experimental import pallas as pl
from jax.experimental.pallas import tpu as pltpu

TOP_K = 6
N_GROUPS = 8
TOPK_GROUPS = 4
ROUTED_SCALE = 2.5
WINDOW = 128
Q_BLOCK = 128
GRID_W = 64
ROPE_BASE = 10000.0
M_CHUNK = 128
N_MOD = 6
EPS = 1e-6

LANES_V7X = 128
MXU_COLS_V7X = 256
VMEM_LIMIT_V7X = 56 * 1024 * 1024

TM_INPROJ = 512
TM_OUTPROJ = 256
TM_FINAL = 128
TN_MOD = 1024
EXPERT_ROWS = 256
TR_ROUTER = 512
TL_SLOTPOS = 2048
TD_DISPATCH = 256

F32 = jnp.float32
BF16 = jnp.bfloat16
_NT = (((1,), (1,)), ((), ()))


def _params(sem):
    return pltpu.CompilerParams(dimension_semantics=sem, vmem_limit_bytes=VMEM_LIMIT_V7X)


def _dot(a, b):
    return jnp.dot(a, b, preferred_element_type=F32)


def _dot_nt(a, b):
    return lax.dot_general(a, b, _NT, preferred_element_type=F32)


def _pack_bf16_pair(x):
    c = x.shape[1] // 2
    lo = pltpu.bitcast(x[:, :c].astype(BF16).astype(F32), jnp.uint32)
    hi = pltpu.bitcast(x[:, c:].astype(BF16).astype(F32), jnp.uint32)
    return (lo >> 16) | (hi & jnp.uint32(0xFFFF0000))


def _unpack_bf16_pair(w):
    lo = pltpu.bitcast(w << 16, F32).astype(BF16)
    hi = pltpu.bitcast(w & jnp.uint32(0xFFFF0000), F32).astype(BF16)
    return jnp.concatenate([lo, hi], axis=1)


def _mod_kernel(c_ref, w_ref, b_ref, o_ref):
    s = jax.nn.silu(c_ref[...]).astype(BF16)
    o_ref[...] = _dot(s, w_ref[...].astype(BF16)) + b_ref[...]


def _modulation(cond, w_mod, b_mod):
    R, D = cond.shape
    N = w_mod.shape[1]
    tn = min(TN_MOD, N)
    return pl.pallas_call(
        _mod_kernel,
        out_shape=jax.ShapeDtypeStruct((R, N), F32),
        grid=(N // tn,),
        in_specs=[pl.BlockSpec((R, D), lambda n: (0, 0)),
                  pl.BlockSpec((D, tn), lambda n: (0, n)),
                  pl.BlockSpec((1, tn), lambda n: (0, n))],
        out_specs=pl.BlockSpec((R, tn), lambda n: (0, n)),
        compiler_params=_params(("arbitrary",)),
        name="modulation",
    )(cond, w_mod, b_mod.reshape(1, N))


def _rope_slice(x, cos, sin_signed, first_half):
    swap = jnp.where(first_half, pltpu.roll(x, 96, 1), pltpu.roll(x, 32, 1))
    return x * cos + swap * sin_signed


def _inproj_kernel(xp_ref, xs_ref, mod_ref, n1_ref, w_ref, wg_ref, cos_ref, sin_ref,
                   z_ref, kv_ref, g_ref, gt_ref, h_scr, *, nctx_tiles, n_rope_tiles, kv_tile, n_gates):
    i = pl.program_id(0)
    n = pl.program_id(1)
    is_ctx = i < nctx_tiles
    tn = z_ref.shape[1]

    @pl.when(n == 0)
    def _():
        x = jnp.where(is_ctx, xp_ref[...], xs_ref[...])
        y = x * lax.rsqrt(jnp.mean(x * x, axis=-1, keepdims=True) + EPS) * n1_ref[...]
        h = (y * (1.0 + mod_ref[0, 1:2, :]) + mod_ref[0, 0:1, :]).astype(BF16)
        h_scr[...] = h
        g = _dot(h, wg_ref[...])
        g_ref[...] = g[:, :n_gates]
        gt_ref[...] = g.T[:n_gates, :]

    acc = _dot(h_scr[...], w_ref[...])

    def rope_cols(ncols):
        cos = jnp.where(is_ctx, 1.0, cos_ref[...])
        sin = jnp.where(is_ctx, 0.0, sin_ref[...])
        lane = lax.broadcasted_iota(jnp.int32, cos.shape, 1)
        first_half = (lane % 64) < 32
        return [_rope_slice(acc[:, c:c + LANES_V7X], cos, sin, first_half)
                for c in range(0, ncols, LANES_V7X)]

    @pl.when(n < n_rope_tiles)
    def _():
        z_ref[...] = jnp.concatenate(rope_cols(tn), axis=1).astype(BF16)

    @pl.when(n == kv_tile)
    def _():
        r = jnp.concatenate(rope_cols(tn // 2) + [acc[:, tn // 2:]], axis=1)
        z_ref[...] = r.astype(BF16)
        kv_ref[...] = r

    @pl.when(jnp.logical_and(n >= n_rope_tiles, n != kv_tile))
    def _():
        z_ref[...] = acc.astype(BF16)


def _inproj(x_prompt2, x_sample2, mods, n1, w_main, w_gate, cos, sin, *, t_ctx, dec_seq, tn, n_rope_tiles, n_gates):
    t_lat, D = x_sample2.shape
    t_all = t_ctx + t_lat
    tm = min(TM_INPROJ, t_ctx, dec_seq)
    nctx = t_ctx // tm
    per_seq = dec_seq // tm
    ncols = w_main.shape[1]
    ntile = ncols // tn
    kv_tile = ntile - 1

    def mod_row(i):
        return jnp.where(i < nctx, 0, 1 + (i - nctx) // per_seq)

    def pos_blk(i):
        return jnp.where(i < nctx, 0, (i - nctx) % per_seq)

    kernel = functools.partial(_inproj_kernel, nctx_tiles=nctx, n_rope_tiles=n_rope_tiles, kv_tile=kv_tile,
                               n_gates=n_gates)
    return pl.pallas_call(
        kernel,
        out_shape=(jax.ShapeDtypeStruct((t_all, ncols), BF16),
                   jax.ShapeDtypeStruct((t_all, tn), F32),
                   jax.ShapeDtypeStruct((t_all, n_gates), F32),
                   jax.ShapeDtypeStruct((n_gates, t_all), F32)),
        grid=(t_all // tm, ntile),
        in_specs=[pl.BlockSpec((tm, D), lambda i, n: (jnp.minimum(i, nctx - 1), 0)),
                  pl.BlockSpec((tm, D), lambda i, n: (jnp.maximum(i - nctx, 0), 0)),
                  pl.BlockSpec((1, N_MOD, D), lambda i, n: (mod_row(i), 0, 0)),
                  pl.BlockSpec((1, D), lambda i, n: (0, 0)),
                  pl.BlockSpec((D, tn), lambda i, n: (0, n)),
                  pl.BlockSpec((D, LANES_V7X), lambda i, n: (0, 0)),
                  pl.BlockSpec((tm, LANES_V7X), lambda i, n: (pos_blk(i), 0)),
                  pl.BlockSpec((tm, LANES_V7X), lambda i, n: (pos_blk(i), 0))],
        out_specs=(pl.BlockSpec((tm, tn), lambda i, n: (i, n)),
                   pl.BlockSpec((tm, tn), lambda i, n: (i, 0)),
                   pl.BlockSpec((tm, n_gates), lambda i, n: (i, 0)),
                   pl.BlockSpec((n_gates, tm), lambda i, n: (0, i))),
        scratch_shapes=[pltpu.VMEM((tm, D), BF16)],
        compiler_params=_params(("arbitrary", "arbitrary")),
        name="inproj",
    )(x_prompt2, x_sample2, mods, n1, w_main, w_gate, cos, sin)


def _mlstm_kernel(fb, bb, sq, fi, la,
                  qf, kf, vf, qb, kb, vb, gf, gb, gtf, gtb, brow, bcol, s0, m0,
                  hf, hb, s_out, m_out, s_scr, m_scr, *, mh, dk, chunk):
    s = pl.program_id(0)
    L = chunk
    assert L == dk
    scale = dk ** -0.5

    @pl.when(fi[s] == 1)
    def _():
        s_scr[...] = s0[0]
        m_scr[...] = m0[0]

    ri = lax.broadcasted_iota(jnp.int32, (L, L), 0)
    ci = lax.broadcasted_iota(jnp.int32, (L, L), 1)
    low = ri >= ci
    upp = ri <= ci
    low_f = low.astype(F32)
    upp_f = upp.astype(F32)
    ones_blk = jnp.ones((L, dk), BF16)
    hi = lax.Precision.HIGHEST
    refs = ((qf, kf, vf, gf, gtf, hf), (qb, kb, vb, gb, gtb, hb))
    units = [(dr, h) for dr in range(2) for h in range(mh)]
    sl = lambda h: slice(h * dk, (h + 1) * dk)

    gate = []
    for dr, (_, _, _, g_ref, gt_ref, _) in enumerate(refs):
        G = g_ref[...] + brow[...]
        GT = gt_ref[...] + bcol[...]
        ic_col = G[:, dr * mh:(dr + 1) * mh]
        lf_col = jax.nn.log_sigmoid(G[:, (2 + dr) * mh:(3 + dr) * mh])
        ic_row = GT[dr * mh:(dr + 1) * mh, :]
        lf_row = jax.nn.log_sigmoid(GT[(2 + dr) * mh:(3 + dr) * mh, :])
        b_col = jnp.dot(low_f if dr == 0 else upp_f, lf_col, precision=hi, preferred_element_type=F32)
        b_row = jnp.dot(lf_row, upp_f if dr == 0 else low_f, precision=hi, preferred_element_type=F32)
        gate.append((ic_col, ic_row, b_col, b_row))

    S_prev = [s_scr[dr * mh + h] for dr, h in units]
    m_prev = [m_scr[dr * mh + h:dr * mh + h + 1, 0:1] for dr, h in units]
    q = [refs[dr][0][:, sl(h)] for dr, h in units]
    k = [refs[dr][1][:, sl(h)] for dr, h in units]
    v_aug = [jnp.concatenate([refs[dr][2][:, sl(h)], ones_blk], axis=1) for dr, h in units]
    qk = [_dot_nt(q[u], k[u]) for u in range(len(units))]
    qs = [_dot(q[u], S_prev[u].astype(BF16)) for u in range(len(units))]

    sm, w_inter, floor, b_rep = [], [], [], []
    for u, (dr, h) in enumerate(units):
        ic_col, ic_row, b_col, b_row = gate[dr]
        bc = jnp.broadcast_to(b_col[:, h:h + 1], (L, L))
        d = jnp.where(low if dr == 0 else upp, bc - b_row[h:h + 1, :] + ic_row[h:h + 1, :], -jnp.inf)
        inter = bc + m_prev[u]
        m_t = jnp.maximum(inter, jnp.broadcast_to(jnp.max(d, axis=-1, keepdims=True), (L, L)))
        sm.append((qk[u] * scale * jnp.exp(d - m_t)).astype(BF16))
        w_inter.append(jnp.exp(inter - m_t))
        floor.append(jnp.exp(-m_t))
        b_rep.append(bc)

    sv = [_dot(sm[u], v_aug[u]) for u in range(len(units))]
    for u, (dr, h) in enumerate(units):
        num = sv[u][:, :dk] + w_inter[u] * qs[u][:, :dk]
        den = sv[u][:, dk:] + w_inter[u] * qs[u][:, dk:]
        refs[dr][5][:, sl(h)] = (num / jnp.maximum(jnp.abs(den), floor[u])).astype(BF16)

    kw_t, wc, m_new = [], [], []
    for u, (dr, h) in enumerate(units):
        ic_col = gate[dr][0]
        bc = b_rep[u]
        b_last = bc[L - 1:L, :] if dr == 0 else bc[0:1, :]
        g = b_last - bc + jnp.broadcast_to(ic_col[:, h:h + 1], (L, L))
        mn = jnp.maximum(b_last + m_prev[u], jnp.max(g, axis=0, keepdims=True))
        kw_t.append((k[u].astype(F32) * (jnp.exp(g - mn) * scale)).T.astype(BF16))
        wc.append(jnp.exp(b_last + m_prev[u] - mn))
        m_new.append(mn)

    upd = [_dot(kw_t[u], v_aug[u]) for u in range(len(units))]
    for u, (dr, h) in enumerate(units):
        r = dr * mh + h
        s_scr[r] = jnp.concatenate([wc[u], wc[u]], axis=1) * S_prev[u] + upd[u]
        m_scr[r:r + 1, :] = m_new[u]

    @pl.when(la[s] == 1)
    def _():
        s_out[0] = s_scr[...]
        m_out[0] = m_scr[...]


def _mlstm(z, gates, gates_t, brow, bcol, s0, m0, steps, *, mh, dk, qcol, kcol, vcol):
    t_all = z.shape[0]
    L = M_CHUNK
    mw = mh * dk
    ng = gates.shape[1]
    nseq = s0.shape[0]
    fb, bb, sq, fi, la = steps
    nsteps = fb.shape[0]

    def zspec(which, col):
        return pl.BlockSpec((L, mw), lambda s, fb, bb, sq, fi, la: ((fb, bb)[which][s], col))

    def gspec(which):
        return pl.BlockSpec((L, ng), lambda s, fb, bb, sq, fi, la: ((fb, bb)[which][s], 0))

    def gtspec(which):
        return pl.BlockSpec((ng, L), lambda s, fb, bb, sq, fi, la: (0, (fb, bb)[which][s]))

    grid_spec = pltpu.PrefetchScalarGridSpec(
        num_scalar_prefetch=5,
        grid=(nsteps,),
        in_specs=[zspec(0, qcol), zspec(0, kcol), zspec(0, vcol),
                  zspec(1, qcol), zspec(1, kcol), zspec(1, vcol),
                  gspec(0), gspec(1), gtspec(0), gtspec(1),
                  pl.BlockSpec((1, ng), lambda s, *_: (0, 0)),
                  pl.BlockSpec((ng, 1), lambda s, *_: (0, 0)),
                  pl.BlockSpec((1, 2 * mh, dk, 2 * dk), lambda s, fb, bb, sq, fi, la: (sq[s], 0, 0, 0)),
                  pl.BlockSpec((1, 2 * mh, LANES_V7X), lambda s, fb, bb, sq, fi, la: (sq[s], 0, 0))],
        out_specs=(pl.BlockSpec((L, mw), lambda s, fb, bb, sq, fi, la: (fb[s], 0)),
                   pl.BlockSpec((L, mw), lambda s, fb, bb, sq, fi, la: (bb[s], 0)),
                   pl.BlockSpec((1, 2 * mh, dk, 2 * dk), lambda s, fb, bb, sq, fi, la: (sq[s], 0, 0, 0)),
                   pl.BlockSpec((1, 2 * mh, LANES_V7X), lambda s, fb, bb, sq, fi, la: (sq[s], 0, 0))),
        scratch_shapes=[pltpu.VMEM((2 * mh, dk, 2 * dk), F32), pltpu.VMEM((2 * mh, LANES_V7X), F32)],
    )
    return pl.pallas_call(
        functools.partial(_mlstm_kernel, mh=mh, dk=dk, chunk=L),
        out_shape=(jax.ShapeDtypeStruct((t_all, mw), BF16),
                   jax.ShapeDtypeStruct((t_all, mw), BF16),
                   jax.ShapeDtypeStruct((nseq, 2 * mh, dk, 2 * dk), F32),
                   jax.ShapeDtypeStruct((nseq, 2 * mh, LANES_V7X), F32)),
        grid_spec=grid_spec,
        compiler_params=_params(("arbitrary",)),
        name="mlstm",
    )(fb, bb, sq, fi, la, z, z, z, z, z, z, gates, gates, gates_t, gates_t, brow, bcol, s0, m0)


def _sink_column(sink_ref, kv, groups, rows_per_group):
    row_g = lax.broadcasted_iota(jnp.int32, (groups * rows_per_group, 1), 0) // rows_per_group
    col = jnp.full((groups * rows_per_group, 1), sink_ref[kv * groups], F32)
    for g in range(1, groups):
        col = jnp.where(row_g == g, sink_ref[kv * groups + g], col)
    return col


def _softmax_av(scores, values, sink_col):
    mx = sink_col
    for s in scores:
        mx = jnp.maximum(mx, jnp.max(s, axis=-1, keepdims=True))
    den = jnp.exp(sink_col - mx)
    out = None
    for s, v in zip(scores, values):
        p = jnp.exp(s - mx)
        den = den + jnp.sum(p, axis=-1, keepdims=True)
        pv = _dot(p.astype(BF16), v)
        out = pv if out is None else out + pv
    return out / den


def _ctx_attn_kernel(sink_ref, q_ref, k_ref, v_ref, o_ref, *, kvh, groups, hd):
    S = q_ref.shape[0]
    scale = hd ** -0.5
    for kv in range(kvh):
        k = k_ref[:, kv * hd:(kv + 1) * hd]
        v = v_ref[:, kv * hd:(kv + 1) * hd]
        q = jnp.concatenate([q_ref[:, (kv * groups + g) * hd:(kv * groups + g + 1) * hd]
                             for g in range(groups)], axis=0)
        o = _softmax_av([_dot_nt(q, k) * scale], [v], _sink_column(sink_ref, kv, groups, S))
        for g in range(groups):
            o_ref[:, (kv * groups + g) * hd:(kv * groups + g + 1) * hd] = o[g * S:(g + 1) * S].astype(BF16)


def _ctx_attention(sink, z, *, batch, seq, kvh, groups, hd, kcol, vcol):
    qw = kvh * groups * hd
    kw = kvh * hd
    return pl.pallas_call(
        functools.partial(_ctx_attn_kernel, kvh=kvh, groups=groups, hd=hd),
        out_shape=jax.ShapeDtypeStruct((batch * seq, qw), BF16),
        grid=(batch,),
        in_specs=[pl.BlockSpec(memory_space=pltpu.SMEM),
                  pl.BlockSpec((seq, qw), lambda b: (b, 0)),
                  pl.BlockSpec((seq, kw), lambda b: (b, kcol)),
                  pl.BlockSpec((seq, kw), lambda b: (b, vcol))],
        out_specs=pl.BlockSpec((seq, qw), lambda b: (b, 0)),
        compiler_params=_params(("arbitrary",)),
        name="ctx_attention",
    )(sink, z, z, z)


def _lat_attn_kernel(sink_ref, q_ref, kp_ref, kc_ref, kn_ref, vp_ref, vc_ref, vn_ref, ck_ref, cv_ref, o_ref,
                     *, kvh, groups, hd):
    j = pl.program_id(1)
    nb = pl.num_programs(1)
    Q = q_ref.shape[0]
    scale = hd ** -0.5
    R = groups * Q
    rq = lax.broadcasted_iota(jnp.int32, (R, Q), 0) % Q
    cc = lax.broadcasted_iota(jnp.int32, (R, Q), 1)
    mask_prev = jnp.logical_and(cc >= rq, j > 0)
    mask_next = jnp.logical_and(cc <= rq, j < nb - 1)
    for kv in range(kvh):
        sl = slice(kv * hd, (kv + 1) * hd)
        q = jnp.concatenate([q_ref[:, (kv * groups + g) * hd:(kv * groups + g + 1) * hd]
                             for g in range(groups)], axis=0)
        s_prev = jnp.where(mask_prev, _dot_nt(q, kp_ref[:, sl]) * scale, -jnp.inf)
        s_cur = _dot_nt(q, kc_ref[:, sl]) * scale
        s_next = jnp.where(mask_next, _dot_nt(q, kn_ref[:, sl]) * scale, -jnp.inf)
        s_ctx = _dot_nt(q, ck_ref[0, 0, :, sl].astype(BF16)) * scale
        o = _softmax_av([s_prev, s_cur, s_next, s_ctx],
                        [vp_ref[:, sl], vc_ref[:, sl], vn_ref[:, sl], cv_ref[0, 0, :, sl].astype(BF16)],
                        _sink_column(sink_ref, kv, groups, Q))
        for g in range(groups):
            o_ref[:, (kv * groups + g) * hd:(kv * groups + g + 1) * hd] = o[g * Q:(g + 1) * Q].astype(BF16)


def _lat_attention(sink, z, cache_k, cache_v, *, t_ctx, dec_batch, dec_seq, kvh, groups, hd, kcol, vcol):
    assert WINDOW == Q_BLOCK
    Q = Q_BLOCK
    nb = dec_seq // Q
    base = t_ctx // Q
    qw = kvh * groups * hd
    kw = kvh * hd
    past = cache_k.shape[2]

    def kvspec(col, shift):
        return pl.BlockSpec((Q, kw), lambda b, j: (base + b * nb + jnp.clip(j + shift, 0, nb - 1), col))

    cspec = pl.BlockSpec((1, 1, past, kw), lambda b, j: (b, 0, 0, 0))
    return pl.pallas_call(
        functools.partial(_lat_attn_kernel, kvh=kvh, groups=groups, hd=hd),
        out_shape=jax.ShapeDtypeStruct((dec_batch * dec_seq, qw), BF16),
        grid=(dec_batch, nb),
        in_specs=[pl.BlockSpec(memory_space=pltpu.SMEM),
                  pl.BlockSpec((Q, qw), lambda b, j: (base + b * nb + j, 0)),
                  kvspec(kcol, -1), kvspec(kcol, 0), kvspec(kcol, 1),
                  kvspec(vcol, -1), kvspec(vcol, 0), kvspec(vcol, 1),
                  cspec, cspec],
        out_specs=pl.BlockSpec((Q, qw), lambda b, j: (b * nb + j, 0)),
        compiler_params=_params(("arbitrary", "arbitrary")),
        name="lat_attention",
    )(sink, z, z, z, z, z, z, z, cache_k, cache_v)


def _outproj_kernel(hf_ref, hb_ref, om_ref, hac_ref, hal_ref, gm_ref, ga_ref, xp_ref, xs_ref, mod_ref,
                    mn_ref, n2_ref, wm_ref, wa_ref, wo_ref, rw_ref,
                    x1_ref, h2_ref, lg_ref, *, nctx_tiles, mh, dv):
    i = pl.program_id(0)
    is_ctx = i < nctx_tiles
    hm = hf_ref[...].astype(F32) + hb_ref[...].astype(F32)
    parts = []
    for h in range(mh):
        sl = hm[:, h * dv:(h + 1) * dv]
        parts.append(sl * lax.rsqrt(jnp.mean(sl * sl, axis=-1, keepdims=True) + EPS))
    hmn = jnp.concatenate(parts, axis=1) * mn_ref[...] * jax.nn.sigmoid(om_ref[...].astype(F32))
    ha = jnp.where(is_ctx, hac_ref[...], hal_ref[...])
    y = (jax.nn.sigmoid(gm_ref[...].astype(F32)) * _dot(hmn.astype(BF16), wm_ref[...])
         + jax.nn.sigmoid(ga_ref[...].astype(F32)) * _dot(ha, wa_ref[...]))
    x = jnp.where(is_ctx, xp_ref[...], xs_ref[...])
    x1 = x + mod_ref[0, 2:3, :] * _dot(y.astype(BF16), wo_ref[...])
    x1_ref[...] = x1
    n = x1 * lax.rsqrt(jnp.mean(x1 * x1, axis=-1, keepdims=True) + EPS) * n2_ref[...]
    h2 = n * (1.0 + mod_ref[0, 4:5, :]) + mod_ref[0, 3:4, :]
    h2_ref[...] = _pack_bf16_pair(h2)
    lg_ref[...] = lax.dot_general(rw_ref[...], h2, _NT, precision=lax.Precision.HIGHEST,
                                  preferred_element_type=F32)


def _outproj(hf, hb, z, ha_ctx, ha_lat, x_prompt2, x_sample2, mods, mnorm, n2, wm, wa, wo, rw_t,
             *, t_ctx, dec_seq, mh, dv, omcol, gmcol, gacol):
    t_all = hf.shape[0]
    D = x_prompt2.shape[1]
    mw = mh * dv
    qw = ha_ctx.shape[1]
    E = rw_t.shape[0]
    tm = min(TM_OUTPROJ, t_ctx, dec_seq)
    nctx = t_ctx // tm
    per_seq = dec_seq // tm

    def ctx_blk(i):
        return (jnp.minimum(i, nctx - 1), 0)

    def lat_blk(i):
        return (jnp.maximum(i - nctx, 0), 0)

    def mod_row(i):
        return (jnp.where(i < nctx, 0, 1 + (i - nctx) // per_seq), 0, 0)

    const = lambda i: (0, 0)
    single = pl.Buffered(1)
    return pl.pallas_call(
        functools.partial(_outproj_kernel, nctx_tiles=nctx, mh=mh, dv=dv),
        out_shape=(jax.ShapeDtypeStruct((t_all, D), F32),
                   jax.ShapeDtypeStruct((t_all, D // 2), jnp.uint32),
                   jax.ShapeDtypeStruct((E, t_all), F32)),
        grid=(t_all // tm,),
        in_specs=[pl.BlockSpec((tm, mw), lambda i: (i, 0)),
                  pl.BlockSpec((tm, mw), lambda i: (i, 0)),
                  pl.BlockSpec((tm, mw), lambda i: (i, omcol)),
                  pl.BlockSpec((tm, qw), ctx_blk),
                  pl.BlockSpec((tm, qw), lat_blk),
                  pl.BlockSpec((tm, D), lambda i: (i, gmcol)),
                  pl.BlockSpec((tm, D), lambda i: (i, gacol)),
                  pl.BlockSpec((tm, D), ctx_blk),
                  pl.BlockSpec((tm, D), lat_blk),
                  pl.BlockSpec((1, N_MOD, D), mod_row),
                  pl.BlockSpec((1, mw), const),
                  pl.BlockSpec((1, D), const),
                  pl.BlockSpec((mw, D), const, pipeline_mode=single),
                  pl.BlockSpec((qw, D), const, pipeline_mode=single),
                  pl.BlockSpec((D, D), const, pipeline_mode=single),
                  pl.BlockSpec((E, D), const, pipeline_mode=single)],
        out_specs=(pl.BlockSpec((tm, D), lambda i: (i, 0)),
                   pl.BlockSpec((tm, D // 2), lambda i: (i, 0)),
                   pl.BlockSpec((E, tm), lambda i: (0, i))),
        compiler_params=_params(("arbitrary",)),
        name="outproj",
    )(hf, hb, z, ha_ctx, ha_lat, z, z, x_prompt2, x_sample2, mods, mnorm, n2, wm, wa, wo, rw_t)


def _dispatch_kernel(pstart_ref, pcount_ref, nact_ref, pos_ref, h2p_ref, xs_hbm, zblk, sem, psem, *, top_k):
    i = pl.program_id(0)
    td = h2p_ref.shape[0]
    rows = zblk.shape[0]

    @pl.when(i == 0)
    def _():
        zblk[...] = jnp.zeros_like(zblk)
        zrow = zblk.at[pl.ds(0, 1)]

        def per_expert(e, total):
            def fill(r, carry):
                pltpu.make_async_copy(zrow, xs_hbm.at[pl.ds(pstart_ref[e] + r, 1)], psem).start()
                return carry

            lax.fori_loop(0, pcount_ref[e], fill, 0)
            return total + pcount_ref[e]

        total = lax.fori_loop(0, pstart_ref.shape[0], per_expert, 0)

        def drain(j, carry):
            pltpu.make_async_copy(zrow, xs_hbm.at[pl.ds(0, 1)], psem).wait()
            return carry

        lax.fori_loop(0, total, drain, 0)

        def empty_block(b, carry):
            fill = pltpu.make_async_copy(zblk, xs_hbm.at[pl.ds(b * rows, rows)], psem)
            fill.start()
            fill.wait()
            return carry

        lax.fori_loop(nact_ref[0], xs_hbm.shape[0] // rows, empty_block, 0)

    for k in range(top_k):
        for t in range(td):
            pltpu.make_async_copy(h2p_ref.at[pl.ds(t, 1)], xs_hbm.at[pl.ds(pos_ref[0, k, t], 1)], sem).start()
    for k in range(top_k):
        pltpu.make_async_copy(h2p_ref, xs_hbm.at[pl.ds(0, td)], sem).wait()


def _dispatch_rows(pad_start, pad_count, nact, pos3, h2p, *, n_slots, rows):
    nt, K, td = pos3.shape
    T, Dh = h2p.shape
    grid_spec = pltpu.PrefetchScalarGridSpec(
        num_scalar_prefetch=3,
        grid=(nt,),
        in_specs=[pl.BlockSpec((1, K, td), lambda i, *_: (i, 0, 0), memory_space=pltpu.SMEM),
                  pl.BlockSpec((td, Dh), lambda i, *_: (i, 0))],
        out_specs=pl.BlockSpec(memory_space=pl.ANY),
        scratch_shapes=[pltpu.VMEM((rows, Dh), jnp.uint32), pltpu.SemaphoreType.DMA(()),
                        pltpu.SemaphoreType.DMA(())],
    )
    return pl.pallas_call(
        functools.partial(_dispatch_kernel, top_k=K),
        out_shape=jax.ShapeDtypeStruct((n_slots, Dh), jnp.uint32),
        grid_spec=grid_spec,
        compiler_params=_params(("arbitrary",)),
        name="dispatch_rows",
    )(pad_start, pad_count, nact, pos3, h2p)


def _moe_kernel(blk_e, nact_ref, x_ref, w1_ref, w3_ref, w2_ref, y_ref):
    i = pl.program_id(0)

    @pl.when(i < nact_ref[0])
    def _():
        x = _unpack_bf16_pair(x_ref[...])
        hmid = (jax.nn.silu(_dot(x, w1_ref[0])) * _dot(x, w3_ref[0])).astype(BF16)
        y_ref[...] = _dot(hmid, w2_ref[0])

    @pl.when(i >= nact_ref[0])
    def _():
        y_ref[...] = jnp.zeros_like(y_ref)


def _moe(blk_e, nact, xs, w1, w3, w2, *, rows):
    nblk = blk_e.shape[0]
    E, D, F = w1.shape
    grid_spec = pltpu.PrefetchScalarGridSpec(
        num_scalar_prefetch=2,
        grid=(nblk,),
        in_specs=[pl.BlockSpec((rows, D // 2), lambda i, be, na: (jnp.minimum(i, na[0] - 1), 0)),
                  pl.BlockSpec((1, D, F), lambda i, be, na: (be[i], 0, 0)),
                  pl.BlockSpec((1, D, F), lambda i, be, na: (be[i], 0, 0)),
                  pl.BlockSpec((1, F, D), lambda i, be, na: (be[i], 0, 0))],
        out_specs=pl.BlockSpec((rows, D), lambda i, be, na: (i, 0)),
    )
    return pl.pallas_call(
        _moe_kernel,
        out_shape=jax.ShapeDtypeStruct((nblk * rows, D), F32),
        grid_spec=grid_spec,
        compiler_params=_params(("arbitrary",)),
        name="routed_experts",
    )(blk_e, nact, xs, w1, w3, w2)


def _router_kernel(lg_ref, rb_ref, eidx_ref, rank_ref, wtok_ref, cnt_ref, carry, *, top_k, n_groups, topk_groups):
    i = pl.program_id(0)

    @pl.when(i == 0)
    def _():
        carry[...] = jnp.zeros_like(carry)

    E, tr = lg_ref.shape
    gs = E // n_groups
    scores = jax.nn.sigmoid(lg_ref[...])
    biased = scores + rb_ref[...]
    b3 = biased.reshape(n_groups, gs, tr)
    io3 = lax.broadcasted_iota(jnp.int32, b3.shape, 1)
    m1 = jnp.max(b3, axis=1, keepdims=True)
    i1 = jnp.min(jnp.where(b3 == m1, io3, gs), axis=1, keepdims=True)
    m2 = jnp.max(jnp.where(io3 == i1, -jnp.inf, b3), axis=1, keepdims=True)
    grp = (m1 + m2).reshape(n_groups, tr)
    iog = lax.broadcasted_iota(jnp.int32, grp.shape, 0)
    sel = jnp.zeros(grp.shape, jnp.bool_)
    for _ in range(topk_groups):
        mx = jnp.max(grp, axis=0, keepdims=True)
        hit = iog == jnp.min(jnp.where(grp == mx, iog, n_groups), axis=0, keepdims=True)
        sel = jnp.logical_or(sel, hit)
        grp = jnp.where(hit, -jnp.inf, grp)
    masked = jnp.where(sel.reshape(n_groups, 1, tr), b3, -jnp.inf).reshape(E, tr)
    ioe = lax.broadcasted_iota(jnp.int32, (E, tr), 0)
    onehot = jnp.zeros((E, tr), F32)
    hits, idxs, ws = [], [], []
    for _ in range(top_k):
        mx = jnp.max(masked, axis=0, keepdims=True)
        ix = jnp.min(jnp.where(masked == mx, ioe, E), axis=0, keepdims=True)
        hit = ioe == ix
        hits.append(hit)
        idxs.append(ix)
        ws.append(jnp.sum(jnp.where(hit, scores, 0.0), axis=0, keepdims=True))
        onehot = onehot + hit.astype(F32)
        masked = jnp.where(hit, -jnp.inf, masked)
    wsum = ws[0]
    for w in ws[1:]:
        wsum = wsum + w
    ri = lax.broadcasted_iota(jnp.int32, (tr, tr), 0)
    ci = lax.broadcasted_iota(jnp.int32, (tr, tr), 1)
    before = _dot(onehot.astype(BF16), (ri < ci).astype(BF16)) + carry[...]
    ranks = [jnp.sum(jnp.where(hit, before, 0.0), axis=0, keepdims=True) for hit in hits]
    carry[...] = carry[...] + jnp.sum(onehot, axis=1, keepdims=True)
    cnt_ref[...] = carry[...].astype(jnp.int32)
    eidx_ref[...] = jnp.concatenate(idxs, axis=0)
    rank_ref[...] = jnp.concatenate(ranks, axis=0).astype(jnp.int32)
    wrows = jnp.concatenate([w / wsum * ROUTED_SCALE for w in ws]
                            + [jnp.zeros((LANES_V7X - top_k, tr), F32)], axis=0)
    wtok_ref[...] = wrows.T


def _router(logits_t, router_b):
    E, T = logits_t.shape
    tr = min(TR_ROUTER, T)
    return pl.pallas_call(
        functools.partial(_router_kernel, top_k=TOP_K, n_groups=N_GROUPS, topk_groups=TOPK_GROUPS),
        out_shape=(jax.ShapeDtypeStruct((TOP_K, T), jnp.int32),
                   jax.ShapeDtypeStruct((TOP_K, T), jnp.int32),
                   jax.ShapeDtypeStruct((T, LANES_V7X), F32),
                   jax.ShapeDtypeStruct((E, 1), jnp.int32)),
        grid=(T // tr,),
        in_specs=[pl.BlockSpec((E, tr), lambda i: (0, i)),
                  pl.BlockSpec((E, 1), lambda i: (0, 0))],
        out_specs=(pl.BlockSpec((TOP_K, tr), lambda i: (0, i)),
                   pl.BlockSpec((TOP_K, tr), lambda i: (0, i)),
                   pl.BlockSpec((tr, LANES_V7X), lambda i: (i, 0)),
                   pl.BlockSpec((E, 1), lambda i: (0, 0))),
        scratch_shapes=[pltpu.VMEM((E, 1), F32)],
        compiler_params=_params(("arbitrary",)),
        name="router",
    )(logits_t, router_b.reshape(E, 1))


def _slot_pos_kernel(start_ref, eidx_ref, rank_ref, pos_ref, *, n_experts):
    eidx = eidx_ref[...]
    pos = rank_ref[...]
    for e in range(n_experts):
        pos = pos + jnp.where(eidx == e, start_ref[e], 0)
    pos_ref[...] = pos


def _slot_pos(start_pad, eidx, rank):
    K, T = eidx.shape
    tl = min(TL_SLOTPOS, T)
    return pl.pallas_call(
        functools.partial(_slot_pos_kernel, n_experts=start_pad.shape[0]),
        out_shape=jax.ShapeDtypeStruct((K, T), jnp.int32),
        grid=(T // tl,),
        in_specs=[pl.BlockSpec(memory_space=pltpu.SMEM),
                  pl.BlockSpec((K, tl), lambda i: (0, i)),
                  pl.BlockSpec((K, tl), lambda i: (0, i))],
        out_specs=pl.BlockSpec((K, tl), lambda i: (0, i)),
        compiler_params=_params(("arbitrary",)),
        name="slot_pos",
    )(start_pad, eidx, rank)


def _dispatch(eidx, rank, counts, rows):
    K, T = eidx.shape
    A = K * T
    E = counts.shape[0]
    padded = (counts + rows - 1) // rows * rows
    end_pad = jnp.cumsum(padded)
    start_pad = end_pad - padded
    nblk = -(-(A + E * (rows - 1)) // rows)
    n_slots = nblk * rows
    nact = end_pad[-1] // rows
    blk = jnp.arange(nblk, dtype=jnp.int32)
    blk_e = jnp.sum(end_pad[None, :] <= (blk * rows)[:, None], axis=1).astype(jnp.int32)
    last_e = jnp.sum(end_pad <= (nact - 1) * rows).astype(jnp.int32)
    blk_e = jnp.where(blk < nact, blk_e, last_e)
    pos = _slot_pos(start_pad.astype(jnp.int32), eidx, rank)
    pad_start = (start_pad + counts).astype(jnp.int32)
    pad_count = (padded - counts).astype(jnp.int32)
    return pos, blk_e, nact.astype(jnp.int32).reshape(1), pad_start, pad_count, n_slots


def _tile_major(pos, tile):
    K, T = pos.shape
    return pos.reshape(K, T // tile, tile).transpose(1, 0, 2)


def _final_kernel(pos_ref, posn_ref, h2p_ref, x1_ref, wt_ref, mod_ref, w1_ref, w3_ref, w2_ref, fn_ref, y_hbm,
                  oc_ref, ol_ref, ybuf0, ybuf1, sem, *, top_k, nctx_tiles):
    i = pl.program_id(0)
    nt = pl.num_programs(0)
    tm = x1_ref.shape[0]

    def gather(idx_ref, buf, s):
        for k in range(top_k):
            for t in range(tm):
                pltpu.make_async_copy(y_hbm.at[pl.ds(idx_ref[0, k, t], 1)], buf.at[k, pl.ds(t, 1)], s).start()

    def gather_wait(buf, s):
        for k in range(top_k):
            pltpu.make_async_copy(y_hbm.at[pl.ds(0, tm)], buf.at[k], s).wait()

    @pl.when(i == 0)
    def _():
        gather(pos_ref, ybuf0, sem.at[0])

    def tile(cur, sem_cur, nxt, sem_nxt):
        @pl.when(i + 1 < nt)
        def _():
            gather(posn_ref, nxt, sem_nxt)

        x = _unpack_bf16_pair(h2p_ref[...])
        hmid = (jax.nn.silu(_dot(x, w1_ref[...])) * _dot(x, w3_ref[...])).astype(BF16)
        moe = _dot(hmid, w2_ref[...])
        gather_wait(cur, sem_cur)
        wt = wt_ref[...]
        for k in range(top_k):
            moe = moe + cur[k] * wt[:, k:k + 1]
        x2 = x1_ref[...] + mod_ref[0, 5:6, :] * moe
        out = x2 * lax.rsqrt(jnp.mean(x2 * x2, axis=-1, keepdims=True) + EPS) * fn_ref[...]

        @pl.when(i < nctx_tiles)
        def _():
            oc_ref[...] = out

        @pl.when(i >= nctx_tiles)
        def _():
            ol_ref[...] = out

    @pl.when(i % 2 == 0)
    def _():
        tile(ybuf0, sem.at[0], ybuf1, sem.at[1])

    @pl.when(i % 2 == 1)
    def _():
        tile(ybuf1, sem.at[1], ybuf0, sem.at[0])


def _final(pos3, h2p, x1, wtok, ys, mods, sw1, sw3, sw2, fnorm, *, t_ctx, dec_seq):
    nt, K, tm = pos3.shape
    t_all, D = x1.shape
    F = sw1.shape[1]
    nctx = t_ctx // tm
    per_seq = dec_seq // tm

    def mod_row(i):
        return (jnp.where(i < nctx, 0, 1 + (i - nctx) // per_seq), 0, 0)

    const = lambda i: (0, 0)
    single = pl.Buffered(1)
    smem_blk = lambda f: pl.BlockSpec((1, K, tm), f, memory_space=pltpu.SMEM)
    return pl.pallas_call(
        functools.partial(_final_kernel, top_k=K, nctx_tiles=nctx),
        out_shape=(jax.ShapeDtypeStruct((t_ctx, D), F32), jax.ShapeDtypeStruct((t_all - t_ctx, D), F32)),
        grid=(nt,),
        in_specs=[smem_blk(lambda i: (i, 0, 0)),
                  smem_blk(lambda i: (jnp.minimum(i + 1, nt - 1), 0, 0)),
                  pl.BlockSpec((tm, D // 2), lambda i: (i, 0)),
                  pl.BlockSpec((tm, D), lambda i: (i, 0)),
                  pl.BlockSpec((tm, LANES_V7X), lambda i: (i, 0)),
                  pl.BlockSpec((1, N_MOD, D), mod_row),
                  pl.BlockSpec((D, F), const, pipeline_mode=single),
                  pl.BlockSpec((D, F), const, pipeline_mode=single),
                  pl.BlockSpec((F, D), const, pipeline_mode=single),
                  pl.BlockSpec((1, D), const),
                  pl.BlockSpec(memory_space=pl.ANY)],
        out_specs=(pl.BlockSpec((tm, D), lambda i: (jnp.minimum(i, nctx - 1), 0)),
                   pl.BlockSpec((tm, D), lambda i: (jnp.maximum(i - nctx, 0), 0))),
        scratch_shapes=[pltpu.VMEM((K, tm, D), F32), pltpu.VMEM((K, tm, D), F32), pltpu.SemaphoreType.DMA((2,))],
        compiler_params=_params(("arbitrary",)),
        name="shared_combine_final",
    )(pos3, pos3, h2p, x1, wtok, mods, sw1, sw3, sw2, fnorm, ys)


def _rope_tables(dec_seq, hd):
    nf = hd // 4
    t = jnp.arange(dec_seq)
    inv = ROPE_BASE ** (-jnp.arange(nf, dtype=F32) / nf)
    ang_r = (t // GRID_W).astype(F32)[:, None] * inv
    ang_c = (t % GRID_W).astype(F32)[:, None] * inv
    cos = jnp.concatenate([jnp.cos(ang_r)] * 2 + [jnp.cos(ang_c)] * 2, axis=1)
    sin = jnp.concatenate([-jnp.sin(ang_r), jnp.sin(ang_r), -jnp.sin(ang_c), jnp.sin(ang_c)], axis=1)
    return cos, sin


def _scan_steps(batch, seq, dec_batch, dec_seq, L):
    fb, bb, sq, fi, la = [], [], [], [], []
    base = 0
    for sid, S in enumerate([seq] * batch + [dec_seq] * dec_batch):
        nc = S // L
        for c in range(nc):
            fb.append(base + c)
            bb.append(base + nc - 1 - c)
            sq.append(sid)
            fi.append(int(c == 0))
            la.append(int(c == nc - 1))
        base += nc
    return tuple(jnp.asarray(np.asarray(a, dtype=np.int32)) for a in (fb, bb, sq, fi, la))


def kernel(x_prompt, x_sample, cache_k, cache_v, state_mlstm_C, state_mlstm_n, state_mlstm_m, c, c_ctx,
           w_mod, b_mod, norm1_w, norm2_w, w_in, igate_b, fgate_b, mlstm_norm_w, attn_sink,
           w_branch_m, w_branch_a, w_out, router_w, router_b, expert_w1, expert_w3, expert_w2,
           shared_w1, shared_w3, shared_w2, final_norm_w):
    batch, seq, D = x_prompt.shape
    dec_batch, dec_seq, _ = x_sample.shape
    depth = w_in.shape[0]
    assert depth == 1, "single trunk layer"
    _, _, past, kvh, hd = cache_k.shape
    mh, dk, dv = state_mlstm_C.shape[3:]
    ah = attn_sink.shape[1]
    groups = ah // kvh
    E = router_w.shape[2]
    assert dk == dv == hd == LANES_V7X
    t_ctx, t_lat = batch * seq, dec_batch * dec_seq
    mw, qw, kw = mh * dk, ah * hd, kvh * hd
    ng = 4 * mh

    wi = w_in[0]
    o = 0
    seg = {}
    for name, width in (("qm", mw), ("km", mw), ("vm", mw), ("om", mw), ("im", 2 * mh), ("fm", 2 * mh),
                        ("qa", qw), ("ka", kw), ("va", kw), ("gm", D), ("ga", D)):
        seg[name] = wi[:, o:o + width]
        o += width
    order = ("qa", "gm", "ga", "qm", "km", "vm", "om", "ka", "va")
    w_main = jnp.concatenate([seg[nm] for nm in order], axis=1).astype(BF16)
    col = {}
    o = 0
    for nm in order:
        col[nm] = o
        o += seg[nm].shape[1]
    tn = 2 * kw
    for nm in order[:-2]:
        assert col[nm] % tn == 0 and seg[nm].shape[1] % tn == 0
    for nm, width in (("gm", D), ("ga", D), ("qm", mw), ("km", mw), ("vm", mw), ("om", mw), ("ka", kw), ("va", kw)):
        assert col[nm] % width == 0
    w_gate = jnp.pad(jnp.concatenate([seg["im"], seg["fm"]], axis=1), ((0, 0), (0, LANES_V7X - ng))).astype(BF16)

    R = -(-(1 + dec_batch) // 8) * 8
    cond = jnp.concatenate([c_ctx[None, :], c, jnp.zeros((R - 1 - dec_batch, D), F32)], axis=0)
    mods = _modulation(cond, w_mod[0], b_mod[0]).reshape(R, N_MOD, D)

    xp2 = x_prompt.reshape(t_ctx, D)
    xs2 = x_sample.reshape(t_lat, D)
    cos, sin = _rope_tables(dec_seq, hd)
    z, kv32, gates, gates_t = _inproj(xp2, xs2, mods, norm1_w, w_main, w_gate, cos, sin, t_ctx=t_ctx,
                                      dec_seq=dec_seq, tn=tn, n_rope_tiles=qw // tn, n_gates=ng)

    nseq = batch + dec_batch
    C0 = jnp.concatenate([jnp.zeros((batch, 2, mh, dk, dv), F32), state_mlstm_C[:, 0]], axis=0)
    n0 = jnp.concatenate([jnp.zeros((batch, 2, mh, dk), F32), state_mlstm_n[:, 0]], axis=0)
    m0 = jnp.concatenate([jnp.zeros((batch, 2, mh), F32), state_mlstm_m[:, 0]], axis=0)
    s0 = jnp.concatenate([C0, jnp.broadcast_to(n0[..., None], (nseq, 2, mh, dk, dv))], axis=-1)
    s0 = s0.reshape(nseq, 2 * mh, dk, 2 * dv)
    m0 = jnp.broadcast_to(m0.reshape(nseq, 2 * mh, 1), (nseq, 2 * mh, LANES_V7X))
    gate_b = jnp.concatenate([igate_b[0].reshape(-1), fgate_b[0].reshape(-1)])
    steps = _scan_steps(batch, seq, dec_batch, dec_seq, M_CHUNK)
    hf, hb, s_fin, m_fin = _mlstm(z, gates, gates_t, gate_b.reshape(1, ng), gate_b.reshape(ng, 1), s0, m0, steps,
                                  mh=mh, dk=dk, qcol=col["qm"] // mw, kcol=col["km"] // mw, vcol=col["vm"] // mw)

    sink = attn_sink[0]
    ha_ctx = _ctx_attention(sink, z, batch=batch, seq=seq, kvh=kvh, groups=groups, hd=hd,
                            kcol=col["ka"] // kw, vcol=col["va"] // kw)
    ha_lat = _lat_attention(sink, z, cache_k.reshape(dec_batch, depth, past, kw),
                            cache_v.reshape(dec_batch, depth, past, kw), t_ctx=t_ctx, dec_batch=dec_batch,
                            dec_seq=dec_seq, kvh=kvh, groups=groups, hd=hd, kcol=col["ka"] // kw,
                            vcol=col["va"] // kw)

    x1, h2p, logits_t = _outproj(hf, hb, z, ha_ctx, ha_lat, xp2, xs2, mods, mlstm_norm_w, norm2_w,
                                w_branch_m[0].astype(BF16), w_branch_a[0].astype(BF16), w_out[0].astype(BF16),
                                router_w[0].T, t_ctx=t_ctx, dec_seq=dec_seq, mh=mh, dv=dv,
                                omcol=col["om"] // mw, gmcol=col["gm"] // D, gacol=col["ga"] // D)

    eidx, rank, wtok, counts = _router(logits_t, router_b[0])
    pos, blk_e, nact, pad_start, pad_count, n_slots = _dispatch(eidx, rank, counts[:, 0], EXPERT_ROWS)
    xs = _dispatch_rows(pad_start, pad_count, nact, _tile_major(pos, min(TD_DISPATCH, t_ctx, dec_seq)), h2p,
                        n_slots=n_slots, rows=EXPERT_ROWS)
    ys = _moe(blk_e, nact, xs, expert_w1[0].astype(BF16), expert_w3[0].astype(BF16),
              expert_w2[0].astype(BF16), rows=EXPERT_ROWS)

    y_ctx, y_lat = _final(_tile_major(pos, min(TM_FINAL, t_ctx, dec_seq)), h2p, x1, wtok, ys, mods,
                          shared_w1[0].astype(BF16), shared_w3[0].astype(BF16), shared_w2[0].astype(BF16),
                          final_norm_w.reshape(1, D), t_ctx=t_ctx, dec_seq=dec_seq)

    y_prompt = y_ctx.reshape(batch, seq, D)
    y_sample = y_lat.reshape(dec_batch, dec_seq, D)
    new_k = kv32[:t_ctx, :kw].reshape(batch, 1, seq, kvh, hd)
    new_v = kv32[:t_ctx, kw:].reshape(batch, 1, seq, kvh, hd)
    s_ctx = s_fin[:batch].reshape(batch, 1, 2, mh, dk, 2 * dv)
    new_C = s_ctx[..., :dv]
    new_n = s_ctx[..., dv]
    new_m = m_fin[:batch, :, 0].reshape(batch, 1, 2, mh)
    return y_prompt, y_sample, new_k, new_v, new_C, new_n, new_m
```

```python
import functools

import numpy as np
import jax
import jax.numpy as jnp
from jax import lax
from jax.experimental import pallas as pl
from jax.experimental.pallas import tpu as pltpu

TOP_K = 6
N_GROUPS = 8
TOPK_GROUPS = 4
ROUTED_SCALE = 2.5
WINDOW = 128
Q_BLOCK = 128
GRID_W = 64
ROPE_BASE = 10000.0
M_CHUNK = 128
N_MOD = 6
EPS = 1e-6

LANES_V7X = 128
MXU_COLS_V7X = 256
VMEM_LIMIT_V7X = 56 * 1024 * 1024

TM_PRENORM = 512
TM_INPROJ = 1024
TM_OUTPROJ = 256
TM_FINAL = 128
TN_MOD = 1024
EXPERT_ROWS = 256
TR_ROUTER = 512
TL_SLOTPOS = 2048
TD_DISPATCH = 256

F32 = jnp.float32
BF16 = jnp.bfloat16
_NT = (((1,), (1,)), ((), ()))


def _params(sem):
    return pltpu.CompilerParams(dimension_semantics=sem, vmem_limit_bytes=VMEM_LIMIT_V7X)


def _dot(a, b):
    return jnp.dot(a, b, preferred_element_type=F32)


def _dot_nt(a, b):
    return lax.dot_general(a, b, _NT, preferred_element_type=F32)


def _pack_bf16_pair(x):
    c = x.shape[1] // 2
    lo = pltpu.bitcast(x[:, :c].astype(BF16).astype(F32), jnp.uint32)
    hi = pltpu.bitcast(x[:, c:].astype(BF16).astype(F32), jnp.uint32)
    return (lo >> 16) | (hi & jnp.uint32(0xFFFF0000))


def _rows_to_tiles(ref, x):
    for a in range(ref.shape[1]):
        ref[:, a, :] = x[:, a * LANES_V7X:(a + 1) * LANES_V7X]


def _unpack_bf16_pair(w):
    lo = pltpu.bitcast(w << 16, F32).astype(BF16)
    hi = pltpu.bitcast(w & jnp.uint32(0xFFFF0000), F32).astype(BF16)
    return jnp.concatenate([lo, hi], axis=1)


def _mod_kernel(c_ref, w_ref, b_ref, o_ref):
    s = jax.nn.silu(c_ref[...]).astype(BF16)
    o_ref[...] = _dot(s, w_ref[...].astype(BF16)) + b_ref[...]


def _modulation(cond, w_mod, b_mod):
    R, D = cond.shape
    N = w_mod.shape[1]
    tn = min(TN_MOD, N)
    return pl.pallas_call(
        _mod_kernel,
        out_shape=jax.ShapeDtypeStruct((R, N), F32),
        grid=(N // tn,),
        in_specs=[pl.BlockSpec((R, D), lambda n: (0, 0)),
                  pl.BlockSpec((D, tn), lambda n: (0, n)),
                  pl.BlockSpec((1, tn), lambda n: (0, n))],
        out_specs=pl.BlockSpec((R, tn), lambda n: (0, n)),
        compiler_params=_params(("arbitrary",)),
        name="modulation",
    )(cond, w_mod, b_mod.reshape(1, N))


def _rope_slice(x, cos, sin_signed, first_half):
    swap = jnp.where(first_half, pltpu.roll(x, 96, 1), pltpu.roll(x, 32, 1))
    return x * cos + swap * sin_signed


def _prenorm_kernel(xp_ref, xs_ref, mod_ref, n1_ref, wg_ref, h_ref, g_ref, gt_ref, *, nctx_tiles, n_gates):
    i = pl.program_id(0)
    x = jnp.where(i < nctx_tiles, xp_ref[...], xs_ref[...])
    y = x * lax.rsqrt(jnp.mean(x * x, axis=-1, keepdims=True) + EPS) * n1_ref[...]
    h = (y * (1.0 + mod_ref[0, 1:2, :]) + mod_ref[0, 0:1, :]).astype(BF16)
    h_ref[...] = h
    g = _dot(h, wg_ref[...])
    g_ref[...] = g[:, :n_gates]
    gt_ref[...] = g.T[:n_gates, :]


def _prenorm(x_prompt2, x_sample2, mods, n1, w_gate, *, t_ctx, dec_seq, n_gates):
    t_lat, D = x_sample2.shape
    t_all = t_ctx + t_lat
    tm = min(TM_PRENORM, t_ctx, dec_seq)
    nctx = t_ctx // tm
    per_seq = dec_seq // tm
    return pl.pallas_call(
        functools.partial(_prenorm_kernel, nctx_tiles=nctx, n_gates=n_gates),
        out_shape=(jax.ShapeDtypeStruct((t_all, D), BF16),
                   jax.ShapeDtypeStruct((t_all, n_gates), F32),
                   jax.ShapeDtypeStruct((n_gates, t_all), F32)),
        grid=(t_all // tm,),
        in_specs=[pl.BlockSpec((tm, D), lambda i: (jnp.minimum(i, nctx - 1), 0)),
                  pl.BlockSpec((tm, D), lambda i: (jnp.maximum(i - nctx, 0), 0)),
                  pl.BlockSpec((1, N_MOD, D), lambda i: (jnp.where(i < nctx, 0, 1 + (i - nctx) // per_seq), 0, 0)),
                  pl.BlockSpec((1, D), lambda i: (0, 0)),
                  pl.BlockSpec((D, LANES_V7X), lambda i: (0, 0))],
        out_specs=(pl.BlockSpec((tm, D), lambda i: (i, 0)),
                   pl.BlockSpec((tm, n_gates), lambda i: (i, 0)),
                   pl.BlockSpec((n_gates, tm), lambda i: (0, i))),
        compiler_params=_params(("arbitrary",)),
        name="prenorm",
    )(x_prompt2, x_sample2, mods, n1, w_gate)


def _inproj_kernel(h_ref, w_ref, cos_ref, sin_ref, z_ref, kv_ref, *, nctx_tiles, n_rope_tiles, kv_tile):
    i = pl.program_id(0)
    n = pl.program_id(1)
    is_ctx = i < nctx_tiles
    tn = z_ref.shape[1]
    acc = _dot(h_ref[...], w_ref[...])

    def rope_cols(ncols):
        cos = jnp.where(is_ctx, 1.0, cos_ref[...])
        sin = jnp.where(is_ctx, 0.0, sin_ref[...])
        lane = lax.broadcasted_iota(jnp.int32, cos.shape, 1)
        first_half = (lane % 64) < 32
        return [_rope_slice(acc[:, c:c + LANES_V7X], cos, sin, first_half)
                for c in range(0, ncols, LANES_V7X)]

    @pl.when(n < n_rope_tiles)
    def _():
        z_ref[...] = jnp.concatenate(rope_cols(tn), axis=1).astype(BF16)

    @pl.when(n == kv_tile)
    def _():
        r = jnp.concatenate(rope_cols(tn // 2) + [acc[:, tn // 2:]], axis=1)
        z_ref[...] = r.astype(BF16)
        kv_ref[...] = r

    @pl.when(jnp.logical_and(n >= n_rope_tiles, n != kv_tile))
    def _():
        z_ref[...] = acc.astype(BF16)


def _inproj(h, w_main, cos, sin, *, t_ctx, dec_seq, tn, n_rope_tiles):
    t_all, D = h.shape
    tm = min(TM_INPROJ, t_ctx, dec_seq)
    nctx = t_ctx // tm
    per_seq = dec_seq // tm
    ncols = w_main.shape[1]
    ntile = ncols // tn
    kv_tile = ntile - 1

    def pos_blk(i):
        return jnp.where(i < nctx, 0, (i - nctx) % per_seq)

    kernel = functools.partial(_inproj_kernel, nctx_tiles=nctx, n_rope_tiles=n_rope_tiles, kv_tile=kv_tile)
    return pl.pallas_call(
        kernel,
        out_shape=(jax.ShapeDtypeStruct((t_all, ncols), BF16),
                   jax.ShapeDtypeStruct((t_all, tn), F32)),
        grid=(t_all // tm, ntile),
        in_specs=[pl.BlockSpec((tm, D), lambda i, n: (i, 0)),
                  pl.BlockSpec((D, tn), lambda i, n: (0, n)),
                  pl.BlockSpec((tm, LANES_V7X), lambda i, n: (pos_blk(i), 0)),
                  pl.BlockSpec((tm, LANES_V7X), lambda i, n: (pos_blk(i), 0))],
        out_specs=(pl.BlockSpec((tm, tn), lambda i, n: (i, n)),
                   pl.BlockSpec((tm, tn), lambda i, n: (i, 0))),
        compiler_params=_params(("arbitrary", "arbitrary")),
        name="inproj",
    )(h, w_main, cos, sin)


def _mlstm_kernel(fb, bb, sq, fi, la,
                  qf, kf, vf, qb, kb, vb, gf, gb, gtf, gtb, brow, bcol, s0, m0,
                  hf, hb, s_out, m_out, s_scr, m_scr, *, mh, dk, chunk):
    s = pl.program_id(0)
    L = chunk
    assert L == dk
    scale = dk ** -0.5

    @pl.when(fi[s] == 1)
    def _():
        s_scr[...] = s0[0]
        m_scr[...] = m0[0]

    ri = lax.broadcasted_iota(jnp.int32, (L, L), 0)
    ci = lax.broadcasted_iota(jnp.int32, (L, L), 1)
    low = ri >= ci
    upp = ri <= ci
    low_f = low.astype(F32)
    upp_f = upp.astype(F32)
    ones_blk = jnp.ones((L, dk), BF16)
    hi = lax.Precision.HIGHEST
    refs = ((qf, kf, vf, gf, gtf, hf), (qb, kb, vb, gb, gtb, hb))
    units = [(dr, h) for dr in range(2) for h in range(mh)]
    sl = lambda h: slice(h * dk, (h + 1) * dk)

    gate = []
    for dr, (_, _, _, g_ref, gt_ref, _) in enumerate(refs):
        G = g_ref[...] + brow[...]
        GT = gt_ref[...] + bcol[...]
        ic_col = G[:, dr * mh:(dr + 1) * mh]
        lf_col = jax.nn.log_sigmoid(G[:, (2 + dr) * mh:(3 + dr) * mh])
        ic_row = GT[dr * mh:(dr + 1) * mh, :]
        lf_row = jax.nn.log_sigmoid(GT[(2 + dr) * mh:(3 + dr) * mh, :])
        b_col = jnp.dot(low_f if dr == 0 else upp_f, lf_col, precision=hi, preferred_element_type=F32)
        b_row = jnp.dot(lf_row, upp_f if dr == 0 else low_f, precision=hi, preferred_element_type=F32)
        gate.append((ic_col, ic_row, b_col, b_row))

    S_prev = [s_scr[dr * mh + h] for dr, h in units]
    m_prev = [m_scr[dr * mh + h:dr * mh + h + 1, 0:1] for dr, h in units]
    q = [refs[dr][0][:, sl(h)] for dr, h in units]
    k = [refs[dr][1][:, sl(h)] for dr, h in units]
    v_aug = [jnp.concatenate([refs[dr][2][:, sl(h)], ones_blk], axis=1) for dr, h in units]
    qk = [_dot_nt(q[u], k[u]) for u in range(len(units))]
    qs = [_dot(q[u], S_prev[u].astype(BF16)) for u in range(len(units))]

    sm, w_inter, floor, b_rep = [], [], [], []
    for u, (dr, h) in enumerate(units):
        ic_col, ic_row, b_col, b_row = gate[dr]
        bc = jnp.broadcast_to(b_col[:, h:h + 1], (L, L))
        d = jnp.where(low if dr == 0 else upp, bc - b_row[h:h + 1, :] + ic_row[h:h + 1, :], -jnp.inf)
        inter = bc + m_prev[u]
        m_t = jnp.maximum(inter, jnp.broadcast_to(jnp.max(d, axis=-1, keepdims=True), (L, L)))
        sm.append((qk[u] * scale * jnp.exp(d - m_t)).astype(BF16))
        w_inter.append(jnp.exp(inter - m_t))
        floor.append(jnp.exp(-m_t))
        b_rep.append(bc)

    sv = [_dot(sm[u], v_aug[u]) for u in range(len(units))]
    for u, (dr, h) in enumerate(units):
        num = sv[u][:, :dk] + w_inter[u] * qs[u][:, :dk]
        den = sv[u][:, dk:] + w_inter[u] * qs[u][:, dk:]
        refs[dr][5][:, sl(h)] = (num / jnp.maximum(jnp.abs(den), floor[u])).astype(BF16)

    kw_t, wc, m_new = [], [], []
    for u, (dr, h) in enumerate(units):
        ic_col = gate[dr][0]
        bc = b_rep[u]
        b_last = bc[L - 1:L, :] if dr == 0 else bc[0:1, :]
        g = b_last - bc + jnp.broadcast_to(ic_col[:, h:h + 1], (L, L))
        mn = jnp.maximum(b_last + m_prev[u], jnp.max(g, axis=0, keepdims=True))
        kw_t.append((k[u].astype(F32) * (jnp.exp(g - mn) * scale)).T.astype(BF16))
        wc.append(jnp.exp(b_last + m_prev[u] - mn))
        m_new.append(mn)

    upd = [_dot(kw_t[u], v_aug[u]) for u in range(len(units))]
    for u, (dr, h) in enumerate(units):
        r = dr * mh + h
        s_scr[r] = jnp.concatenate([wc[u], wc[u]], axis=1) * S_prev[u] + upd[u]
        m_scr[r:r + 1, :] = m_new[u]

    @pl.when(la[s] == 1)
    def _():
        s_out[0] = s_scr[...]
        m_out[0] = m_scr[...]


def _mlstm(z, gates, gates_t, brow, bcol, s0, m0, steps, *, mh, dk, qcol, kcol, vcol):
    t_all = z.shape[0]
    L = M_CHUNK
    mw = mh * dk
    ng = gates.shape[1]
    nseq = s0.shape[0]
    fb, bb, sq, fi, la = steps
    nsteps = fb.shape[0]

    def zspec(which, col):
        return pl.BlockSpec((L, mw), lambda s, fb, bb, sq, fi, la: ((fb, bb)[which][s], col))

    def gspec(which):
        return pl.BlockSpec((L, ng), lambda s, fb, bb, sq, fi, la: ((fb, bb)[which][s], 0))

    def gtspec(which):
        return pl.BlockSpec((ng, L), lambda s, fb, bb, sq, fi, la: (0, (fb, bb)[which][s]))

    grid_spec = pltpu.PrefetchScalarGridSpec(
        num_scalar_prefetch=5,
        grid=(nsteps,),
        in_specs=[zspec(0, qcol), zspec(0, kcol), zspec(0, vcol),
                  zspec(1, qcol), zspec(1, kcol), zspec(1, vcol),
                  gspec(0), gspec(1), gtspec(0), gtspec(1),
                  pl.BlockSpec((1, ng), lambda s, *_: (0, 0)),
                  pl.BlockSpec((ng, 1), lambda s, *_: (0, 0)),
                  pl.BlockSpec((1, 2 * mh, dk, 2 * dk), lambda s, fb, bb, sq, fi, la: (sq[s], 0, 0, 0)),
                  pl.BlockSpec((1, 2 * mh, LANES_V7X), lambda s, fb, bb, sq, fi, la: (sq[s], 0, 0))],
        out_specs=(pl.BlockSpec((L, mw), lambda s, fb, bb, sq, fi, la: (fb[s], 0)),
                   pl.BlockSpec((L, mw), lambda s, fb, bb, sq, fi, la: (bb[s], 0)),
                   pl.BlockSpec((1, 2 * mh, dk, 2 * dk), lambda s, fb, bb, sq, fi, la: (sq[s], 0, 0, 0)),
                   pl.BlockSpec((1, 2 * mh, LANES_V7X), lambda s, fb, bb, sq, fi, la: (sq[s], 0, 0))),
        scratch_shapes=[pltpu.VMEM((2 * mh, dk, 2 * dk), F32), pltpu.VMEM((2 * mh, LANES_V7X), F32)],
    )
    return pl.pallas_call(
        functools.partial(_mlstm_kernel, mh=mh, dk=dk, chunk=L),
        out_shape=(jax.ShapeDtypeStruct((t_all, mw), BF16),
                   jax.ShapeDtypeStruct((t_all, mw), BF16),
                   jax.ShapeDtypeStruct((nseq, 2 * mh, dk, 2 * dk), F32),
                   jax.ShapeDtypeStruct((nseq, 2 * mh, LANES_V7X), F32)),
        grid_spec=grid_spec,
        compiler_params=_params(("arbitrary",)),
        name="mlstm",
    )(fb, bb, sq, fi, la, z, z, z, z, z, z, gates, gates, gates_t, gates_t, brow, bcol, s0, m0)


def _sink_column(sink_ref, kv, groups, rows_per_group):
    row_g = lax.broadcasted_iota(jnp.int32, (groups * rows_per_group, 1), 0) // rows_per_group
    col = jnp.full((groups * rows_per_group, 1), sink_ref[kv * groups], F32)
    for g in range(1, groups):
        col = jnp.where(row_g == g, sink_ref[kv * groups + g], col)
    return col


def _softmax_av(scores, values, sink_col):
    mx = sink_col
    for s in scores:
        mx = jnp.maximum(mx, jnp.max(s, axis=-1, keepdims=True))
    den = jnp.exp(sink_col - mx)
    out = None
    for s, v in zip(scores, values):
        p = jnp.exp(s - mx)
        den = den + jnp.sum(p, axis=-1, keepdims=True)
        pv = _dot(p.astype(BF16), v)
        out = pv if out is None else out + pv
    return out / den


def _ctx_attn_kernel(sink_ref, q_ref, k_ref, v_ref, o_ref, *, kvh, groups, hd):
    S = q_ref.shape[0]
    scale = hd ** -0.5
    for kv in range(kvh):
        k = k_ref[:, kv * hd:(kv + 1) * hd]
        v = v_ref[:, kv * hd:(kv + 1) * hd]
        q = jnp.concatenate([q_ref[:, (kv * groups + g) * hd:(kv * groups + g + 1) * hd]
                             for g in range(groups)], axis=0)
        o = _softmax_av([_dot_nt(q, k) * scale], [v], _sink_column(sink_ref, kv, groups, S))
        for g in range(groups):
            o_ref[:, (kv * groups + g) * hd:(kv * groups + g + 1) * hd] = o[g * S:(g + 1) * S].astype(BF16)


def _ctx_attention(sink, z, *, batch, seq, kvh, groups, hd, kcol, vcol):
    qw = kvh * groups * hd
    kw = kvh * hd
    return pl.pallas_call(
        functools.partial(_ctx_attn_kernel, kvh=kvh, groups=groups, hd=hd),
        out_shape=jax.ShapeDtypeStruct((batch * seq, qw), BF16),
        grid=(batch,),
        in_specs=[pl.BlockSpec(memory_space=pltpu.SMEM),
                  pl.BlockSpec((seq, qw), lambda b: (b, 0)),
                  pl.BlockSpec((seq, kw), lambda b: (b, kcol)),
                  pl.BlockSpec((seq, kw), lambda b: (b, vcol))],
        out_specs=pl.BlockSpec((seq, qw), lambda b: (b, 0)),
        compiler_params=_params(("arbitrary",)),
        name="ctx_attention",
    )(sink, z, z, z)


def _lat_attn_kernel(sink_ref, q_ref, kp_ref, kc_ref, kn_ref, vp_ref, vc_ref, vn_ref, ck_ref, cv_ref, o_ref,
                     *, kvh, groups, hd):
    j = pl.program_id(1)
    nb = pl.num_programs(1)
    Q = q_ref.shape[0]
    scale = hd ** -0.5
    R = groups * Q
    rq = lax.broadcasted_iota(jnp.int32, (R, Q), 0) % Q
    cc = lax.broadcasted_iota(jnp.int32, (R, Q), 1)
    mask_prev = jnp.logical_and(cc >= rq, j > 0)
    mask_next = jnp.logical_and(cc <= rq, j < nb - 1)
    for kv in range(kvh):
        sl = slice(kv * hd, (kv + 1) * hd)
        q = jnp.concatenate([q_ref[:, (kv * groups + g) * hd:(kv * groups + g + 1) * hd]
                             for g in range(groups)], axis=0)
        s_prev = jnp.where(mask_prev, _dot_nt(q, kp_ref[:, sl]) * scale, -jnp.inf)
        s_cur = _dot_nt(q, kc_ref[:, sl]) * scale
        s_next = jnp.where(mask_next, _dot_nt(q, kn_ref[:, sl]) * scale, -jnp.inf)
        s_ctx = _dot_nt(q, ck_ref[0, 0, :, sl].astype(BF16)) * scale
        o = _softmax_av([s_prev, s_cur, s_next, s_ctx],
                        [vp_ref[:, sl], vc_ref[:, sl], vn_ref[:, sl], cv_ref[0, 0, :, sl].astype(BF16)],
                        _sink_column(sink_ref, kv, groups, Q))
        for g in range(groups):
            o_ref[:, (kv * groups + g) * hd:(kv * groups + g + 1) * hd] = o[g * Q:(g + 1) * Q].astype(BF16)


def _lat_attention(sink, z, cache_k, cache_v, *, t_ctx, dec_batch, dec_seq, kvh, groups, hd, kcol, vcol):
    assert WINDOW == Q_BLOCK
    Q = Q_BLOCK
    nb = dec_seq // Q
    base = t_ctx // Q
    qw = kvh * groups * hd
    kw = kvh * hd
    past = cache_k.shape[2]

    def kvspec(col, shift):
        return pl.BlockSpec((Q, kw), lambda b, j: (base + b * nb + jnp.clip(j + shift, 0, nb - 1), col))

    cspec = pl.BlockSpec((1, 1, past, kw), lambda b, j: (b, 0, 0, 0))
    return pl.pallas_call(
        functools.partial(_lat_attn_kernel, kvh=kvh, groups=groups, hd=hd),
        out_shape=jax.ShapeDtypeStruct((dec_batch * dec_seq, qw), BF16),
        grid=(dec_batch, nb),
        in_specs=[pl.BlockSpec(memory_space=pltpu.SMEM),
                  pl.BlockSpec((Q, qw), lambda b, j: (base + b * nb + j, 0)),
                  kvspec(kcol, -1), kvspec(kcol, 0), kvspec(kcol, 1),
                  kvspec(vcol, -1), kvspec(vcol, 0), kvspec(vcol, 1),
                  cspec, cspec],
        out_specs=pl.BlockSpec((Q, qw), lambda b, j: (b * nb + j, 0)),
        compiler_params=_params(("arbitrary", "arbitrary")),
        name="lat_attention",
    )(sink, z, z, z, z, z, z, z, cache_k, cache_v)


def _outproj_kernel(hf_ref, hb_ref, om_ref, hac_ref, hal_ref, gm_ref, ga_ref, xp_ref, xs_ref, mod_ref,
                    mn_ref, n2_ref, wm_ref, wa_ref, wo_ref, rw_ref,
                    x1_ref, h2_ref, h2t_ref, lg_ref, *, nctx_tiles, mh, dv):
    i = pl.program_id(0)
    is_ctx = i < nctx_tiles
    hm = hf_ref[...].astype(F32) + hb_ref[...].astype(F32)
    parts = []
    for h in range(mh):
        sl = hm[:, h * dv:(h + 1) * dv]
        parts.append(sl * lax.rsqrt(jnp.mean(sl * sl, axis=-1, keepdims=True) + EPS))
    hmn = jnp.concatenate(parts, axis=1) * mn_ref[...] * jax.nn.sigmoid(om_ref[...].astype(F32))
    ha = jnp.where(is_ctx, hac_ref[...], hal_ref[...])
    y = (jax.nn.sigmoid(gm_ref[...].astype(F32)) * _dot(hmn.astype(BF16), wm_ref[...])
         + jax.nn.sigmoid(ga_ref[...].astype(F32)) * _dot(ha, wa_ref[...]))
    x = jnp.where(is_ctx, xp_ref[...], xs_ref[...])
    x1 = x + mod_ref[0, 2:3, :] * _dot(y.astype(BF16), wo_ref[...])
    x1_ref[...] = x1
    n = x1 * lax.rsqrt(jnp.mean(x1 * x1, axis=-1, keepdims=True) + EPS) * n2_ref[...]
    h2 = n * (1.0 + mod_ref[0, 4:5, :]) + mod_ref[0, 3:4, :]
    h2p = _pack_bf16_pair(h2)
    h2_ref[...] = h2p
    _rows_to_tiles(h2t_ref, h2p)
    lg_ref[...] = lax.dot_general(rw_ref[...], h2, _NT, precision=lax.Precision.HIGHEST,
                                  preferred_element_type=F32)


def _outproj(hf, hb, z, ha_ctx, ha_lat, x_prompt2, x_sample2, mods, mnorm, n2, wm, wa, wo, rw_t,
             *, t_ctx, dec_seq, mh, dv, omcol, gmcol, gacol):
    t_all = hf.shape[0]
    D = x_prompt2.shape[1]
    mw = mh * dv
    qw = ha_ctx.shape[1]
    E = rw_t.shape[0]
    tm = min(TM_OUTPROJ, t_ctx, dec_seq)
    nctx = t_ctx // tm
    per_seq = dec_seq // tm

    def ctx_blk(i):
        return (jnp.minimum(i, nctx - 1), 0)

    def lat_blk(i):
        return (jnp.maximum(i - nctx, 0), 0)

    def mod_row(i):
        return (jnp.where(i < nctx, 0, 1 + (i - nctx) // per_seq), 0, 0)

    const = lambda i: (0, 0)
    single = pl.Buffered(1)
    return pl.pallas_call(
        functools.partial(_outproj_kernel, nctx_tiles=nctx, mh=mh, dv=dv),
        out_shape=(jax.ShapeDtypeStruct((t_all, D), F32),
                   jax.ShapeDtypeStruct((t_all, D // 2), jnp.uint32),
                   jax.ShapeDtypeStruct((t_all, D // 2 // LANES_V7X, LANES_V7X), jnp.uint32),
                   jax.ShapeDtypeStruct((E, t_all), F32)),
        grid=(t_all // tm,),
        in_specs=[pl.BlockSpec((tm, mw), lambda i: (i, 0)),
                  pl.BlockSpec((tm, mw), lambda i: (i, 0)),
                  pl.BlockSpec((tm, mw), lambda i: (i, omcol)),
                  pl.BlockSpec((tm, qw), ctx_blk),
                  pl.BlockSpec((tm, qw), lat_blk),
                  pl.BlockSpec((tm, D), lambda i: (i, gmcol)),
                  pl.BlockSpec((tm, D), lambda i: (i, gacol)),
                  pl.BlockSpec((tm, D), ctx_blk),
                  pl.BlockSpec((tm, D), lat_blk),
                  pl.BlockSpec((1, N_MOD, D), mod_row),
                  pl.BlockSpec((1, mw), const),
                  pl.BlockSpec((1, D), const),
                  pl.BlockSpec((mw, D), const, pipeline_mode=single),
                  pl.BlockSpec((qw, D), const, pipeline_mode=single),
                  pl.BlockSpec((D, D), const, pipeline_mode=single),
                  pl.BlockSpec((E, D), const, pipeline_mode=single)],
        out_specs=(pl.BlockSpec((tm, D), lambda i: (i, 0)),
                   pl.BlockSpec((tm, D // 2), lambda i: (i, 0)),
                   pl.BlockSpec((tm, D // 2 // LANES_V7X, LANES_V7X), lambda i: (i, 0, 0)),
                   pl.BlockSpec((E, tm), lambda i: (0, i))),
        compiler_params=_params(("arbitrary",)),
        name="outproj",
    )(hf, hb, z, ha_ctx, ha_lat, z, z, x_prompt2, x_sample2, mods, mnorm, n2, wm, wa, wo, rw_t)


def _dispatch_kernel(pstart_ref, pcount_ref, nact_ref, pos_ref, h2p_ref, xs_hbm, zblk, sem, psem, *, top_k):
    i = pl.program_id(0)
    td = h2p_ref.shape[0]
    rows = zblk.shape[0]

    @pl.when(i == 0)
    def _():
        zblk[...] = jnp.zeros_like(zblk)
        zrow = zblk.at[0]

        def per_expert(e, total):
            def fill(r, carry):
                pltpu.make_async_copy(zrow, xs_hbm.at[pstart_ref[e] + r], psem).start()
                return carry

            lax.fori_loop(0, pcount_ref[e], fill, 0)
            return total + pcount_ref[e]

        total = lax.fori_loop(0, pstart_ref.shape[0], per_expert, 0)

        def drain(j, carry):
            pltpu.make_async_copy(zrow, xs_hbm.at[0], psem).wait()
            return carry

        lax.fori_loop(0, total, drain, 0)

        def empty_block(b, carry):
            fill = pltpu.make_async_copy(zblk, xs_hbm.at[pl.ds(b * rows, rows)], psem)
            fill.start()
            fill.wait()
            return carry

        lax.fori_loop(nact_ref[0], xs_hbm.shape[0] // rows, empty_block, 0)

    for k in range(top_k):
        for t in range(td):
            pltpu.make_async_copy(h2p_ref.at[t], xs_hbm.at[pos_ref[0, k, t]], sem).start()
    for k in range(top_k):
        pltpu.make_async_copy(h2p_ref, xs_hbm.at[pl.ds(0, td)], sem).wait()


def _dispatch_rows(pad_start, pad_count, nact, pos3, h2p, *, n_slots, rows):
    nt, K, td = pos3.shape
    T, c, _ = h2p.shape
    grid_spec = pltpu.PrefetchScalarGridSpec(
        num_scalar_prefetch=3,
        grid=(nt,),
        in_specs=[pl.BlockSpec((1, K, td), lambda i, *_: (i, 0, 0), memory_space=pltpu.SMEM),
                  pl.BlockSpec((td, c, LANES_V7X), lambda i, *_: (i, 0, 0))],
        out_specs=pl.BlockSpec(memory_space=pl.ANY),
        scratch_shapes=[pltpu.VMEM((rows, c, LANES_V7X), jnp.uint32), pltpu.SemaphoreType.DMA(()),
                        pltpu.SemaphoreType.DMA(())],
    )
    return pl.pallas_call(
        functools.partial(_dispatch_kernel, top_k=K),
        out_shape=jax.ShapeDtypeStruct((n_slots, c, LANES_V7X), jnp.uint32),
        grid_spec=grid_spec,
        compiler_params=_params(("arbitrary",)),
        name="dispatch_rows",
    )(pad_start, pad_count, nact, pos3, h2p)


def _moe_kernel(blk_e, nact_ref, w1_ref, w3_ref, w2_ref, xs_hbm, y_ref, xbuf0, xbuf1, sem, *, rows):
    i = pl.program_id(0)
    nact = nact_ref[0]
    c = xs_hbm.shape[1]

    def fetch(b, buf, s):
        for a in range(c):
            pltpu.make_async_copy(xs_hbm.at[pl.ds(b * rows, rows), a, :],
                                  buf.at[:, pl.ds(a * LANES_V7X, LANES_V7X)], s).start()

    def fetch_wait(buf, s):
        for a in range(c):
            pltpu.make_async_copy(xs_hbm.at[pl.ds(0, rows), a, :],
                                  buf.at[:, pl.ds(a * LANES_V7X, LANES_V7X)], s).wait()

    @pl.when(i == 0)
    def _():
        fetch(0, xbuf0, sem.at[0])

    def block(cur, sem_cur, nxt, sem_nxt):
        fetch_wait(cur, sem_cur)

        @pl.when(i + 1 < nact)
        def _():
            fetch(i + 1, nxt, sem_nxt)

        x = _unpack_bf16_pair(cur[...])
        hmid = (jax.nn.silu(_dot(x, w1_ref[0])) * _dot(x, w3_ref[0])).astype(BF16)
        y_ref[...] = _pack_bf16_pair(_dot(hmid, w2_ref[0]))

    @pl.when(jnp.logical_and(i < nact, i % 2 == 0))
    def _():
        block(xbuf0, sem.at[0], xbuf1, sem.at[1])

    @pl.when(jnp.logical_and(i < nact, i % 2 == 1))
    def _():
        block(xbuf1, sem.at[1], xbuf0, sem.at[0])

    @pl.when(i >= nact)
    def _():
        y_ref[...] = jnp.zeros_like(y_ref)


def _moe(blk_e, nact, xs, w1, w3, w2, *, rows):
    nblk = blk_e.shape[0]
    E, D, F = w1.shape
    grid_spec = pltpu.PrefetchScalarGridSpec(
        num_scalar_prefetch=2,
        grid=(nblk,),
        in_specs=[pl.BlockSpec((1, D, F), lambda i, be, na: (be[i], 0, 0)),
                  pl.BlockSpec((1, D, F), lambda i, be, na: (be[i], 0, 0)),
                  pl.BlockSpec((1, F, D), lambda i, be, na: (be[i], 0, 0)),
                  pl.BlockSpec(memory_space=pl.ANY)],
        out_specs=pl.BlockSpec((rows, D // 2), lambda i, be, na: (i, 0)),
        scratch_shapes=[pltpu.VMEM((rows, D // 2), jnp.uint32), pltpu.VMEM((rows, D // 2), jnp.uint32),
                        pltpu.SemaphoreType.DMA((2,))],
    )
    return pl.pallas_call(
        functools.partial(_moe_kernel, rows=rows),
        out_shape=jax.ShapeDtypeStruct((nblk * rows, D // 2), jnp.uint32),
        grid_spec=grid_spec,
        compiler_params=_params(("arbitrary",)),
        name="routed_experts",
    )(blk_e, nact, w1, w3, w2, xs)


def _router_kernel(lg_ref, rb_ref, eidx_ref, rank_ref, wtok_ref, cnt_ref, carry, *, top_k, n_groups, topk_groups):
    i = pl.program_id(0)

    @pl.when(i == 0)
    def _():
        carry[...] = jnp.zeros_like(carry)

    E, tr = lg_ref.shape
    gs = E // n_groups
    scores = jax.nn.sigmoid(lg_ref[...])
    biased = scores + rb_ref[...]
    b3 = biased.reshape(n_groups, gs, tr)
    io3 = lax.broadcasted_iota(jnp.int32, b3.shape, 1)
    m1 = jnp.max(b3, axis=1, keepdims=True)
    i1 = jnp.min(jnp.where(b3 == m1, io3, gs), axis=1, keepdims=True)
    m2 = jnp.max(jnp.where(io3 == i1, -jnp.inf, b3), axis=1, keepdims=True)
    grp = (m1 + m2).reshape(n_groups, tr)
    iog = lax.broadcasted_iota(jnp.int32, grp.shape, 0)
    sel = jnp.zeros(grp.shape, jnp.bool_)
    for _ in range(topk_groups):
        mx = jnp.max(grp, axis=0, keepdims=True)
        hit = iog == jnp.min(jnp.where(grp == mx, iog, n_groups), axis=0, keepdims=True)
        sel = jnp.logical_or(sel, hit)
        grp = jnp.where(hit, -jnp.inf, grp)
    masked = jnp.where(sel.reshape(n_groups, 1, tr), b3, -jnp.inf).reshape(E, tr)
    ioe = lax.broadcasted_iota(jnp.int32, (E, tr), 0)
    onehot = jnp.zeros((E, tr), F32)
    hits, idxs, ws = [], [], []
    for _ in range(top_k):
        mx = jnp.max(masked, axis=0, keepdims=True)
        ix = jnp.min(jnp.where(masked == mx, ioe, E), axis=0, keepdims=True)
        hit = ioe == ix
        hits.append(hit)
        idxs.append(ix)
        ws.append(jnp.sum(jnp.where(hit, scores, 0.0), axis=0, keepdims=True))
        onehot = onehot + hit.astype(F32)
        masked = jnp.where(hit, -jnp.inf, masked)
    wsum = ws[0]
    for w in ws[1:]:
        wsum = wsum + w
    ri = lax.broadcasted_iota(jnp.int32, (tr, tr), 0)
    ci = lax.broadcasted_iota(jnp.int32, (tr, tr), 1)
    before = _dot(onehot.astype(BF16), (ri < ci).astype(BF16)) + carry[...]
    ranks = [jnp.sum(jnp.where(hit, before, 0.0), axis=0, keepdims=True) for hit in hits]
    carry[...] = carry[...] + jnp.sum(onehot, axis=1, keepdims=True)
    cnt_ref[...] = carry[...].astype(jnp.int32)
    eidx_ref[...] = jnp.concatenate(idxs, axis=0)
    rank_ref[...] = jnp.concatenate(ranks, axis=0).astype(jnp.int32)
    wrows = jnp.concatenate([w / wsum * ROUTED_SCALE for w in ws]
                            + [jnp.zeros((LANES_V7X - top_k, tr), F32)], axis=0)
    wtok_ref[...] = wrows.T


def _router(logits_t, router_b):
    E, T = logits_t.shape
    tr = min(TR_ROUTER, T)
    return pl.pallas_call(
        functools.partial(_router_kernel, top_k=TOP_K, n_groups=N_GROUPS, topk_groups=TOPK_GROUPS),
        out_shape=(jax.ShapeDtypeStruct((TOP_K, T), jnp.int32),
                   jax.ShapeDtypeStruct((TOP_K, T), jnp.int32),
                   jax.ShapeDtypeStruct((T, LANES_V7X), F32),
                   jax.ShapeDtypeStruct((E, 1), jnp.int32)),
        grid=(T // tr,),
        in_specs=[pl.BlockSpec((E, tr), lambda i: (0, i)),
                  pl.BlockSpec((E, 1), lambda i: (0, 0))],
        out_specs=(pl.BlockSpec((TOP_K, tr), lambda i: (0, i)),
                   pl.BlockSpec((TOP_K, tr), lambda i: (0, i)),
                   pl.BlockSpec((tr, LANES_V7X), lambda i: (i, 0)),
                   pl.BlockSpec((E, 1), lambda i: (0, 0))),
        scratch_shapes=[pltpu.VMEM((E, 1), F32)],
        compiler_params=_params(("arbitrary",)),
        name="router",
    )(logits_t, router_b.reshape(E, 1))


def _slot_pos_kernel(start_ref, eidx_ref, rank_ref, pos_ref, *, n_experts):
    eidx = eidx_ref[...]
    pos = rank_ref[...]
    for e in range(n_experts):
        pos = pos + jnp.where(eidx == e, start_ref[e], 0)
    pos_ref[...] = pos


def _slot_pos(start_pad, eidx, rank):
    K, T = eidx.shape
    tl = min(TL_SLOTPOS, T)
    return pl.pallas_call(
        functools.partial(_slot_pos_kernel, n_experts=start_pad.shape[0]),
        out_shape=jax.ShapeDtypeStruct((K, T), jnp.int32),
        grid=(T // tl,),
        in_specs=[pl.BlockSpec(memory_space=pltpu.SMEM),
                  pl.BlockSpec((K, tl), lambda i: (0, i)),
                  pl.BlockSpec((K, tl), lambda i: (0, i))],
        out_specs=pl.BlockSpec((K, tl), lambda i: (0, i)),
        compiler_params=_params(("arbitrary",)),
        name="slot_pos",
    )(start_pad, eidx, rank)


def _dispatch(eidx, rank, counts, rows):
    K, T = eidx.shape
    A = K * T
    E = counts.shape[0]
    padded = (counts + rows - 1) // rows * rows
    end_pad = jnp.cumsum(padded)
    start_pad = end_pad - padded
    nblk = -(-(A + E * (rows - 1)) // rows)
    n_slots = nblk * rows
    nact = end_pad[-1] // rows
    blk = jnp.arange(nblk, dtype=jnp.int32)
    blk_e = jnp.sum(end_pad[None, :] <= (blk * rows)[:, None], axis=1).astype(jnp.int32)
    last_e = jnp.sum(end_pad <= (nact - 1) * rows).astype(jnp.int32)
    blk_e = jnp.where(blk < nact, blk_e, last_e)
    pos = _slot_pos(start_pad.astype(jnp.int32), eidx, rank)
    pad_start = (start_pad + counts).astype(jnp.int32)
    pad_count = (padded - counts).astype(jnp.int32)
    return pos, blk_e, nact.astype(jnp.int32).reshape(1), pad_start, pad_count, n_slots


def _tile_major(pos, tile):
    K, T = pos.shape
    return pos.reshape(K, T // tile, tile).transpose(1, 0, 2)


def _final_kernel(pos_ref, posn_ref, h2p_ref, x1_ref, wt_ref, mod_ref, w1_ref, w3_ref, w2_ref, fn_ref, y_hbm,
                  oc_ref, ol_ref, ybuf0, ybuf1, sem, *, top_k, nctx_tiles):
    i = pl.program_id(0)
    nt = pl.num_programs(0)
    tm = x1_ref.shape[0]

    def gather(idx_ref, buf, s):
        for k in range(top_k):
            for t in range(tm):
                pltpu.make_async_copy(y_hbm.at[pl.ds(idx_ref[0, k, t], 1)], buf.at[k, pl.ds(t, 1)], s).start()

    def gather_wait(buf, s):
        for k in range(top_k):
            pltpu.make_async_copy(y_hbm.at[pl.ds(0, tm)], buf.at[k], s).wait()

    @pl.when(i == 0)
    def _():
        gather(pos_ref, ybuf0, sem.at[0])

    def tile(cur, sem_cur, nxt, sem_nxt):
        @pl.when(i + 1 < nt)
        def _():
            gather(posn_ref, nxt, sem_nxt)

        x = _unpack_bf16_pair(h2p_ref[...])
        hmid = (jax.nn.silu(_dot(x, w1_ref[...])) * _dot(x, w3_ref[...])).astype(BF16)
        moe = _dot(hmid, w2_ref[...])
        gather_wait(cur, sem_cur)
        wt = wt_ref[...]
        lo = hi = None
        for k in range(top_k):
            w = cur[k]
            wk = wt[:, k:k + 1]
            lo_k = pltpu.bitcast(w << 16, F32) * wk
            hi_k = pltpu.bitcast(w & jnp.uint32(0xFFFF0000), F32) * wk
            lo = lo_k if lo is None else lo + lo_k
            hi = hi_k if hi is None else hi + hi_k
        moe = moe + jnp.concatenate([lo, hi], axis=1)
        x2 =x1_ref[...] + mod_ref[0, 5:6, :] * moe
        out = x2 * lax.rsqrt(jnp.mean(x2 * x2, axis=-1, keepdims=True) + EPS) * fn_ref[...]

        @pl.when(i < nctx_tiles)
        def _():
            oc_ref[...] = out

        @pl.when(i >= nctx_tiles)
        def _():
            ol_ref[...] = out

    @pl.when(i % 2 == 0)
    def _():
        tile(ybuf0, sem.at[0], ybuf1, sem.at[1])

    @pl.when(i % 2 == 1)
    def _():
        tile(ybuf1, sem.at[1], ybuf0, sem.at[0])


def _final(pos3, h2p, x1, wtok, ys, mods, sw1, sw3, sw2, fnorm, *, t_ctx, dec_seq):
    nt, K, tm = pos3.shape
    t_all, D = x1.shape
    F = sw1.shape[1]
    nctx = t_ctx // tm
    per_seq = dec_seq // tm

    def mod_row(i):
        return (jnp.where(i < nctx, 0, 1 + (i - nctx) // per_seq), 0, 0)

    const = lambda i: (0, 0)
    single = pl.Buffered(1)
    smem_blk = lambda f: pl.BlockSpec((1, K, tm), f, memory_space=pltpu.SMEM)
    return pl.pallas_call(
        functools.partial(_final_kernel, top_k=K, nctx_tiles=nctx),
        out_shape=(jax.ShapeDtypeStruct((t_ctx, D), F32), jax.ShapeDtypeStruct((t_all - t_ctx, D), F32)),
        grid=(nt,),
        in_specs=[smem_blk(lambda i: (i, 0, 0)),
                  smem_blk(lambda i: (jnp.minimum(i + 1, nt - 1), 0, 0)),
                  pl.BlockSpec((tm, D // 2), lambda i: (i, 0)),
                  pl.BlockSpec((tm, D), lambda i: (i, 0)),
                  pl.BlockSpec((tm, LANES_V7X), lambda i: (i, 0)),
                  pl.BlockSpec((1, N_MOD, D), mod_row),
                  pl.BlockSpec((D, F), const, pipeline_mode=single),
                  pl.BlockSpec((D, F), const, pipeline_mode=single),
                  pl.BlockSpec((F, D), const, pipeline_mode=single),
                  pl.BlockSpec((1, D), const),
                  pl.BlockSpec(memory_space=pl.ANY)],
        out_specs=(pl.BlockSpec((tm, D), lambda i: (jnp.minimum(i, nctx - 1), 0)),
                   pl.BlockSpec((tm, D), lambda i: (jnp.maximum(i - nctx, 0), 0))),
        scratch_shapes=[pltpu.VMEM((K, tm, D // 2), jnp.uint32), pltpu.VMEM((K, tm, D // 2), jnp.uint32),
                        pltpu.SemaphoreType.DMA((2,))],
        compiler_params=_params(("arbitrary",)),
        name="shared_combine_final",
    )(pos3, pos3, h2p, x1, wtok, mods, sw1, sw3, sw2, fnorm, ys)


def _rope_tables(dec_seq, hd):
    nf = hd // 4
    t = jnp.arange(dec_seq)
    inv = ROPE_BASE ** (-jnp.arange(nf, dtype=F32) / nf)
    ang_r = (t // GRID_W).astype(F32)[:, None] * inv
    ang_c = (t % GRID_W).astype(F32)[:, None] * inv
    cos = jnp.concatenate([jnp.cos(ang_r)] * 2 + [jnp.cos(ang_c)] * 2, axis=1)
    sin = jnp.concatenate([-jnp.sin(ang_r), jnp.sin(ang_r), -jnp.sin(ang_c), jnp.sin(ang_c)], axis=1)
    return cos, sin


def _scan_steps(batch, seq, dec_batch, dec_seq, L):
    fb, bb, sq, fi, la = [], [], [], [], []
    base = 0
    for sid, S in enumerate([seq] * batch + [dec_seq] * dec_batch):
        nc = S // L
        for c in range(nc):
            fb.append(base + c)
            bb.append(base + nc - 1 - c)
            sq.append(sid)
            fi.append(int(c == 0))
            la.append(int(c == nc - 1))
        base += nc
    return tuple(jnp.asarray(np.asarray(a, dtype=np.int32)) for a in (fb, bb, sq, fi, la))


def kernel(x_prompt, x_sample, cache_k, cache_v, state_mlstm_C, state_mlstm_n, state_mlstm_m, c, c_ctx,
           w_mod, b_mod, norm1_w, norm2_w, w_in, igate_b, fgate_b, mlstm_norm_w, attn_sink,
           w_branch_m, w_branch_a, w_out, router_w, router_b, expert_w1, expert_w3, expert_w2,
           shared_w1, shared_w3, shared_w2, final_norm_w):
    batch, seq, D = x_prompt.shape
    dec_batch, dec_seq, _ = x_sample.shape
    depth = w_in.shape[0]
    assert depth == 1, "single trunk layer"
    _, _, past, kvh, hd = cache_k.shape
    mh, dk, dv = state_mlstm_C.shape[3:]
    ah = attn_sink.shape[1]
    groups = ah // kvh
    E = router_w.shape[2]
    assert dk == dv == hd == LANES_V7X
    t_ctx, t_lat = batch * seq, dec_batch * dec_seq
    mw, qw, kw = mh * dk, ah * hd, kvh * hd
    ng = 4 * mh

    wi = w_in[0]
    o = 0
    seg = {}
    for name, width in (("qm", mw), ("km", mw), ("vm", mw), ("om", mw), ("im", 2 * mh), ("fm", 2 * mh),
                        ("qa", qw), ("ka", kw), ("va", kw), ("gm", D), ("ga", D)):
        seg[name] = wi[:, o:o + width]
        o += width
    order = ("qa", "gm", "ga", "qm", "km", "vm", "om", "ka", "va")
    w_main = jnp.concatenate([seg[nm] for nm in order], axis=1).astype(BF16)
    col = {}
    o = 0
    for nm in order:
        col[nm] = o
        o += seg[nm].shape[1]
    tn = 2 * kw
    for nm in order[:-2]:
        assert col[nm] % tn == 0 and seg[nm].shape[1] % tn == 0
    for nm, width in (("gm", D), ("ga", D), ("qm", mw), ("km", mw), ("vm", mw), ("om", mw), ("ka", kw), ("va", kw)):
        assert col[nm] % width == 0
    w_gate = jnp.pad(jnp.concatenate([seg["im"], seg["fm"]], axis=1), ((0, 0), (0, LANES_V7X - ng))).astype(BF16)

    R = -(-(1 + dec_batch) // 8) * 8
    cond = jnp.concatenate([c_ctx[None, :], c, jnp.zeros((R - 1 - dec_batch, D), F32)], axis=0)
    mods = _modulation(cond, w_mod[0], b_mod[0]).reshape(R, N_MOD, D)

    xp2 = x_prompt.reshape(t_ctx, D)
    xs2 = x_sample.reshape(t_lat, D)
    cos, sin = _rope_tables(dec_seq, hd)
    h1, gates, gates_t = _prenorm(xp2, xs2, mods, norm1_w, w_gate, t_ctx=t_ctx, dec_seq=dec_seq, n_gates=ng)
    z, kv32 = _inproj(h1, w_main, cos, sin, t_ctx=t_ctx, dec_seq=dec_seq, tn=tn, n_rope_tiles=qw // tn)

    nseq = batch + dec_batch
    C0 = jnp.concatenate([jnp.zeros((batch, 2, mh, dk, dv), F32), state_mlstm_C[:, 0]], axis=0)
    n0 = jnp.concatenate([jnp.zeros((batch, 2, mh, dk), F32), state_mlstm_n[:, 0]], axis=0)
    m0 = jnp.concatenate([jnp.zeros((batch, 2, mh), F32), state_mlstm_m[:, 0]], axis=0)
    s0 = jnp.concatenate([C0, jnp.broadcast_to(n0[..., None], (nseq, 2, mh, dk, dv))], axis=-1)
    s0 = s0.reshape(nseq, 2 * mh, dk, 2 * dv)
    m0 = jnp.broadcast_to(m0.reshape(nseq, 2 * mh, 1), (nseq, 2 * mh, LANES_V7X))
    gate_b = jnp.concatenate([igate_b[0].reshape(-1), fgate_b[0].reshape(-1)])
    steps = _scan_steps(batch, seq, dec_batch, dec_seq, M_CHUNK)
    hf, hb, s_fin, m_fin = _mlstm(z, gates, gates_t, gate_b.reshape(1, ng), gate_b.reshape(ng, 1), s0, m0, steps,
                                  mh=mh, dk=dk, qcol=col["qm"] // mw, kcol=col["km"] // mw, vcol=col["vm"] // mw)

    sink = attn_sink[0]
    ha_ctx = _ctx_attention(sink, z, batch=batch, seq=seq, kvh=kvh, groups=groups, hd=hd,
                            kcol=col["ka"] // kw, vcol=col["va"] // kw)
    ha_lat = _lat_attention(sink, z, cache_k.reshape(dec_batch, depth, past, kw),
                            cache_v.reshape(dec_batch, depth, past, kw), t_ctx=t_ctx, dec_batch=dec_batch,
                            dec_seq=dec_seq, kvh=kvh, groups=groups, hd=hd, kcol=col["ka"] // kw,
                            vcol=col["va"] // kw)

    x1, h2p, h2t, logits_t = _outproj(hf, hb, z, ha_ctx, ha_lat, xp2, xs2, mods, mlstm_norm_w, norm2_w,
                                w_branch_m[0].astype(BF16), w_branch_a[0].astype(BF16), w_out[0].astype(BF16),
                                router_w[0].T, t_ctx=t_ctx, dec_seq=dec_seq, mh=mh, dv=dv,
                                omcol=col["om"] // mw, gmcol=col["gm"] // D, gacol=col["ga"] // D)

    eidx, rank, wtok, counts = _router(logits_t, router_b[0])
    pos, blk_e, nact, pad_start, pad_count, n_slots = _dispatch(eidx, rank, counts[:, 0], EXPERT_ROWS)
    xs = _dispatch_rows(pad_start, pad_count, nact, _tile_major(pos, min(TD_DISPATCH, t_ctx, dec_seq)), h2t,
                        n_slots=n_slots, rows=EXPERT_ROWS)
    ys = _moe(blk_e, nact, xs, expert_w1[0].astype(BF16), expert_w3[0].astype(BF16),
              expert_w2[0].astype(BF16), rows=EXPERT_ROWS)

    y_ctx, y_lat = _final(_tile_major(pos, min(TM_FINAL, t_ctx, dec_seq)), h2p, x1, wtok, ys, mods,
                          shared_w1[0].astype(BF16), shared_w3[0].astype(BF16), shared_w2[0].astype(BF16),
                          final_norm_w.reshape(1, D), t_ctx=t_ctx, dec_seq=dec_seq)

    y_prompt = y_ctx.reshape(batch, seq, D)
    y_sample = y_lat.reshape(dec_batch, dec_seq, D)
    new_k = kv32[:t_ctx, :kw].reshape(batch, 1, seq, kvh, hd)
    new_v = kv32[:t_ctx, kw:].reshape(batch, 1, seq, kvh, hd)
    s_ctx = s_fin[:batch].reshape(batch, 1, 2, mh, dk, 2 * dv)
    new_C = s_ctx[..., :dv]
    new_n = s_ctx[..., dv]
    new_m = m_fin[:batch, :, 0].reshape(batch, 1, 2, mh)
    return y_prompt, y_sample, new_k, new_v, new_C, new_n, new_m
```

```python
import functools

import numpy as np
import jax
import jax.numpy as jnp
from jax import lax
from jax.experimental import pallas as pl
from jax.experimental.pallas import tpu as pltpu

TOP_K = 6
N_GROUPS = 8
TOPK_GROUPS = 4
ROUTED_SCALE = 2.5
WINDOW = 128
Q_BLOCK = 128
GRID_W = 64
ROPE_BASE = 10000.0
M_CHUNK = 128
N_MOD = 6
EPS = 1e-6

LANES_V7X = 128
MXU_COLS_V7X = 256
VMEM_LIMIT_V7X = 56 * 1024 * 1024

TM_PRENORM = 512
TM_INPROJ = 1024
TM_OUTPROJ = 256
OUTPROJ_SPLIT = 2
TM_FINAL = 256
TN_MOD = 1024
EXPERT_ROWS = 256
TR_ROUTER = 512
TL_SLOTPOS = 2048
TD_DISPATCH = 256

F32 = jnp.float32
BF16 = jnp.bfloat16
_NT = (((1,), (1,)), ((), ()))


def _params(sem):
    return pltpu.CompilerParams(dimension_semantics=sem, vmem_limit_bytes=VMEM_LIMIT_V7X)


def _dot(a, b):
    return jnp.dot(a, b, preferred_element_type=F32)


def _dot_nt(a, b):
    return lax.dot_general(a, b, _NT, preferred_element_type=F32)


def _pack_bf16_pair(x):
    c = x.shape[1] // 2
    lo = pltpu.bitcast(x[:, :c].astype(BF16).astype(F32), jnp.uint32)
    hi = pltpu.bitcast(x[:, c:].astype(BF16).astype(F32), jnp.uint32)
    return (lo >> 16) | (hi & jnp.uint32(0xFFFF0000))


def _rows_to_tiles(ref, x):
    for a in range(ref.shape[1]):
        ref[:, a, :] = x[:, a * LANES_V7X:(a + 1) * LANES_V7X]


def _unpack_bf16_pair(w):
    lo = pltpu.bitcast(w << 16, F32).astype(BF16)
    hi = pltpu.bitcast(w & jnp.uint32(0xFFFF0000), F32).astype(BF16)
    return jnp.concatenate([lo, hi], axis=1)


def _mod_kernel(c_ref, w_ref, b_ref, o_ref):
    s = jax.nn.silu(c_ref[...]).astype(BF16)
    o_ref[...] = _dot(s, w_ref[...].astype(BF16)) + b_ref[...]


def _modulation(cond, w_mod, b_mod):
    R, D = cond.shape
    N = w_mod.shape[1]
    tn = min(TN_MOD, N)
    return pl.pallas_call(
        _mod_kernel,
        out_shape=jax.ShapeDtypeStruct((R, N), F32),
        grid=(N // tn,),
        in_specs=[pl.BlockSpec((R, D), lambda n: (0, 0)),
                  pl.BlockSpec((D, tn), lambda n: (0, n)),
                  pl.BlockSpec((1, tn), lambda n: (0, n))],
        out_specs=pl.BlockSpec((R, tn), lambda n: (0, n)),
        compiler_params=_params(("arbitrary",)),
        name="modulation",
    )(cond, w_mod, b_mod.reshape(1, N))


def _rope_slice(x, cos, sin_signed, first_half):
    swap = jnp.where(first_half, pltpu.roll(x, 96, 1), pltpu.roll(x, 32, 1))
    return x * cos + swap * sin_signed


def _prenorm_kernel(xp_ref, xs_ref, mod_ref, n1_ref, wg_ref, h_ref, g_ref, gt_ref, *, nctx_tiles, n_gates):
    i = pl.program_id(0)
    x = jnp.where(i < nctx_tiles, xp_ref[...], xs_ref[...])
    y = x * lax.rsqrt(jnp.mean(x * x, axis=-1, keepdims=True) + EPS) * n1_ref[...]
    h = (y * (1.0 + mod_ref[0, 1:2, :]) + mod_ref[0, 0:1, :]).astype(BF16)
    h_ref[...] = h
    g = _dot(h, wg_ref[...])
    g_ref[...] = g[:, :n_gates]
    gt_ref[...] = g.T[:n_gates, :]


def _prenorm(x_prompt2, x_sample2, mods, n1, w_gate, *, t_ctx, dec_seq, n_gates):
    t_lat, D = x_sample2.shape
    t_all = t_ctx + t_lat
    tm = min(TM_PRENORM, t_ctx, dec_seq)
    nctx = t_ctx // tm
    per_seq = dec_seq // tm
    return pl.pallas_call(
        functools.partial(_prenorm_kernel, nctx_tiles=nctx, n_gates=n_gates),
        out_shape=(jax.ShapeDtypeStruct((t_all, D), BF16),
                   jax.ShapeDtypeStruct((t_all, n_gates), F32),
                   jax.ShapeDtypeStruct((n_gates, t_all), F32)),
        grid=(t_all // tm,),
        in_specs=[pl.BlockSpec((tm, D), lambda i: (jnp.minimum(i, nctx - 1), 0)),
                  pl.BlockSpec((tm, D), lambda i: (jnp.maximum(i - nctx, 0), 0)),
                  pl.BlockSpec((1, N_MOD, D), lambda i: (jnp.where(i < nctx, 0, 1 + (i - nctx) // per_seq), 0, 0)),
                  pl.BlockSpec((1, D), lambda i: (0, 0)),
                  pl.BlockSpec((D, LANES_V7X), lambda i: (0, 0))],
        out_specs=(pl.BlockSpec((tm, D), lambda i: (i, 0)),
                   pl.BlockSpec((tm, n_gates), lambda i: (i, 0)),
                   pl.BlockSpec((n_gates, tm), lambda i: (0, i))),
        compiler_params=_params(("arbitrary",)),
        name="prenorm",
    )(x_prompt2, x_sample2, mods, n1, w_gate)


def _inproj_kernel(h_ref, w_ref, cos_ref, sin_ref, z_ref, kv_ref, *, nctx_tiles, n_rope_tiles, kv_tile):
    i = pl.program_id(0)
    n = pl.program_id(1)
    is_ctx = i < nctx_tiles
    tn = z_ref.shape[1]
    acc = _dot(h_ref[...], w_ref[...])

    def rope_cols(ncols):
        cos = jnp.where(is_ctx, 1.0, cos_ref[...])
        sin = jnp.where(is_ctx, 0.0, sin_ref[...])
        lane = lax.broadcasted_iota(jnp.int32, cos.shape, 1)
        first_half = (lane % 64) < 32
        return [_rope_slice(acc[:, c:c + LANES_V7X], cos, sin, first_half)
                for c in range(0, ncols, LANES_V7X)]

    @pl.when(n < n_rope_tiles)
    def _():
        z_ref[...] = jnp.concatenate(rope_cols(tn), axis=1).astype(BF16)

    @pl.when(n == kv_tile)
    def _():
        r = jnp.concatenate(rope_cols(tn // 2) + [acc[:, tn // 2:]], axis=1)
        z_ref[...] = r.astype(BF16)
        kv_ref[...] = r

    @pl.when(jnp.logical_and(n >= n_rope_tiles, n != kv_tile))
    def _():
        z_ref[...] = acc.astype(BF16)


def _inproj(h, w_main, cos, sin, *, t_ctx, dec_seq, tn, n_rope_tiles):
    t_all, D = h.shape
    tm = min(TM_INPROJ, t_ctx, dec_seq)
    nctx = t_ctx // tm
    per_seq = dec_seq // tm
    ncols = w_main.shape[1]
    ntile = ncols // tn
    kv_tile = ntile - 1

    def pos_blk(i):
        return jnp.where(i < nctx, 0, (i - nctx) % per_seq)

    kernel = functools.partial(_inproj_kernel, nctx_tiles=nctx, n_rope_tiles=n_rope_tiles, kv_tile=kv_tile)
    return pl.pallas_call(
        kernel,
        out_shape=(jax.ShapeDtypeStruct((t_all, ncols), BF16),
                   jax.ShapeDtypeStruct((t_all, tn), F32)),
        grid=(t_all // tm, ntile),
        in_specs=[pl.BlockSpec((tm, D), lambda i, n: (i, 0)),
                  pl.BlockSpec((D, tn), lambda i, n: (0, n)),
                  pl.BlockSpec((tm, LANES_V7X), lambda i, n: (pos_blk(i), 0)),
                  pl.BlockSpec((tm, LANES_V7X), lambda i, n: (pos_blk(i), 0))],
        out_specs=(pl.BlockSpec((tm, tn), lambda i, n: (i, n)),
                   pl.BlockSpec((tm, tn), lambda i, n: (i, 0))),
        compiler_params=_params(("arbitrary", "arbitrary")),
        name="inproj",
    )(h, w_main, cos, sin)


def _mlstm_kernel(fb, bb, sq, fi, la,
                  qf, kf, vf, qb, kb, vb, gf, gb, gtf, gtb, brow, bcol, s0, m0,
                  hf, hb, s_out, m_out, s_scr, m_scr, *, mh, dk, chunk):
    s = pl.program_id(0)
    L = chunk
    assert L == dk
    scale = dk ** -0.5

    @pl.when(fi[s] == 1)
    def _():
        s_scr[...] = s0[0]
        m_scr[...] = m0[0]

    ri = lax.broadcasted_iota(jnp.int32, (L, L), 0)
    ci = lax.broadcasted_iota(jnp.int32, (L, L), 1)
    low = ri >= ci
    upp = ri <= ci
    low_f = low.astype(F32)
    upp_f = upp.astype(F32)
    ones_blk = jnp.ones((L, dk), BF16)
    hi = lax.Precision.HIGHEST
    refs = ((qf, kf, vf, gf, gtf, hf), (qb, kb, vb, gb, gtb, hb))
    units = [(dr, h) for dr in range(2) for h in range(mh)]
    sl = lambda h: slice(h * dk, (h + 1) * dk)

    gate = []
    for dr, (_, _, _, g_ref, gt_ref, _) in enumerate(refs):
        G = g_ref[...] + brow[...]
        GT = gt_ref[...] + bcol[...]
        ic_col = G[:, dr * mh:(dr + 1) * mh]
        lf_col = jax.nn.log_sigmoid(G[:, (2 + dr) * mh:(3 + dr) * mh])
        ic_row = GT[dr * mh:(dr + 1) * mh, :]
        lf_row = jax.nn.log_sigmoid(GT[(2 + dr) * mh:(3 + dr) * mh, :])
        b_col = jnp.dot(low_f if dr == 0 else upp_f, lf_col, precision=hi, preferred_element_type=F32)
        b_row = jnp.dot(lf_row, upp_f if dr == 0 else low_f, precision=hi, preferred_element_type=F32)
        gate.append((ic_col, ic_row, b_col, b_row))

    S_prev = [s_scr[dr * mh + h] for dr, h in units]
    m_prev = [m_scr[dr * mh + h:dr * mh + h + 1, 0:1] for dr, h in units]
    q = [refs[dr][0][:, sl(h)] for dr, h in units]
    k = [refs[dr][1][:, sl(h)] for dr, h in units]
    v_aug = [jnp.concatenate([refs[dr][2][:, sl(h)], ones_blk], axis=1) for dr, h in units]
    qk = [_dot_nt(q[u], k[u]) for u in range(len(units))]
    qs = [_dot(q[u], S_prev[u].astype(BF16)) for u in range(len(units))]

    sm, w_inter, floor, b_rep = [], [], [], []
    for u, (dr, h) in enumerate(units):
        ic_col, ic_row, b_col, b_row = gate[dr]
        bc = jnp.broadcast_to(b_col[:, h:h + 1], (L, L))
        d = jnp.where(low if dr == 0 else upp, bc - b_row[h:h + 1, :] + ic_row[h:h + 1, :], -jnp.inf)
        inter = bc + m_prev[u]
        m_t = jnp.maximum(inter, jnp.broadcast_to(jnp.max(d, axis=-1, keepdims=True), (L, L)))
        sm.append((qk[u] * scale * jnp.exp(d - m_t)).astype(BF16))
        w_inter.append(jnp.exp(inter - m_t))
        floor.append(jnp.exp(-m_t))
        b_rep.append(bc)

    sv = [_dot(sm[u], v_aug[u]) for u in range(len(units))]
    for u, (dr, h) in enumerate(units):
        num = sv[u][:, :dk] + w_inter[u] * qs[u][:, :dk]
        den = sv[u][:, dk:] + w_inter[u] * qs[u][:, dk:]
        refs[dr][5][:, sl(h)] = (num / jnp.maximum(jnp.abs(den), floor[u])).astype(BF16)

    kw_t, wc, m_new = [], [], []
    for u, (dr, h) in enumerate(units):
        ic_col = gate[dr][0]
        bc = b_rep[u]
        b_last = bc[L - 1:L, :] if dr == 0 else bc[0:1, :]
        g = b_last - bc + jnp.broadcast_to(ic_col[:, h:h + 1], (L, L))
        mn = jnp.maximum(b_last + m_prev[u], jnp.max(g, axis=0, keepdims=True))
        kw_t.append((k[u].astype(F32) * (jnp.exp(g - mn) * scale)).T.astype(BF16))
        wc.append(jnp.exp(b_last + m_prev[u] - mn))
        m_new.append(mn)

    upd = [_dot(kw_t[u], v_aug[u]) for u in range(len(units))]
    for u, (dr, h) in enumerate(units):
        r = dr * mh + h
        s_scr[r] = jnp.concatenate([wc[u], wc[u]], axis=1) * S_prev[u] + upd[u]
        m_scr[r:r + 1, :] = m_new[u]

    @pl.when(la[s] == 1)
    def _():
        s_out[0] = s_scr[...]
        m_out[0] = m_scr[...]


def _mlstm(z, gates, gates_t, brow, bcol, s0, m0, steps, *, mh, dk, qcol, kcol, vcol):
    t_all = z.shape[0]
    L = M_CHUNK
    mw = mh * dk
    ng = gates.shape[1]
    nseq = s0.shape[0]
    fb, bb, sq, fi, la = steps
    nsteps = fb.shape[0]

    def zspec(which, col):
        return pl.BlockSpec((L, mw), lambda s, fb, bb, sq, fi, la: ((fb, bb)[which][s], col))

    def gspec(which):
        return pl.BlockSpec((L, ng), lambda s, fb, bb, sq, fi, la: ((fb, bb)[which][s], 0))

    def gtspec(which):
        return pl.BlockSpec((ng, L), lambda s, fb, bb, sq, fi, la: (0, (fb, bb)[which][s]))

    grid_spec = pltpu.PrefetchScalarGridSpec(
        num_scalar_prefetch=5,
        grid=(nsteps,),
        in_specs=[zspec(0, qcol), zspec(0, kcol), zspec(0, vcol),
                  zspec(1, qcol), zspec(1, kcol), zspec(1, vcol),
                  gspec(0), gspec(1), gtspec(0), gtspec(1),
                  pl.BlockSpec((1, ng), lambda s, *_: (0, 0)),
                  pl.BlockSpec((ng, 1), lambda s, *_: (0, 0)),
                  pl.BlockSpec((1, 2 * mh, dk, 2 * dk), lambda s, fb, bb, sq, fi, la: (sq[s], 0, 0, 0)),
                  pl.BlockSpec((1, 2 * mh, LANES_V7X), lambda s, fb, bb, sq, fi, la: (sq[s], 0, 0))],
        out_specs=(pl.BlockSpec((L, mw), lambda s, fb, bb, sq, fi, la: (fb[s], 0)),
                   pl.BlockSpec((L, mw), lambda s, fb, bb, sq, fi, la: (bb[s], 0)),
                   pl.BlockSpec((1, 2 * mh, dk, 2 * dk), lambda s, fb, bb, sq, fi, la: (sq[s], 0, 0, 0)),
                   pl.BlockSpec((1, 2 * mh, LANES_V7X), lambda s, fb, bb, sq, fi, la: (sq[s], 0, 0))),
        scratch_shapes=[pltpu.VMEM((2 * mh, dk, 2 * dk), F32), pltpu.VMEM((2 * mh, LANES_V7X), F32)],
    )
    return pl.pallas_call(
        functools.partial(_mlstm_kernel, mh=mh, dk=dk, chunk=L),
        out_shape=(jax.ShapeDtypeStruct((t_all, mw), BF16),
                   jax.ShapeDtypeStruct((t_all, mw), BF16),
                   jax.ShapeDtypeStruct((nseq, 2 * mh, dk, 2 * dk), F32),
                   jax.ShapeDtypeStruct((nseq, 2 * mh, LANES_V7X), F32)),
        grid_spec=grid_spec,
        compiler_params=_params(("arbitrary",)),
        name="mlstm",
    )(fb, bb, sq, fi, la, z, z, z, z, z, z, gates, gates, gates_t, gates_t, brow, bcol, s0, m0)


def _sink_column(sink_ref, kv, groups, rows_per_group):
    shape = (groups * rows_per_group, LANES_V7X)
    row_g = lax.broadcasted_iota(jnp.int32, shape, 0) // rows_per_group
    col = jnp.full(shape, sink_ref[kv * groups], F32)
    for g in range(1, groups):
        col = jnp.where(row_g == g, sink_ref[kv * groups + g], col)
    return col


def _softmax_probs(scores, sink):
    hd = sink.shape[1]
    s = jnp.concatenate(scores, axis=1)
    mx = jnp.maximum(jnp.broadcast_to(jnp.max(s, axis=-1, keepdims=True), sink.shape), sink)
    return jnp.exp(s - jnp.concatenate([mx] * (s.shape[1] // hd), axis=1)).astype(BF16), mx


def _weighted_values(p, mx, values, sink):
    hd = sink.shape[1]
    acc = None
    off = 0
    for v in values:
        v_aug = jnp.concatenate([v, jnp.ones(v.shape, BF16)], axis=1)
        pv = _dot(p[:, off:off + v.shape[0]], v_aug)
        acc = pv if acc is None else acc + pv
        off += v.shape[0]
    return acc[:, :hd] / (acc[:, hd:] + jnp.exp(sink - mx))


def _ctx_attn_kernel(sink_ref, q_ref, k_ref, v_ref, o_ref, *, kvh, groups, hd):
    S = q_ref.shape[0]
    scale = hd ** -0.5
    for kv in range(kvh):
        k = k_ref[:, kv * hd:(kv + 1) * hd]
        v = v_ref[:, kv * hd:(kv + 1) * hd]
        q = jnp.concatenate([q_ref[:, (kv * groups + g) * hd:(kv * groups + g + 1) * hd]
                             for g in range(groups)], axis=0)
        sink = _sink_column(sink_ref, kv, groups, S)
        p, mx = _softmax_probs([_dot_nt(q, k) * scale], sink)
        o = _weighted_values(p, mx, [v], sink)
        for g in range(groups):
            o_ref[:, (kv * groups + g) * hd:(kv * groups + g + 1) * hd] = o[g * S:(g + 1) * S].astype(BF16)


def _ctx_attention(sink, z, *, batch, seq, kvh, groups, hd, kcol, vcol):
    qw = kvh * groups * hd
    kw = kvh * hd
    return pl.pallas_call(
        functools.partial(_ctx_attn_kernel, kvh=kvh, groups=groups, hd=hd),
        out_shape=jax.ShapeDtypeStruct((batch * seq, qw), BF16),
        grid=(batch,),
        in_specs=[pl.BlockSpec(memory_space=pltpu.SMEM),
                  pl.BlockSpec((seq, qw), lambda b: (b, 0)),
                  pl.BlockSpec((seq, kw), lambda b: (b, kcol)),
                  pl.BlockSpec((seq, kw), lambda b: (b, vcol))],
        out_specs=pl.BlockSpec((seq, qw), lambda b: (b, 0)),
        compiler_params=_params(("arbitrary",)),
        name="ctx_attention",
    )(sink, z, z, z)


def _lat_attn_kernel(sink_ref, q_ref, kp_ref, kc_ref, kn_ref, vp_ref, vc_ref, vn_ref, ck_ref, cv_ref, o_ref,
                     *, kvh, groups, hd):
    j = pl.program_id(1)
    nb = pl.num_programs(1)
    Q = q_ref.shape[0]
    scale = hd ** -0.5
    R = groups * Q
    rq = lax.broadcasted_iota(jnp.int32, (R, Q), 0) % Q
    cc = lax.broadcasted_iota(jnp.int32, (R, Q), 1)
    mask_prev = jnp.logical_and(cc >= rq, j > 0)
    mask_next = jnp.logical_and(cc <= rq, j < nb - 1)
    heads = [slice(kv * hd, (kv + 1) * hd) for kv in range(kvh)]
    sinks = [_sink_column(sink_ref, kv, groups, Q) for kv in range(kvh)]
    scores = []
    for kv, sl in enumerate(heads):
        q = jnp.concatenate([q_ref[:, (kv * groups + g) * hd:(kv * groups + g + 1) * hd]
                             for g in range(groups)], axis=0)
        scores.append([jnp.where(mask_prev, _dot_nt(q, kp_ref[:, sl]) * scale, -jnp.inf),
                       _dot_nt(q, kc_ref[:, sl]) * scale,
                       jnp.where(mask_next, _dot_nt(q, kn_ref[:, sl]) * scale, -jnp.inf),
                       _dot_nt(q, ck_ref[0, 0, :, sl].astype(BF16)) * scale])
    probs = [_softmax_probs(scores[kv], sinks[kv]) for kv in range(kvh)]
    for kv, sl in enumerate(heads):
        p, mx = probs[kv]
        o = _weighted_values(p, mx, [vp_ref[:, sl], vc_ref[:, sl], vn_ref[:, sl],
                                     cv_ref[0, 0, :, sl].astype(BF16)], sinks[kv])
        for g in range(groups):
            o_ref[:, (kv * groups + g) * hd:(kv * groups + g + 1) * hd] = o[g * Q:(g + 1) * Q].astype(BF16)


def _lat_attention(sink, z, cache_k, cache_v, *, t_ctx, dec_batch, dec_seq, kvh, groups, hd, kcol, vcol):
    assert WINDOW == Q_BLOCK
    Q = Q_BLOCK
    nb = dec_seq // Q
    base = t_ctx // Q
    qw = kvh * groups * hd
    kw = kvh * hd
    past = cache_k.shape[2]

    def kvspec(col, shift):
        return pl.BlockSpec((Q, kw), lambda b, j: (base + b * nb + jnp.clip(j + shift, 0, nb - 1), col))

    cspec = pl.BlockSpec((1, 1, past, kw), lambda b, j: (b, 0, 0, 0))
    return pl.pallas_call(
        functools.partial(_lat_attn_kernel, kvh=kvh, groups=groups, hd=hd),
        out_shape=jax.ShapeDtypeStruct((dec_batch * dec_seq, qw), BF16),
        grid=(dec_batch, nb),
        in_specs=[pl.BlockSpec(memory_space=pltpu.SMEM),
                  pl.BlockSpec((Q, qw), lambda b, j: (base + b * nb + j, 0)),
                  kvspec(kcol, -1), kvspec(kcol, 0), kvspec(kcol, 1),
                  kvspec(vcol, -1), kvspec(vcol, 0), kvspec(vcol, 1),
                  cspec, cspec],
        out_specs=pl.BlockSpec((Q, qw), lambda b, j: (b * nb + j, 0)),
        compiler_params=_params(("arbitrary", "arbitrary")),
        name="lat_attention",
    )(sink, z, z, z, z, z, z, z, cache_k, cache_v)


def _outproj_kernel(hf_ref, hb_ref, om_ref, hac_ref, hal_ref, gm_ref, ga_ref, xp_ref, xs_ref, mod_ref,
                    mn_ref, n2_ref, wm_ref, wa_ref, wo_ref, rw_ref,
                    x1_ref, h2_ref, h2t_ref, lg_ref, *, nctx_tiles, mh, dv):
    i = pl.program_id(0)
    is_ctx = i < nctx_tiles
    tm = x1_ref.shape[0]
    for r0 in range(0, tm, tm // OUTPROJ_SPLIT):
        rs = pl.ds(r0, tm // OUTPROJ_SPLIT)
        hm = hf_ref[rs, :].astype(F32) + hb_ref[rs, :].astype(F32)
        parts = []
        for h in range(mh):
            sl = hm[:, h * dv:(h + 1) * dv]
            parts.append(sl * lax.rsqrt(jnp.mean(sl * sl, axis=-1, keepdims=True) + EPS))
        hmn = jnp.concatenate(parts, axis=1) * mn_ref[...] * jax.nn.sigmoid(om_ref[rs, :].astype(F32))
        ha = jnp.where(is_ctx, hac_ref[rs, :], hal_ref[rs, :])
        y = (jax.nn.sigmoid(gm_ref[rs, :].astype(F32)) * _dot(hmn.astype(BF16), wm_ref[...])
             + jax.nn.sigmoid(ga_ref[rs, :].astype(F32)) * _dot(ha, wa_ref[...]))
        x = jnp.where(is_ctx, xp_ref[rs, :], xs_ref[rs, :])
        x1 = x + mod_ref[0, 2:3, :] * _dot(y.astype(BF16), wo_ref[...])
        x1_ref[rs, :] = x1
        n = x1 * lax.rsqrt(jnp.mean(x1 * x1, axis=-1, keepdims=True) + EPS) * n2_ref[...]
        h2 = n * (1.0 + mod_ref[0, 4:5, :]) + mod_ref[0, 3:4, :]
        h2p = _pack_bf16_pair(h2)
        h2_ref[rs, :] = h2p
        _rows_to_tiles(h2t_ref.at[rs], h2p)
        lg_ref[:, rs] = lax.dot_general(rw_ref[...], h2, _NT, precision=lax.Precision.HIGHEST,
                                        preferred_element_type=F32)


def _outproj(hf, hb, z, ha_ctx, ha_lat, x_prompt2, x_sample2, mods, mnorm, n2, wm, wa, wo, rw_t,
             *, t_ctx, dec_seq, mh, dv, omcol, gmcol, gacol):
    t_all = hf.shape[0]
    D = x_prompt2.shape[1]
    mw = mh * dv
    qw = ha_ctx.shape[1]
    E = rw_t.shape[0]
    tm = min(TM_OUTPROJ, t_ctx, dec_seq)
    nctx = t_ctx // tm
    per_seq = dec_seq // tm

    def ctx_blk(i):
        return (jnp.minimum(i, nctx - 1), 0)

    def lat_blk(i):
        return (jnp.maximum(i - nctx, 0), 0)

    def mod_row(i):
        return (jnp.where(i < nctx, 0, 1 + (i - nctx) // per_seq), 0, 0)

    const = lambda i: (0, 0)
    single = pl.Buffered(1)
    return pl.pallas_call(
        functools.partial(_outproj_kernel, nctx_tiles=nctx, mh=mh, dv=dv),
        out_shape=(jax.ShapeDtypeStruct((t_all, D), F32),
                   jax.ShapeDtypeStruct((t_all, D // 2), jnp.uint32),
                   jax.ShapeDtypeStruct((t_all, D // 2 // LANES_V7X, LANES_V7X), jnp.uint32),
                   jax.ShapeDtypeStruct((E, t_all), F32)),
        grid=(t_all // tm,),
        in_specs=[pl.BlockSpec((tm, mw), lambda i: (i, 0)),
                  pl.BlockSpec((tm, mw), lambda i: (i, 0)),
                  pl.BlockSpec((tm, mw), lambda i: (i, omcol)),
                  pl.BlockSpec((tm, qw), ctx_blk),
                  pl.BlockSpec((tm, qw), lat_blk),
                  pl.BlockSpec((tm, D), lambda i: (i, gmcol)),
                  pl.BlockSpec((tm, D), lambda i: (i, gacol)),
                  pl.BlockSpec((tm, D), ctx_blk),
                  pl.BlockSpec((tm, D), lat_blk),
                  pl.BlockSpec((1, N_MOD, D), mod_row),
                  pl.BlockSpec((1, mw), const),
                  pl.BlockSpec((1, D), const),
                  pl.BlockSpec((mw, D), const, pipeline_mode=single),
                  pl.BlockSpec((qw, D), const, pipeline_mode=single),
                  pl.BlockSpec((D, D), const, pipeline_mode=single),
                  pl.BlockSpec((E, D), const, pipeline_mode=single)],
        out_specs=(pl.BlockSpec((tm, D), lambda i: (i, 0)),
                   pl.BlockSpec((tm, D // 2), lambda i: (i, 0)),
                   pl.BlockSpec((tm, D // 2 // LANES_V7X, LANES_V7X), lambda i: (i, 0, 0)),
                   pl.BlockSpec((E, tm), lambda i: (0, i))),
        compiler_params=_params(("arbitrary",)),
        name="outproj",
    )(hf, hb, z, ha_ctx, ha_lat, z, z, x_prompt2, x_sample2, mods, mnorm, n2, wm, wa, wo, rw_t)


def _dispatch_kernel(pstart_ref, pcount_ref, nact_ref, pos_ref, h2t_ref, h2p_ref, x1_ref, mod_ref,
                     w1_ref, w3_ref, w2_ref, xs_hbm, x1s_ref, zblk, sem, psem, *, top_k):
    i = pl.program_id(0)
    td = h2t_ref.shape[0]
    rows = zblk.shape[0]

    @pl.when(i == 0)
    def _():
        zblk[...] = jnp.zeros_like(zblk)
        zrow = zblk.at[0]

        def per_expert(e, total):
            def fill(r, carry):
                pltpu.make_async_copy(zrow, xs_hbm.at[pstart_ref[e] + r], psem).start()
                return carry

            lax.fori_loop(0, pcount_ref[e], fill, 0)
            return total + pcount_ref[e]

        total = lax.fori_loop(0, pstart_ref.shape[0], per_expert, 0)

        def drain(j, carry):
            pltpu.make_async_copy(zrow, xs_hbm.at[0], psem).wait()
            return carry

        lax.fori_loop(0, total, drain, 0)

        def empty_block(b, carry):
            fill = pltpu.make_async_copy(zblk, xs_hbm.at[pl.ds(b * rows, rows)], psem)
            fill.start()
            fill.wait()
            return carry

        lax.fori_loop(nact_ref[0], xs_hbm.shape[0] // rows, empty_block, 0)

    for k in range(top_k):
        for t in range(td):
            pltpu.make_async_copy(h2t_ref.at[t], xs_hbm.at[pos_ref[0, k, t]], sem).start()

    x = _unpack_bf16_pair(h2p_ref[...])
    hmid = (jax.nn.silu(_dot(x, w1_ref[...])) * _dot(x, w3_ref[...])).astype(BF16)
    x1s_ref[...] = x1_ref[...] + mod_ref[0, 5:6, :] * _dot(hmid, w2_ref[...])

    for k in range(top_k):
        pltpu.make_async_copy(h2t_ref, xs_hbm.at[pl.ds(0, td)], sem).wait()


def _dispatch_rows(pad_start, pad_count, nact, pos3, h2t, h2p, x1, mods, sw1, sw3, sw2, *, n_slots, rows,
                   t_ctx, dec_seq):
    nt, K, td = pos3.shape
    T, c, _ = h2t.shape
    D = x1.shape[1]
    F = sw1.shape[1]
    nctx = t_ctx // td
    per_seq = dec_seq // td
    const = lambda i, *_: (0, 0)
    single = pl.Buffered(1)
    grid_spec = pltpu.PrefetchScalarGridSpec(
        num_scalar_prefetch=3,
        grid=(nt,),
        in_specs=[pl.BlockSpec((1, K, td), lambda i, *_: (i, 0, 0), memory_space=pltpu.SMEM),
                  pl.BlockSpec((td, c, LANES_V7X), lambda i, *_: (i, 0, 0)),
                  pl.BlockSpec((td, D // 2), lambda i, *_: (i, 0)),
                  pl.BlockSpec((td, D), lambda i, *_: (i, 0)),
                  pl.BlockSpec((1, N_MOD, D),
                               lambda i, *_: (jnp.where(i < nctx, 0, 1 + (i - nctx) // per_seq), 0, 0)),
                  pl.BlockSpec((D, F), const, pipeline_mode=single),
                  pl.BlockSpec((D, F), const, pipeline_mode=single),
                  pl.BlockSpec((F, D), const, pipeline_mode=single)],
        out_specs=(pl.BlockSpec(memory_space=pl.ANY),
                   pl.BlockSpec((td, D), lambda i, *_: (i, 0))),
        scratch_shapes=[pltpu.VMEM((rows, c, LANES_V7X), jnp.uint32), pltpu.SemaphoreType.DMA(()),
                        pltpu.SemaphoreType.DMA(())],
    )
    return pl.pallas_call(
        functools.partial(_dispatch_kernel, top_k=K),
        out_shape=(jax.ShapeDtypeStruct((n_slots, c, LANES_V7X), jnp.uint32),
                   jax.ShapeDtypeStruct((T, D), F32)),
        grid_spec=grid_spec,
        compiler_params=_params(("arbitrary",)),
        name="dispatch_shared",
    )(pad_start, pad_count, nact, pos3, h2t, h2p, x1, mods, sw1, sw3, sw2)


def _moe_kernel(blk_e, nact_ref, w1_ref, w3_ref, w2_ref, xs_hbm, y_ref, xbuf0, xbuf1, sem, *, rows):
    i = pl.program_id(0)
    nact = nact_ref[0]
    c = xs_hbm.shape[1]

    def fetch(b, buf, s):
        for a in range(c):
            pltpu.make_async_copy(xs_hbm.at[pl.ds(b * rows, rows), a, :],
                                  buf.at[:, pl.ds(a * LANES_V7X, LANES_V7X)], s).start()

    def fetch_wait(buf, s):
        for a in range(c):
            pltpu.make_async_copy(xs_hbm.at[pl.ds(0, rows), a, :],
                                  buf.at[:, pl.ds(a * LANES_V7X, LANES_V7X)], s).wait()

    @pl.when(i == 0)
    def _():
        fetch(0, xbuf0, sem.at[0])

    def block(cur, sem_cur, nxt, sem_nxt):
        fetch_wait(cur, sem_cur)

        @pl.when(i + 1 < nact)
        def _():
            fetch(i + 1, nxt, sem_nxt)

        x = _unpack_bf16_pair(cur[...])
        hmid = (jax.nn.silu(_dot(x, w1_ref[0])) * _dot(x, w3_ref[0])).astype(BF16)
        y_ref[...] = _pack_bf16_pair(_dot(hmid, w2_ref[0]))

    @pl.when(jnp.logical_and(i < nact, i % 2 == 0))
    def _():
        block(xbuf0, sem.at[0], xbuf1, sem.at[1])

    @pl.when(jnp.logical_and(i < nact, i % 2 == 1))
    def _():
        block(xbuf1, sem.at[1], xbuf0, sem.at[0])

    @pl.when(i >= nact)
    def _():
        y_ref[...] = jnp.zeros_like(y_ref)


def _moe(blk_e, nact, xs, w1, w3, w2, *, rows):
    nblk = blk_e.shape[0]
    E, D, F = w1.shape
    grid_spec = pltpu.PrefetchScalarGridSpec(
        num_scalar_prefetch=2,
        grid=(nblk,),
        in_specs=[pl.BlockSpec((1, D, F), lambda i, be, na: (be[i], 0, 0)),
                  pl.BlockSpec((1, D, F), lambda i, be, na: (be[i], 0, 0)),
                  pl.BlockSpec((1, F, D), lambda i, be, na: (be[i], 0, 0)),
                  pl.BlockSpec(memory_space=pl.ANY)],
        out_specs=pl.BlockSpec((rows, D // 2), lambda i, be, na: (i, 0)),
        scratch_shapes=[pltpu.VMEM((rows, D // 2), jnp.uint32), pltpu.VMEM((rows, D // 2), jnp.uint32),
                        pltpu.SemaphoreType.DMA((2,))],
    )
    return pl.pallas_call(
        functools.partial(_moe_kernel, rows=rows),
        out_shape=jax.ShapeDtypeStruct((nblk * rows, D // 2), jnp.uint32),
        grid_spec=grid_spec,
        compiler_params=_params(("arbitrary",)),
        name="routed_experts",
    )(blk_e, nact, w1, w3, w2, xs)


def _router_kernel(lg_ref, rb_ref, eidx_ref, rank_ref, wtok_ref, cnt_ref, carry, *, top_k, n_groups, topk_groups):
    i = pl.program_id(0)

    @pl.when(i == 0)
    def _():
        carry[...] = jnp.zeros_like(carry)

    E, tr = lg_ref.shape
    gs = E // n_groups
    scores = jax.nn.sigmoid(lg_ref[...])
    biased = scores + rb_ref[...]
    b3 = biased.reshape(n_groups, gs, tr)
    io3 = lax.broadcasted_iota(jnp.int32, b3.shape, 1)
    m1 = jnp.max(b3, axis=1, keepdims=True)
    i1 = jnp.min(jnp.where(b3 == m1, io3, gs), axis=1, keepdims=True)
    m2 = jnp.max(jnp.where(io3 == i1, -jnp.inf, b3), axis=1, keepdims=True)
    grp = (m1 + m2).reshape(n_groups, tr)
    iog = lax.broadcasted_iota(jnp.int32, grp.shape, 0)
    sel = jnp.zeros(grp.shape, jnp.bool_)
    for _ in range(topk_groups):
        mx = jnp.max(grp, axis=0, keepdims=True)
        hit = iog == jnp.min(jnp.where(grp == mx, iog, n_groups), axis=0, keepdims=True)
        sel = jnp.logical_or(sel, hit)
        grp = jnp.where(hit, -jnp.inf, grp)
    masked = jnp.where(sel.reshape(n_groups, 1, tr), b3, -jnp.inf).reshape(E, tr)
    ioe = lax.broadcasted_iota(jnp.int32, (E, tr), 0)
    onehot = jnp.zeros((E, tr), F32)
    hits, idxs, ws = [], [], []
    for _ in range(top_k):
        mx = jnp.max(masked, axis=0, keepdims=True)
        ix = jnp.min(jnp.where(masked == mx, ioe, E), axis=0, keepdims=True)
        hit = ioe == ix
        hits.append(hit)
        idxs.append(ix)
        ws.append(jnp.sum(jnp.where(hit, scores, 0.0), axis=0, keepdims=True))
        onehot = onehot + hit.astype(F32)
        masked = jnp.where(hit, -jnp.inf, masked)
    wsum = ws[0]
    for w in ws[1:]:
        wsum = wsum + w
    ri = lax.broadcasted_iota(jnp.int32, (tr, tr), 0)
    ci = lax.broadcasted_iota(jnp.int32, (tr, tr), 1)
    before = _dot(onehot.astype(BF16), (ri < ci).astype(BF16)) + carry[...]
    ranks = [jnp.sum(jnp.where(hit, before, 0.0), axis=0, keepdims=True) for hit in hits]
    carry[...] = carry[...] + jnp.sum(onehot, axis=1, keepdims=True)
    cnt_ref[...] = carry[...].astype(jnp.int32)
    eidx_ref[...] = jnp.concatenate(idxs, axis=0)
    rank_ref[...] = jnp.concatenate(ranks, axis=0).astype(jnp.int32)
    wrows = jnp.concatenate([w / wsum * ROUTED_SCALE for w in ws]
                            + [jnp.zeros((LANES_V7X - top_k, tr), F32)], axis=0)
    wtok_ref[...] = wrows.T


def _router(logits_t, router_b):
    E, T = logits_t.shape
    tr = min(TR_ROUTER, T)
    return pl.pallas_call(
        functools.partial(_router_kernel, top_k=TOP_K, n_groups=N_GROUPS, topk_groups=TOPK_GROUPS),
        out_shape=(jax.ShapeDtypeStruct((TOP_K, T), jnp.int32),
                   jax.ShapeDtypeStruct((TOP_K, T), jnp.int32),
                   jax.ShapeDtypeStruct((T, LANES_V7X), F32),
                   jax.ShapeDtypeStruct((E, 1), jnp.int32)),
        grid=(T // tr,),
        in_specs=[pl.BlockSpec((E, tr), lambda i: (0, i)),
                  pl.BlockSpec((E, 1), lambda i: (0, 0))],
        out_specs=(pl.BlockSpec((TOP_K, tr), lambda i: (0, i)),
                   pl.BlockSpec((TOP_K, tr), lambda i: (0, i)),
                   pl.BlockSpec((tr, LANES_V7X), lambda i: (i, 0)),
                   pl.BlockSpec((E, 1), lambda i: (0, 0))),
        scratch_shapes=[pltpu.VMEM((E, 1), F32)],
        compiler_params=_params(("arbitrary",)),
        name="router",
    )(logits_t, router_b.reshape(E, 1))


def _slot_pos_kernel(start_ref, eidx_ref, rank_ref, pos_ref, *, n_experts):
    eidx = eidx_ref[...]
    pos = rank_ref[...]
    for e in range(n_experts):
        pos = pos + jnp.where(eidx == e, start_ref[e], 0)
    pos_ref[...] = pos


def _slot_pos(start_pad, eidx, rank):
    K, T = eidx.shape
    tl = min(TL_SLOTPOS, T)
    return pl.pallas_call(
        functools.partial(_slot_pos_kernel, n_experts=start_pad.shape[0]),
        out_shape=jax.ShapeDtypeStruct((K, T), jnp.int32),
        grid=(T // tl,),
        in_specs=[pl.BlockSpec(memory_space=pltpu.SMEM),
                  pl.BlockSpec((K, tl), lambda i: (0, i)),
                  pl.BlockSpec((K, tl), lambda i: (0, i))],
        out_specs=pl.BlockSpec((K, tl), lambda i: (0, i)),
        compiler_params=_params(("arbitrary",)),
        name="slot_pos",
    )(start_pad, eidx, rank)


def _dispatch(eidx, rank, counts, rows):
    K, T = eidx.shape
    A = K * T
    E = counts.shape[0]
    padded = (counts + rows - 1) // rows * rows
    end_pad = jnp.cumsum(padded)
    start_pad = end_pad - padded
    nblk = -(-(A + E * (rows - 1)) // rows)
    n_slots = nblk * rows
    nact = end_pad[-1] // rows
    blk = jnp.arange(nblk, dtype=jnp.int32)
    blk_e = jnp.sum(end_pad[None, :] <= (blk * rows)[:, None], axis=1).astype(jnp.int32)
    last_e = jnp.sum(end_pad <= (nact - 1) * rows).astype(jnp.int32)
    blk_e = jnp.where(blk < nact, blk_e, last_e)
    pos = _slot_pos(start_pad.astype(jnp.int32), eidx, rank)
    pad_start = (start_pad + counts).astype(jnp.int32)
    pad_count = (padded - counts).astype(jnp.int32)
    return pos, blk_e, nact.astype(jnp.int32).reshape(1), pad_start, pad_count, n_slots


def _tile_major(pos, tile):
    K, T = pos.shape
    return pos.reshape(K, T // tile, tile).transpose(1, 0, 2)


def _final_kernel(pos_ref, posn_ref, x1s_ref, wt_ref, mod_ref, fn_ref, y_hbm,
                  oc_ref, ol_ref, ybuf0, ybuf1, sem, *, top_k, nctx_tiles):
    i = pl.program_id(0)
    nt = pl.num_programs(0)
    tm = x1s_ref.shape[0]

    def gather(idx_ref, buf, s):
        for k in range(top_k):
            for t in range(tm):
                pltpu.make_async_copy(y_hbm.at[pl.ds(idx_ref[0, k, t], 1)], buf.at[k, pl.ds(t, 1)], s).start()

    def gather_wait(buf, s):
        for k in range(top_k):
            pltpu.make_async_copy(y_hbm.at[pl.ds(0, tm)], buf.at[k], s).wait()

    @pl.when(i == 0)
    def _():
        gather(pos_ref, ybuf0, sem.at[0])

    def tile(cur, sem_cur, nxt, sem_nxt):
        @pl.when(i + 1 < nt)
        def _():
            gather(posn_ref, nxt, sem_nxt)

        gather_wait(cur, sem_cur)
        wt = wt_ref[...]
        lo = hi = None
        for k in range(top_k):
            w = cur[k]
            wk = wt[:, k:k + 1]
            lo_k = pltpu.bitcast(w << 16, F32) * wk
            hi_k = pltpu.bitcast(w & jnp.uint32(0xFFFF0000), F32) * wk
            lo = lo_k if lo is None else lo + lo_k
            hi = hi_k if hi is None else hi + hi_k
        x2 = x1s_ref[...] + mod_ref[0, 5:6, :] * jnp.concatenate([lo, hi], axis=1)
        out = x2 * lax.rsqrt(jnp.mean(x2 * x2, axis=-1, keepdims=True) + EPS) * fn_ref[...]

        @pl.when(i < nctx_tiles)
        def _():
            oc_ref[...] = out

        @pl.when(i >= nctx_tiles)
        def _():
            ol_ref[...] = out

    @pl.when(i % 2 == 0)
    def _():
        tile(ybuf0, sem.at[0], ybuf1, sem.at[1])

    @pl.when(i % 2 == 1)
    def _():
        tile(ybuf1, sem.at[1], ybuf0, sem.at[0])


def _final(pos3, x1s, wtok, ys, mods, fnorm, *, t_ctx, dec_seq):
    nt, K, tm = pos3.shape
    t_all, D = x1s.shape
    nctx = t_ctx // tm
    per_seq = dec_seq // tm

    def mod_row(i):
        return (jnp.where(i < nctx, 0, 1 + (i - nctx) // per_seq), 0, 0)

    const = lambda i: (0, 0)
    smem_blk = lambda f: pl.BlockSpec((1, K, tm), f, memory_space=pltpu.SMEM)
    return pl.pallas_call(
        functools.partial(_final_kernel, top_k=K, nctx_tiles=nctx),
        out_shape=(jax.ShapeDtypeStruct((t_ctx, D), F32), jax.ShapeDtypeStruct((t_all - t_ctx, D), F32)),
        grid=(nt,),
        in_specs=[smem_blk(lambda i: (i, 0, 0)),
                  smem_blk(lambda i: (jnp.minimum(i + 1, nt - 1), 0, 0)),
                  pl.BlockSpec((tm, D), lambda i: (i, 0)),
                  pl.BlockSpec((tm, LANES_V7X), lambda i: (i, 0)),
                  pl.BlockSpec((1, N_MOD, D), mod_row),
                  pl.BlockSpec((1, D), const),
                  pl.BlockSpec(memory_space=pl.ANY)],
        out_specs=(pl.BlockSpec((tm, D), lambda i: (jnp.minimum(i, nctx - 1), 0)),
                   pl.BlockSpec((tm, D), lambda i: (jnp.maximum(i - nctx, 0), 0))),
        scratch_shapes=[pltpu.VMEM((K, tm, D // 2), jnp.uint32), pltpu.VMEM((K, tm, D // 2), jnp.uint32),
                        pltpu.SemaphoreType.DMA((2,))],
        compiler_params=_params(("arbitrary",)),
        name="combine_final",
    )(pos3, pos3, x1s, wtok, mods, fnorm, ys)


def _rope_tables(dec_seq, hd):
    nf = hd // 4
    t = jnp.arange(dec_seq)
    inv = ROPE_BASE ** (-jnp.arange(nf, dtype=F32) / nf)
    ang_r = (t // GRID_W).astype(F32)[:, None] * inv
    ang_c = (t % GRID_W).astype(F32)[:, None] * inv
    cos = jnp.concatenate([jnp.cos(ang_r)] * 2 + [jnp.cos(ang_c)] * 2, axis=1)
    sin = jnp.concatenate([-jnp.sin(ang_r), jnp.sin(ang_r), -jnp.sin(ang_c), jnp.sin(ang_c)], axis=1)
    return cos, sin


def _scan_steps(batch, seq, dec_batch, dec_seq, L):
    fb, bb, sq, fi, la = [], [], [], [], []
    base = 0
    for sid, S in enumerate([seq] * batch + [dec_seq] * dec_batch):
        nc = S // L
        for c in range(nc):
            fb.append(base + c)
            bb.append(base + nc - 1 - c)
            sq.append(sid)
            fi.append(int(c == 0))
            la.append(int(c == nc - 1))
        base += nc
    return tuple(jnp.asarray(np.asarray(a, dtype=np.int32)) for a in (fb, bb, sq, fi, la))


def kernel(x_prompt, x_sample, cache_k, cache_v, state_mlstm_C, state_mlstm_n, state_mlstm_m, c, c_ctx,
           w_mod, b_mod, norm1_w, norm2_w, w_in, igate_b, fgate_b, mlstm_norm_w, attn_sink,
           w_branch_m, w_branch_a, w_out, router_w, router_b, expert_w1, expert_w3, expert_w2,
           shared_w1, shared_w3, shared_w2, final_norm_w):
    batch, seq, D = x_prompt.shape
    dec_batch, dec_seq, _ = x_sample.shape
    depth = w_in.shape[0]
    assert depth == 1, "single trunk layer"
    _, _, past, kvh, hd = cache_k.shape
    mh, dk, dv = state_mlstm_C.shape[3:]
    ah = attn_sink.shape[1]
    groups = ah // kvh
    E = router_w.shape[2]
    assert dk == dv == hd == LANES_V7X
    t_ctx, t_lat = batch * seq, dec_batch * dec_seq
    mw, qw, kw = mh * dk, ah * hd, kvh * hd
    ng = 4 * mh

    wi = w_in[0]
    o = 0
    seg = {}
    for name, width in (("qm", mw), ("km", mw), ("vm", mw), ("om", mw), ("im", 2 * mh), ("fm", 2 * mh),
                        ("qa", qw), ("ka", kw), ("va", kw), ("gm", D), ("ga", D)):
        seg[name] = wi[:, o:o + width]
        o += width
    order = ("qa", "gm", "ga", "qm", "km", "vm", "om", "ka", "va")
    w_main = jnp.concatenate([seg[nm] for nm in order], axis=1).astype(BF16)
    col = {}
    o = 0
    for nm in order:
        col[nm] = o
        o += seg[nm].shape[1]
    tn = 2 * kw
    for nm in order[:-2]:
        assert col[nm] % tn == 0 and seg[nm].shape[1] % tn == 0
    for nm, width in (("gm", D), ("ga", D), ("qm", mw), ("km", mw), ("vm", mw), ("om", mw), ("ka", kw), ("va", kw)):
        assert col[nm] % width == 0
    w_gate = jnp.pad(jnp.concatenate([seg["im"], seg["fm"]], axis=1), ((0, 0), (0, LANES_V7X - ng))).astype(BF16)

    R = -(-(1 + dec_batch) // 8) * 8
    cond = jnp.concatenate([c_ctx[None, :], c, jnp.zeros((R - 1 - dec_batch, D), F32)], axis=0)
    mods = _modulation(cond, w_mod[0], b_mod[0]).reshape(R, N_MOD, D)

    xp2 = x_prompt.reshape(t_ctx, D)
    xs2 = x_sample.reshape(t_lat, D)
    cos, sin = _rope_tables(dec_seq, hd)
    h1, gates, gates_t = _prenorm(xp2, xs2, mods, norm1_w, w_gate, t_ctx=t_ctx, dec_seq=dec_seq, n_gates=ng)
    z, kv32 = _inproj(h1, w_main, cos, sin, t_ctx=t_ctx, dec_seq=dec_seq, tn=tn, n_rope_tiles=qw // tn)

    nseq = batch + dec_batch
    C0 = jnp.concatenate([jnp.zeros((batch, 2, mh, dk, dv), F32), state_mlstm_C[:, 0]], axis=0)
    n0 = jnp.concatenate([jnp.zeros((batch, 2, mh, dk), F32), state_mlstm_n[:, 0]], axis=0)
    m0 = jnp.concatenate([jnp.zeros((batch, 2, mh), F32), state_mlstm_m[:, 0]], axis=0)
    s0 = jnp.concatenate([C0, jnp.broadcast_to(n0[..., None], (nseq, 2, mh, dk, dv))], axis=-1)
    s0 = s0.reshape(nseq, 2 * mh, dk, 2 * dv)
    m0 = jnp.broadcast_to(m0.reshape(nseq, 2 * mh, 1), (nseq, 2 * mh, LANES_V7X))
    gate_b = jnp.concatenate([igate_b[0].reshape(-1), fgate_b[0].reshape(-1)])
    steps = _scan_steps(batch, seq, dec_batch, dec_seq, M_CHUNK)
    hf, hb, s_fin, m_fin = _mlstm(z, gates, gates_t, gate_b.reshape(1, ng), gate_b.reshape(ng, 1), s0, m0, steps,
                                  mh=mh, dk=dk, qcol=col["qm"] // mw, kcol=col["km"] // mw, vcol=col["vm"] // mw)

    sink = attn_sink[0]
    ha_ctx = _ctx_attention(sink, z, batch=batch, seq=seq, kvh=kvh, groups=groups, hd=hd,
                            kcol=col["ka"] // kw, vcol=col["va"] // kw)
    ha_lat = _lat_attention(sink, z, cache_k.reshape(dec_batch, depth, past, kw),
                            cache_v.reshape(dec_batch, depth, past, kw), t_ctx=t_ctx, dec_batch=dec_batch,
                            dec_seq=dec_seq, kvh=kvh, groups=groups, hd=hd, kcol=col["ka"] // kw,
                            vcol=col["va"] // kw)

    x1, h2p, h2t, logits_t = _outproj(hf, hb, z, ha_ctx, ha_lat, xp2, xs2, mods, mlstm_norm_w, norm2_w,
                                w_branch_m[0].astype(BF16), w_branch_a[0].astype(BF16), w_out[0].astype(BF16),
                                router_w[0].T, t_ctx=t_ctx, dec_seq=dec_seq, mh=mh, dv=dv,
                                omcol=col["om"] // mw, gmcol=col["gm"] // D, gacol=col["ga"] // D)

    eidx, rank, wtok, counts = _router(logits_t, router_b[0])
    pos, blk_e, nact, pad_start, pad_count, n_slots = _dispatch(eidx, rank, counts[:, 0], EXPERT_ROWS)
    xs, x1s = _dispatch_rows(pad_start, pad_count, nact, _tile_major(pos, min(TD_DISPATCH, t_ctx, dec_seq)),
                             h2t, h2p, x1, mods, shared_w1[0].astype(BF16), shared_w3[0].astype(BF16),
                             shared_w2[0].astype(BF16), n_slots=n_slots, rows=EXPERT_ROWS, t_ctx=t_ctx,
                             dec_seq=dec_seq)
    ys = _moe(blk_e, nact, xs, expert_w1[0].astype(BF16), expert_w3[0].astype(BF16),
              expert_w2[0].astype(BF16), rows=EXPERT_ROWS)

    y_ctx, y_lat = _final(_tile_major(pos, min(TM_FINAL, t_ctx, dec_seq)), x1s, wtok, ys, mods,
                          final_norm_w.reshape(1, D), t_ctx=t_ctx, dec_seq=dec_seq)

    y_prompt = y_ctx.reshape(batch, seq, D)
    y_sample = y_lat.reshape(dec_batch, dec_seq, D)
    new_k = kv32[:t_ctx, :kw].reshape(batch, 1, seq, kvh, hd)
    new_v = kv32[:t_ctx, kw:].reshape(batch, 1, seq, kvh, hd)
    s_ctx = s_fin[:batch].reshape(batch, 1, 2, mh, dk, 2 * dv)
    new_C = s_ctx[..., :dv]
    new_n = s_ctx[..., dv]
    new_m = m_fin[:batch, :, 0].reshape(batch, 1, 2, mh)
    return y_prompt, y_sample, new_k, new_v, new_C, new_n, new_m
```

```python
import functools

import numpy as np
import jax
import jax.numpy as jnp
from jax import lax
from jax.experimental import pallas as pl
from jax.experimental.pallas import tpu as pltpu

TOP_K = 6
N_GROUPS = 8
TOPK_GROUPS = 4
ROUTED_SCALE = 2.5
WINDOW = 128
Q_BLOCK = 128
GRID_W = 64
ROPE_BASE = 10000.0
M_CHUNK = 128
N_MOD = 6
EPS = 1e-6

LANES_V7X = 128
MXU_COLS_V7X = 256
VMEM_LIMIT_V7X = 56 * 1024 * 1024

TM_PRENORM = 512
TM_INPROJ = 1024
TM_OUTPROJ = 256
OUTPROJ_SPLIT = 2
WEIGHT_CHUNKS = 8
TM_FINAL = 256
TN_MOD = 1024
EXPERT_ROWS = 256
TR_ROUTER = 512
TL_SLOTPOS = 2048
TD_DISPATCH = 256

F32 = jnp.float32
BF16 = jnp.bfloat16
_NT = (((1,), (1,)), ((), ()))


def _params(sem):
    return pltpu.CompilerParams(dimension_semantics=sem, vmem_limit_bytes=VMEM_LIMIT_V7X)


def _dot(a, b):
    return jnp.dot(a, b, preferred_element_type=F32)


def _dot_nt(a, b):
    return lax.dot_general(a, b, _NT, preferred_element_type=F32)


def _pack_bf16_pair(x):
    c = x.shape[1] // 2
    lo = pltpu.bitcast(x[:, :c].astype(BF16).astype(F32), jnp.uint32)
    hi = pltpu.bitcast(x[:, c:].astype(BF16).astype(F32), jnp.uint32)
    return (lo >> 16) | (hi & jnp.uint32(0xFFFF0000))


def _rows_to_tiles(ref, x):
    for a in range(ref.shape[1]):
        ref[:, a, :] = x[:, a * LANES_V7X:(a + 1) * LANES_V7X]


def _unpack_bf16_pair(w):
    lo = pltpu.bitcast(w << 16, F32).astype(BF16)
    hi = pltpu.bitcast(w & jnp.uint32(0xFFFF0000), F32).astype(BF16)
    return jnp.concatenate([lo, hi], axis=1)


def _mod_kernel(c_ref, w_ref, b_ref, o_ref):
    s = jax.nn.silu(c_ref[...]).astype(BF16)
    o_ref[...] = _dot(s, w_ref[...].astype(BF16)) + b_ref[...]


def _modulation(cond, w_mod, b_mod):
    R, D = cond.shape
    N = w_mod.shape[1]
    tn = min(TN_MOD, N)
    return pl.pallas_call(
        _mod_kernel,
        out_shape=jax.ShapeDtypeStruct((R, N), F32),
        grid=(N // tn,),
        in_specs=[pl.BlockSpec((R, D), lambda n: (0, 0)),
                  pl.BlockSpec((D, tn), lambda n: (0, n)),
                  pl.BlockSpec((1, tn), lambda n: (0, n))],
        out_specs=pl.BlockSpec((R, tn), lambda n: (0, n)),
        compiler_params=_params(("arbitrary",)),
        name="modulation",
    )(cond, w_mod, b_mod.reshape(1, N))


def _rope_slice(x, cos, sin_signed, first_half):
    swap = jnp.where(first_half, pltpu.roll(x, 96, 1), pltpu.roll(x, 32, 1))
    return x * cos + swap * sin_signed


def _prenorm_kernel(xp_ref, xs_ref, mod_ref, n1_ref, wg_ref, h_ref, g_ref, gt_ref, *, nctx_tiles, n_gates):
    i = pl.program_id(0)
    x = jnp.where(i < nctx_tiles, xp_ref[...], xs_ref[...])
    y = x * lax.rsqrt(jnp.mean(x * x, axis=-1, keepdims=True) + EPS) * n1_ref[...]
    h = (y * (1.0 + mod_ref[0, 1:2, :]) + mod_ref[0, 0:1, :]).astype(BF16)
    h_ref[...] = h
    g = _dot(h, wg_ref[...])
    g_ref[...] = g[:, :n_gates]
    gt_ref[...] = g.T[:n_gates, :]


def _prenorm(x_prompt2, x_sample2, mods, n1, w_gate, *, t_ctx, dec_seq, n_gates):
    t_lat, D = x_sample2.shape
    t_all = t_ctx + t_lat
    tm = min(TM_PRENORM, t_ctx, dec_seq)
    nctx = t_ctx // tm
    per_seq = dec_seq // tm
    return pl.pallas_call(
        functools.partial(_prenorm_kernel, nctx_tiles=nctx, n_gates=n_gates),
        out_shape=(jax.ShapeDtypeStruct((t_all, D), BF16),
                   jax.ShapeDtypeStruct((t_all, n_gates), F32),
                   jax.ShapeDtypeStruct((n_gates, t_all), F32)),
        grid=(t_all // tm,),
        in_specs=[pl.BlockSpec((tm, D), lambda i: (jnp.minimum(i, nctx - 1), 0)),
                  pl.BlockSpec((tm, D), lambda i: (jnp.maximum(i - nctx, 0), 0)),
                  pl.BlockSpec((1, N_MOD, D), lambda i: (jnp.where(i < nctx, 0, 1 + (i - nctx) // per_seq), 0, 0)),
                  pl.BlockSpec((1, D), lambda i: (0, 0)),
                  pl.BlockSpec((D, LANES_V7X), lambda i: (0, 0))],
        out_specs=(pl.BlockSpec((tm, D), lambda i: (i, 0)),
                   pl.BlockSpec((tm, n_gates), lambda i: (i, 0)),
                   pl.BlockSpec((n_gates, tm), lambda i: (0, i))),
        compiler_params=_params(("arbitrary",)),
        name="prenorm",
    )(x_prompt2, x_sample2, mods, n1, w_gate)


def _inproj_kernel(h_ref, w_ref, cos_ref, sin_ref, z_ref, kv_ref, *, nctx_tiles, n_rope_tiles, kv_tile):
    i = pl.program_id(0)
    n = pl.program_id(1)
    is_ctx = i < nctx_tiles
    tn = z_ref.shape[1]
    acc = _dot(h_ref[...], w_ref[...])

    def rope_cols(ncols):
        cos = jnp.where(is_ctx, 1.0, cos_ref[...])
        sin = jnp.where(is_ctx, 0.0, sin_ref[...])
        lane = lax.broadcasted_iota(jnp.int32, cos.shape, 1)
        first_half = (lane % 64) < 32
        return [_rope_slice(acc[:, c:c + LANES_V7X], cos, sin, first_half)
                for c in range(0, ncols, LANES_V7X)]

    @pl.when(n < n_rope_tiles)
    def _():
        z_ref[...] = jnp.concatenate(rope_cols(tn), axis=1).astype(BF16)

    @pl.when(n == kv_tile)
    def _():
        r = jnp.concatenate(rope_cols(tn // 2) + [acc[:, tn // 2:]], axis=1)
        z_ref[...] = r.astype(BF16)
        kv_ref[...] = r

    @pl.when(jnp.logical_and(n >= n_rope_tiles, n != kv_tile))
    def _():
        z_ref[...] = acc.astype(BF16)


def _inproj(h, w_main, cos, sin, *, t_ctx, dec_seq, tn, n_rope_tiles):
    t_all, D = h.shape
    tm = min(TM_INPROJ, t_ctx, dec_seq)
    nctx = t_ctx // tm
    per_seq = dec_seq // tm
    ncols = w_main.shape[1]
    ntile = ncols // tn
    kv_tile = ntile - 1

    def pos_blk(i):
        return jnp.where(i < nctx, 0, (i - nctx) % per_seq)

    kernel = functools.partial(_inproj_kernel, nctx_tiles=nctx, n_rope_tiles=n_rope_tiles, kv_tile=kv_tile)
    return pl.pallas_call(
        kernel,
        out_shape=(jax.ShapeDtypeStruct((t_all, ncols), BF16),
                   jax.ShapeDtypeStruct((t_all, tn), F32)),
        grid=(t_all // tm, ntile),
        in_specs=[pl.BlockSpec((tm, D), lambda i, n: (i, 0)),
                  pl.BlockSpec((D, tn), lambda i, n: (0, n)),
                  pl.BlockSpec((tm, LANES_V7X), lambda i, n: (pos_blk(i), 0)),
                  pl.BlockSpec((tm, LANES_V7X), lambda i, n: (pos_blk(i), 0))],
        out_specs=(pl.BlockSpec((tm, tn), lambda i, n: (i, n)),
                   pl.BlockSpec((tm, tn), lambda i, n: (i, 0))),
        compiler_params=_params(("arbitrary", "arbitrary")),
        name="inproj",
    )(h, w_main, cos, sin)


def _mlstm_kernel(fb, bb, sq, fi, la,
                  qf, kf, vf, qb, kb, vb, gf, gb, gtf, gtb, brow, bcol, s0, m0,
                  hf, hb, s_out, m_out, s_scr, m_scr, *, mh, dk, chunk):
    s = pl.program_id(0)
    L = chunk
    assert L == dk
    scale = dk ** -0.5

    @pl.when(fi[s] == 1)
    def _():
        s_scr[...] = s0[0]
        m_scr[...] = m0[0]

    ri = lax.broadcasted_iota(jnp.int32, (L, L), 0)
    ci = lax.broadcasted_iota(jnp.int32, (L, L), 1)
    low = ri >= ci
    upp = ri <= ci
    low_f = low.astype(F32)
    upp_f = upp.astype(F32)
    ones_blk = jnp.ones((L, dk), BF16)
    hi = lax.Precision.HIGHEST
    refs = ((qf, kf, vf, gf, gtf, hf), (qb, kb, vb, gb, gtb, hb))
    units = [(dr, h) for dr in range(2) for h in range(mh)]
    sl = lambda h: slice(h * dk, (h + 1) * dk)

    gate = []
    for dr, (_, _, _, g_ref, gt_ref, _) in enumerate(refs):
        G = g_ref[...] + brow[...]
        GT = gt_ref[...] + bcol[...]
        ic_col = G[:, dr * mh:(dr + 1) * mh]
        lf_col = jax.nn.log_sigmoid(G[:, (2 + dr) * mh:(3 + dr) * mh])
        ic_row = GT[dr * mh:(dr + 1) * mh, :]
        lf_row = jax.nn.log_sigmoid(GT[(2 + dr) * mh:(3 + dr) * mh, :])
        b_col = jnp.dot(low_f if dr == 0 else upp_f, lf_col, precision=hi, preferred_element_type=F32)
        b_row = jnp.dot(lf_row, upp_f if dr == 0 else low_f, precision=hi, preferred_element_type=F32)
        gate.append((ic_col, ic_row, b_col, b_row))

    S_prev = [s_scr[dr * mh + h] for dr, h in units]
    m_prev = [m_scr[dr * mh + h:dr * mh + h + 1, 0:1] for dr, h in units]
    q = [refs[dr][0][:, sl(h)] for dr, h in units]
    k = [refs[dr][1][:, sl(h)] for dr, h in units]
    v_aug = [jnp.concatenate([refs[dr][2][:, sl(h)], ones_blk], axis=1) for dr, h in units]
    qk = [_dot_nt(q[u], k[u]) for u in range(len(units))]
    qs = [_dot(q[u], S_prev[u].astype(BF16)) for u in range(len(units))]

    sm, w_inter, floor, b_rep = [], [], [], []
    for u, (dr, h) in enumerate(units):
        ic_col, ic_row, b_col, b_row = gate[dr]
        bc = jnp.broadcast_to(b_col[:, h:h + 1], (L, L))
        d = jnp.where(low if dr == 0 else upp, bc - b_row[h:h + 1, :] + ic_row[h:h + 1, :], -jnp.inf)
        inter = bc + m_prev[u]
        m_t = jnp.maximum(inter, jnp.broadcast_to(jnp.max(d, axis=-1, keepdims=True), (L, L)))
        sm.append((qk[u] * scale * jnp.exp(d - m_t)).astype(BF16))
        w_inter.append(jnp.exp(inter - m_t))
        floor.append(jnp.exp(-m_t))
        b_rep.append(bc)

    sv = [_dot(sm[u], v_aug[u]) for u in range(len(units))]
    for u, (dr, h) in enumerate(units):
        num = sv[u][:, :dk] + w_inter[u] * qs[u][:, :dk]
        den = sv[u][:, dk:] + w_inter[u] * qs[u][:, dk:]
        refs[dr][5][:, sl(h)] = (num / jnp.maximum(jnp.abs(den), floor[u])).astype(BF16)

    kw_t, wc, m_new = [], [], []
    for u, (dr, h) in enumerate(units):
        ic_col = gate[dr][0]
        bc = b_rep[u]
        b_last = bc[L - 1:L, :] if dr == 0 else bc[0:1, :]
        g = b_last - bc + jnp.broadcast_to(ic_col[:, h:h + 1], (L, L))
        mn = jnp.maximum(b_last + m_prev[u], jnp.max(g, axis=0, keepdims=True))
        kw_t.append((k[u].astype(F32) * (jnp.exp(g - mn) * scale)).T.astype(BF16))
        wc.append(jnp.exp(b_last + m_prev[u] - mn))
        m_new.append(mn)

    upd = [_dot(kw_t[u], v_aug[u]) for u in range(len(units))]
    for u, (dr, h) in enumerate(units):
        r = dr * mh + h
        s_scr[r] = jnp.concatenate([wc[u], wc[u]], axis=1) * S_prev[u] + upd[u]
        m_scr[r:r + 1, :] = m_new[u]

    @pl.when(la[s] == 1)
    def _():
        s_out[0] = s_scr[...]
        m_out[0] = m_scr[...]


def _mlstm(z, gates, gates_t, brow, bcol, s0, m0, steps, *, mh, dk, qcol, kcol, vcol):
    t_all = z.shape[0]
    L = M_CHUNK
    mw = mh * dk
    ng = gates.shape[1]
    nseq = s0.shape[0]
    fb, bb, sq, fi, la = steps
    nsteps = fb.shape[0]

    def zspec(which, col):
        return pl.BlockSpec((L, mw), lambda s, fb, bb, sq, fi, la: ((fb, bb)[which][s], col))

    def gspec(which):
        return pl.BlockSpec((L, ng), lambda s, fb, bb, sq, fi, la: ((fb, bb)[which][s], 0))

    def gtspec(which):
        return pl.BlockSpec((ng, L), lambda s, fb, bb, sq, fi, la: (0, (fb, bb)[which][s]))

    grid_spec = pltpu.PrefetchScalarGridSpec(
        num_scalar_prefetch=5,
        grid=(nsteps,),
        in_specs=[zspec(0, qcol), zspec(0, kcol), zspec(0, vcol),
                  zspec(1, qcol), zspec(1, kcol), zspec(1, vcol),
                  gspec(0), gspec(1), gtspec(0), gtspec(1),
                  pl.BlockSpec((1, ng), lambda s, *_: (0, 0)),
                  pl.BlockSpec((ng, 1), lambda s, *_: (0, 0)),
                  pl.BlockSpec((1, 2 * mh, dk, 2 * dk), lambda s, fb, bb, sq, fi, la: (sq[s], 0, 0, 0)),
                  pl.BlockSpec((1, 2 * mh, LANES_V7X), lambda s, fb, bb, sq, fi, la: (sq[s], 0, 0))],
        out_specs=(pl.BlockSpec((L, mw), lambda s, fb, bb, sq, fi, la: (fb[s], 0)),
                   pl.BlockSpec((L, mw), lambda s, fb, bb, sq, fi, la: (bb[s], 0)),
                   pl.BlockSpec((1, 2 * mh, dk, 2 * dk), lambda s, fb, bb, sq, fi, la: (sq[s], 0, 0, 0)),
                   pl.BlockSpec((1, 2 * mh, LANES_V7X), lambda s, fb, bb, sq, fi, la: (sq[s], 0, 0))),
        scratch_shapes=[pltpu.VMEM((2 * mh, dk, 2 * dk), F32), pltpu.VMEM((2 * mh, LANES_V7X), F32)],
    )
    return pl.pallas_call(
        functools.partial(_mlstm_kernel, mh=mh, dk=dk, chunk=L),
        out_shape=(jax.ShapeDtypeStruct((t_all, mw), BF16),
                   jax.ShapeDtypeStruct((t_all, mw), BF16),
                   jax.ShapeDtypeStruct((nseq, 2 * mh, dk, 2 * dk), F32),
                   jax.ShapeDtypeStruct((nseq, 2 * mh, LANES_V7X), F32)),
        grid_spec=grid_spec,
        compiler_params=_params(("arbitrary",)),
        name="mlstm",
    )(fb, bb, sq, fi, la, z, z, z, z, z, z, gates, gates, gates_t, gates_t, brow, bcol, s0, m0)


def _sink_column(sink_ref, kv, groups, rows_per_group):
    shape = (groups * rows_per_group, LANES_V7X)
    row_g = lax.broadcasted_iota(jnp.int32, shape, 0) // rows_per_group
    col = jnp.full(shape, sink_ref[kv * groups], F32)
    for g in range(1, groups):
        col = jnp.where(row_g == g, sink_ref[kv * groups + g], col)
    return col


def _softmax_probs(scores, sink):
    hd = sink.shape[1]
    s = jnp.concatenate(scores, axis=1)
    mx = jnp.maximum(jnp.broadcast_to(jnp.max(s, axis=-1, keepdims=True), sink.shape), sink)
    return jnp.exp(s - jnp.concatenate([mx] * (s.shape[1] // hd), axis=1)).astype(BF16), mx


def _weighted_values(p, mx, values, sink):
    hd = sink.shape[1]
    acc = None
    off = 0
    for v in values:
        v_aug = jnp.concatenate([v, jnp.ones(v.shape, BF16)], axis=1)
        pv = _dot(p[:, off:off + v.shape[0]], v_aug)
        acc = pv if acc is None else acc + pv
        off += v.shape[0]
    return acc[:, :hd] / (acc[:, hd:] + jnp.exp(sink - mx))


def _ctx_attn_kernel(sink_ref, q_ref, k_ref, v_ref, o_ref, *, kvh, groups, hd):
    S = q_ref.shape[0]
    scale = hd ** -0.5
    for kv in range(kvh):
        k = k_ref[:, kv * hd:(kv + 1) * hd]
        v = v_ref[:, kv * hd:(kv + 1) * hd]
        q = jnp.concatenate([q_ref[:, (kv * groups + g) * hd:(kv * groups + g + 1) * hd]
                             for g in range(groups)], axis=0)
        sink = _sink_column(sink_ref, kv, groups, S)
        p, mx = _softmax_probs([_dot_nt(q, k) * scale], sink)
        o = _weighted_values(p, mx, [v], sink)
        for g in range(groups):
            o_ref[:, (kv * groups + g) * hd:(kv * groups + g + 1) * hd] = o[g * S:(g + 1) * S].astype(BF16)


def _ctx_attention(sink, z, *, batch, seq, kvh, groups, hd, kcol, vcol):
    qw = kvh * groups * hd
    kw = kvh * hd
    return pl.pallas_call(
        functools.partial(_ctx_attn_kernel, kvh=kvh, groups=groups, hd=hd),
        out_shape=jax.ShapeDtypeStruct((batch * seq, qw), BF16),
        grid=(batch,),
        in_specs=[pl.BlockSpec(memory_space=pltpu.SMEM),
                  pl.BlockSpec((seq, qw), lambda b: (b, 0)),
                  pl.BlockSpec((seq, kw), lambda b: (b, kcol)),
                  pl.BlockSpec((seq, kw), lambda b: (b, vcol))],
        out_specs=pl.BlockSpec((seq, qw), lambda b: (b, 0)),
        compiler_params=_params(("arbitrary",)),
        name="ctx_attention",
    )(sink, z, z, z)


def _lat_attn_kernel(sink_ref, q_ref, kp_ref, kc_ref, kn_ref, vp_ref, vc_ref, vn_ref, ck_ref, cv_ref, o_ref,
                     *, kvh, groups, hd):
    j = pl.program_id(1)
    nb = pl.num_programs(1)
    Q = q_ref.shape[0]
    scale = hd ** -0.5
    R = groups * Q
    rq = lax.broadcasted_iota(jnp.int32, (R, Q), 0) % Q
    cc = lax.broadcasted_iota(jnp.int32, (R, Q), 1)
    mask_prev = jnp.logical_and(cc >= rq, j > 0)
    mask_next = jnp.logical_and(cc <= rq, j < nb - 1)
    heads = [slice(kv * hd, (kv + 1) * hd) for kv in range(kvh)]
    sinks = [_sink_column(sink_ref, kv, groups, Q) for kv in range(kvh)]
    scores = []
    for kv, sl in enumerate(heads):
        q = jnp.concatenate([q_ref[:, (kv * groups + g) * hd:(kv * groups + g + 1) * hd]
                             for g in range(groups)], axis=0)
        scores.append([jnp.where(mask_prev, _dot_nt(q, kp_ref[:, sl]) * scale, -jnp.inf),
                       _dot_nt(q, kc_ref[:, sl]) * scale,
                       jnp.where(mask_next, _dot_nt(q, kn_ref[:, sl]) * scale, -jnp.inf),
                       _dot_nt(q, ck_ref[0, 0, :, sl].astype(BF16)) * scale])
    probs = [_softmax_probs(scores[kv], sinks[kv]) for kv in range(kvh)]
    for kv, sl in enumerate(heads):
        p, mx = probs[kv]
        o = _weighted_values(p, mx, [vp_ref[:, sl], vc_ref[:, sl], vn_ref[:, sl],
                                     cv_ref[0, 0, :, sl].astype(BF16)], sinks[kv])
        for g in range(groups):
            o_ref[:, (kv * groups + g) * hd:(kv * groups + g + 1) * hd] = o[g * Q:(g + 1) * Q].astype(BF16)


def _lat_attention(sink, z, cache_k, cache_v, *, t_ctx, dec_batch, dec_seq, kvh, groups, hd, kcol, vcol):
    assert WINDOW == Q_BLOCK
    Q = Q_BLOCK
    nb = dec_seq // Q
    base = t_ctx // Q
    qw = kvh * groups * hd
    kw = kvh * hd
    past = cache_k.shape[2]

    def kvspec(col, shift):
        return pl.BlockSpec((Q, kw), lambda b, j: (base + b * nb + jnp.clip(j + shift, 0, nb - 1), col))

    cspec = pl.BlockSpec((1, 1, past, kw), lambda b, j: (b, 0, 0, 0))
    return pl.pallas_call(
        functools.partial(_lat_attn_kernel, kvh=kvh, groups=groups, hd=hd),
        out_shape=jax.ShapeDtypeStruct((dec_batch * dec_seq, qw), BF16),
        grid=(dec_batch, nb),
        in_specs=[pl.BlockSpec(memory_space=pltpu.SMEM),
                  pl.BlockSpec((Q, qw), lambda b, j: (base + b * nb + j, 0)),
                  kvspec(kcol, -1), kvspec(kcol, 0), kvspec(kcol, 1),
                  kvspec(vcol, -1), kvspec(vcol, 0), kvspec(vcol, 1),
                  cspec, cspec],
        out_specs=pl.BlockSpec((Q, qw), lambda b, j: (b * nb + j, 0)),
        compiler_params=_params(("arbitrary", "arbitrary")),
        name="lat_attention",
    )(sink, z, z, z, z, z, z, z, cache_k, cache_v)


def _outproj_kernel(hf_ref, hb_ref, om_ref, hac_ref, hal_ref, gm_ref, ga_ref, xp_ref, xs_ref, mod_ref,
                    mn_ref, n2_ref, wm_ref, wa_ref, wo_ref, rw_ref,
                    x1_ref, h2_ref, h2t_ref, lg_ref, *, nctx_tiles, mh, dv):
    i = pl.program_id(0)
    is_ctx = i < nctx_tiles
    tm = x1_ref.shape[0]
    for r0 in range(0, tm, tm // OUTPROJ_SPLIT):
        rs = pl.ds(r0, tm // OUTPROJ_SPLIT)
        hm = hf_ref[rs, :].astype(F32) + hb_ref[rs, :].astype(F32)
        parts = []
        for h in range(mh):
            sl = hm[:, h * dv:(h + 1) * dv]
            parts.append(sl * lax.rsqrt(jnp.mean(sl * sl, axis=-1, keepdims=True) + EPS))
        hmn = jnp.concatenate(parts, axis=1) * mn_ref[...] * jax.nn.sigmoid(om_ref[rs, :].astype(F32))
        ha = jnp.where(is_ctx, hac_ref[rs, :], hal_ref[rs, :])
        y = (jax.nn.sigmoid(gm_ref[rs, :].astype(F32)) * _dot(hmn.astype(BF16), wm_ref[...])
             + jax.nn.sigmoid(ga_ref[rs, :].astype(F32)) * _dot(ha, wa_ref[...]))
        x = jnp.where(is_ctx, xp_ref[rs, :], xs_ref[rs, :])
        x1 = x + mod_ref[0, 2:3, :] * _dot(y.astype(BF16), wo_ref[...])
        x1_ref[rs, :] = x1
        n = x1 * lax.rsqrt(jnp.mean(x1 * x1, axis=-1, keepdims=True) + EPS) * n2_ref[...]
        h2 = n * (1.0 + mod_ref[0, 4:5, :]) + mod_ref[0, 3:4, :]
        h2p = _pack_bf16_pair(h2)
        h2_ref[rs, :] = h2p
        _rows_to_tiles(h2t_ref.at[rs], h2p)
        lg_ref[:, rs] = lax.dot_general(rw_ref[...], h2, _NT, precision=lax.Precision.HIGHEST,
                                        preferred_element_type=F32)


def _outproj(hf, hb, z, ha_ctx, ha_lat, x_prompt2, x_sample2, mods, mnorm, n2, wm, wa, wo, rw_t,
             *, t_ctx, dec_seq, mh, dv, omcol, gmcol, gacol):
    t_all = hf.shape[0]
    D = x_prompt2.shape[1]
    mw = mh * dv
    qw = ha_ctx.shape[1]
    E = rw_t.shape[0]
    tm = min(TM_OUTPROJ, t_ctx, dec_seq)
    nctx = t_ctx // tm
    per_seq = dec_seq // tm

    def ctx_blk(i):
        return (jnp.minimum(i, nctx - 1), 0)

    def lat_blk(i):
        return (jnp.maximum(i - nctx, 0), 0)

    def mod_row(i):
        return (jnp.where(i < nctx, 0, 1 + (i - nctx) // per_seq), 0, 0)

    const = lambda i: (0, 0)
    single = pl.Buffered(1)
    return pl.pallas_call(
        functools.partial(_outproj_kernel, nctx_tiles=nctx, mh=mh, dv=dv),
        out_shape=(jax.ShapeDtypeStruct((t_all, D), F32),
                   jax.ShapeDtypeStruct((t_all, D // 2), jnp.uint32),
                   jax.ShapeDtypeStruct((t_all, D // 2 // LANES_V7X, LANES_V7X), jnp.uint32),
                   jax.ShapeDtypeStruct((E, t_all), F32)),
        grid=(t_all // tm,),
        in_specs=[pl.BlockSpec((tm, mw), lambda i: (i, 0)),
                  pl.BlockSpec((tm, mw), lambda i: (i, 0)),
                  pl.BlockSpec((tm, mw), lambda i: (i, omcol)),
                  pl.BlockSpec((tm, qw), ctx_blk),
                  pl.BlockSpec((tm, qw), lat_blk),
                  pl.BlockSpec((tm, D), lambda i: (i, gmcol)),
                  pl.BlockSpec((tm, D), lambda i: (i, gacol)),
                  pl.BlockSpec((tm, D), ctx_blk),
                  pl.BlockSpec((tm, D), lat_blk),
                  pl.BlockSpec((1, N_MOD, D), mod_row),
                  pl.BlockSpec((1, mw), const),
                  pl.BlockSpec((1, D), const),
                  pl.BlockSpec((mw, D), const, pipeline_mode=single),
                  pl.BlockSpec((qw, D), const, pipeline_mode=single),
                  pl.BlockSpec((D, D), const, pipeline_mode=single),
                  pl.BlockSpec((E, D), const, pipeline_mode=single)],
        out_specs=(pl.BlockSpec((tm, D), lambda i: (i, 0)),
                   pl.BlockSpec((tm, D // 2), lambda i: (i, 0)),
                   pl.BlockSpec((tm, D // 2 // LANES_V7X, LANES_V7X), lambda i: (i, 0, 0)),
                   pl.BlockSpec((E, tm), lambda i: (0, i))),
        compiler_params=_params(("arbitrary",)),
        name="outproj",
    )(hf, hb, z, ha_ctx, ha_lat, z, z, x_prompt2, x_sample2, mods, mnorm, n2, wm, wa, wo, rw_t)


def _dispatch_kernel(pstart_ref, pcount_ref, nact_ref, pos_ref, h2t_ref, h2p_ref, x1_ref, mod_ref,
                     w1_ref, w3_ref, w2_ref, xs_hbm, x1s_ref, zblk, sem, psem, *, top_k):
    i = pl.program_id(0)
    td = h2t_ref.shape[0]
    rows = zblk.shape[0]

    @pl.when(i == 0)
    def _():
        zblk[...] = jnp.zeros_like(zblk)
        zrow = zblk.at[0]

        def per_expert(e, total):
            def fill(r, carry):
                pltpu.make_async_copy(zrow, xs_hbm.at[pstart_ref[e] + r], psem).start()
                return carry

            lax.fori_loop(0, pcount_ref[e], fill, 0)
            return total + pcount_ref[e]

        total = lax.fori_loop(0, pstart_ref.shape[0], per_expert, 0)

        def drain(j, carry):
            pltpu.make_async_copy(zrow, xs_hbm.at[0], psem).wait()
            return carry

        lax.fori_loop(0, total, drain, 0)

        def empty_block(b, carry):
            fill = pltpu.make_async_copy(zblk, xs_hbm.at[pl.ds(b * rows, rows)], psem)
            fill.start()
            fill.wait()
            return carry

        lax.fori_loop(nact_ref[0], xs_hbm.shape[0] // rows, empty_block, 0)

    for k in range(top_k):
        for t in range(td):
            pltpu.make_async_copy(h2t_ref.at[t], xs_hbm.at[pos_ref[0, k, t]], sem).start()

    x = _unpack_bf16_pair(h2p_ref[...])
    hmid = (jax.nn.silu(_dot(x, w1_ref[...])) * _dot(x, w3_ref[...])).astype(BF16)
    x1s_ref[...] = x1_ref[...] + mod_ref[0, 5:6, :] * _dot(hmid, w2_ref[...])

    for k in range(top_k):
        pltpu.make_async_copy(h2t_ref, xs_hbm.at[pl.ds(0, td)], sem).wait()


def _dispatch_rows(pad_start, pad_count, nact, pos3, h2t, h2p, x1, mods, sw1, sw3, sw2, *, n_slots, rows,
                   t_ctx, dec_seq):
    nt, K, td = pos3.shape
    T, c, _ = h2t.shape
    D = x1.shape[1]
    F = sw1.shape[1]
    nctx = t_ctx // td
    per_seq = dec_seq // td
    const = lambda i, *_: (0, 0)
    single = pl.Buffered(1)
    grid_spec = pltpu.PrefetchScalarGridSpec(
        num_scalar_prefetch=3,
        grid=(nt,),
        in_specs=[pl.BlockSpec((1, K, td), lambda i, *_: (i, 0, 0), memory_space=pltpu.SMEM),
                  pl.BlockSpec((td, c, LANES_V7X), lambda i, *_: (i, 0, 0)),
                  pl.BlockSpec((td, D // 2), lambda i, *_: (i, 0)),
                  pl.BlockSpec((td, D), lambda i, *_: (i, 0)),
                  pl.BlockSpec((1, N_MOD, D),
                               lambda i, *_: (jnp.where(i < nctx, 0, 1 + (i - nctx) // per_seq), 0, 0)),
                  pl.BlockSpec((D, F), const, pipeline_mode=single),
                  pl.BlockSpec((D, F), const, pipeline_mode=single),
                  pl.BlockSpec((F, D), const, pipeline_mode=single)],
        out_specs=(pl.BlockSpec(memory_space=pl.ANY),
                   pl.BlockSpec((td, D), lambda i, *_: (i, 0))),
        scratch_shapes=[pltpu.VMEM((rows, c, LANES_V7X), jnp.uint32), pltpu.SemaphoreType.DMA(()),
                        pltpu.SemaphoreType.DMA(())],
    )
    return pl.pallas_call(
        functools.partial(_dispatch_kernel, top_k=K),
        out_shape=(jax.ShapeDtypeStruct((n_slots, c, LANES_V7X), jnp.uint32),
                   jax.ShapeDtypeStruct((T, D), F32)),
        grid_spec=grid_spec,
        compiler_params=_params(("arbitrary",)),
        name="dispatch_shared",
    )(pad_start, pad_count, nact, pos3, h2t, h2p, x1, mods, sw1, sw3, sw2)


def _moe_kernel(blk_e, nact_ref, w1_hbm, w3_hbm, w2_hbm, xs_hbm, y_ref,
                xbuf0, xbuf1, w1b, w3b, w2b, stg_in, stg_out, sem, wsem, *, rows):
    i = pl.program_id(0)
    nact = nact_ref[0]
    c = xs_hbm.shape[1]
    e = blk_e[i]

    def fetch(b, buf, s):
        for a in range(c):
            pltpu.make_async_copy(xs_hbm.at[pl.ds(b * rows, rows), a, :],
                                  buf.at[:, pl.ds(a * LANES_V7X, LANES_V7X)], s).start()

    def fetch_wait(buf, s):
        for a in range(c):
            pltpu.make_async_copy(xs_hbm.at[pl.ds(0, rows), a, :],
                                  buf.at[:, pl.ds(a * LANES_V7X, LANES_V7X)], s).wait()

    @pl.when(i == 0)
    def _():
        fetch(0, xbuf0, sem.at[0])

    @pl.when(jnp.logical_and(i < nact, jnp.logical_or(i == 0, e != blk_e[jnp.maximum(i - 1, 0)])))
    def _():
        rc_in, rc_out = stg_in.shape[1], stg_out.shape[1]
        plan = ([(w1_hbm, w1b, stg_in, r, rc_in) for r in range(0, w1b.shape[0], rc_in)]
                + [(w3_hbm, w3b, stg_in, r, rc_in) for r in range(0, w3b.shape[0], rc_in)]
                + [(w2_hbm, w2b, stg_out, r, rc_out) for r in range(0, w2b.shape[0], rc_out)])

        def chunk_copy(n):
            src, _, stg, r, rc = plan[n]
            return pltpu.make_async_copy(src.at[e, pl.ds(r, rc), :], stg.at[n % 2], wsem.at[n % 2])

        chunk_copy(0).start()
        for n, (_, dst, stg, r, rc) in enumerate(plan):
            if n + 1 < len(plan):
                chunk_copy(n + 1).start()
            chunk_copy(n).wait()
            dst[pl.ds(r, rc), :] = stg[n % 2].astype(BF16)

    def block(cur, sem_cur, nxt, sem_nxt):
        fetch_wait(cur, sem_cur)

        @pl.when(i + 1 < nact)
        def _():
            fetch(i + 1, nxt, sem_nxt)

        x = _unpack_bf16_pair(cur[...])
        hmid = (jax.nn.silu(_dot(x, w1b[...])) * _dot(x, w3b[...])).astype(BF16)
        y_ref[...] = _pack_bf16_pair(_dot(hmid, w2b[...]))

    @pl.when(jnp.logical_and(i < nact, i % 2 == 0))
    def _():
        block(xbuf0, sem.at[0], xbuf1, sem.at[1])

    @pl.when(jnp.logical_and(i < nact, i % 2 == 1))
    def _():
        block(xbuf1, sem.at[1], xbuf0, sem.at[0])

    @pl.when(i >= nact)
    def _():
        y_ref[...] = jnp.zeros_like(y_ref)


def _moe(blk_e, nact, xs, w1, w3, w2, *, rows):
    nblk = blk_e.shape[0]
    E, D, F = w1.shape
    hbm = pl.BlockSpec(memory_space=pl.ANY)
    grid_spec = pltpu.PrefetchScalarGridSpec(
        num_scalar_prefetch=2,
        grid=(nblk,),
        in_specs=[hbm, hbm, hbm, hbm],
        out_specs=pl.BlockSpec((rows, D // 2), lambda i, be, na: (i, 0)),
        scratch_shapes=[pltpu.VMEM((rows, D // 2), jnp.uint32), pltpu.VMEM((rows, D // 2), jnp.uint32),
                        pltpu.VMEM((D, F), BF16), pltpu.VMEM((D, F), BF16), pltpu.VMEM((F, D), BF16),
                        pltpu.VMEM((2, D // WEIGHT_CHUNKS, F), F32), pltpu.VMEM((2, F // WEIGHT_CHUNKS, D), F32),
                        pltpu.SemaphoreType.DMA((2,)), pltpu.SemaphoreType.DMA((2,))],
    )
    return pl.pallas_call(
        functools.partial(_moe_kernel, rows=rows),
        out_shape=jax.ShapeDtypeStruct((nblk * rows, D // 2), jnp.uint32),
        grid_spec=grid_spec,
        compiler_params=_params(("arbitrary",)),
        name="routed_experts",
    )(blk_e, nact, w1, w3, w2, xs)


def _router_kernel(lg_ref, rb_ref, eidx_ref, rank_ref, wtok_ref, cnt_ref, carry, *, top_k, n_groups, topk_groups):
    i = pl.program_id(0)

    @pl.when(i == 0)
    def _():
        carry[...] = jnp.zeros_like(carry)

    E, tr = lg_ref.shape
    gs = E // n_groups
    scores = jax.nn.sigmoid(lg_ref[...])
    biased = scores + rb_ref[...]
    b3 = biased.reshape(n_groups, gs, tr)
    io3 = lax.broadcasted_iota(jnp.int32, b3.shape, 1)
    m1 = jnp.max(b3, axis=1, keepdims=True)
    i1 = jnp.min(jnp.where(b3 == m1, io3, gs), axis=1, keepdims=True)
    m2 = jnp.max(jnp.where(io3 == i1, -jnp.inf, b3), axis=1, keepdims=True)
    grp = (m1 + m2).reshape(n_groups, tr)
    iog = lax.broadcasted_iota(jnp.int32, grp.shape, 0)
    sel = jnp.zeros(grp.shape, jnp.bool_)
    for _ in range(topk_groups):
        mx = jnp.max(grp, axis=0, keepdims=True)
        hit = iog == jnp.min(jnp.where(grp == mx, iog, n_groups), axis=0, keepdims=True)
        sel = jnp.logical_or(sel, hit)
        grp = jnp.where(hit, -jnp.inf, grp)
    masked = jnp.where(sel.reshape(n_groups, 1, tr), b3, -jnp.inf).reshape(E, tr)
    ioe = lax.broadcasted_iota(jnp.int32, (E, tr), 0)
    onehot = jnp.zeros((E, tr), F32)
    hits, idxs, ws = [], [], []
    for _ in range(top_k):
        mx = jnp.max(masked, axis=0, keepdims=True)
        ix = jnp.min(jnp.where(masked == mx, ioe, E), axis=0, keepdims=True)
        hit = ioe == ix
        hits.append(hit)
        idxs.append(ix)
        ws.append(jnp.sum(jnp.where(hit, scores, 0.0), axis=0, keepdims=True))
        onehot = onehot + hit.astype(F32)
        masked = jnp.where(hit, -jnp.inf, masked)
    wsum = ws[0]
    for w in ws[1:]:
        wsum = wsum + w
    ri = lax.broadcasted_iota(jnp.int32, (tr, tr), 0)
    ci = lax.broadcasted_iota(jnp.int32, (tr, tr), 1)
    before = _dot(onehot.astype(BF16), (ri < ci).astype(BF16)) + carry[...]
    ranks = [jnp.sum(jnp.where(hit, before, 0.0), axis=0, keepdims=True) for hit in hits]
    carry[...] = carry[...] + jnp.sum(onehot, axis=1, keepdims=True)
    cnt_ref[...] = carry[...].astype(jnp.int32)
    eidx_ref[...] = jnp.concatenate(idxs, axis=0)
    rank_ref[...] = jnp.concatenate(ranks, axis=0).astype(jnp.int32)
    wrows = jnp.concatenate([w / wsum * ROUTED_SCALE for w in ws]
                            + [jnp.zeros((LANES_V7X - top_k, tr), F32)], axis=0)
    wtok_ref[...] = wrows.T


def _router(logits_t, router_b):
    E, T = logits_t.shape
    tr = min(TR_ROUTER, T)
    return pl.pallas_call(
        functools.partial(_router_kernel, top_k=TOP_K, n_groups=N_GROUPS, topk_groups=TOPK_GROUPS),
        out_shape=(jax.ShapeDtypeStruct((TOP_K, T), jnp.int32),
                   jax.ShapeDtypeStruct((TOP_K, T), jnp.int32),
                   jax.ShapeDtypeStruct((T, LANES_V7X), F32),
                   jax.ShapeDtypeStruct((E, 1), jnp.int32)),
        grid=(T // tr,),
        in_specs=[pl.BlockSpec((E, tr), lambda i: (0, i)),
                  pl.BlockSpec((E, 1), lambda i: (0, 0))],
        out_specs=(pl.BlockSpec((TOP_K, tr), lambda i: (0, i)),
                   pl.BlockSpec((TOP_K, tr), lambda i: (0, i)),
                   pl.BlockSpec((tr, LANES_V7X), lambda i: (i, 0)),
                   pl.BlockSpec((E, 1), lambda i: (0, 0))),
        scratch_shapes=[pltpu.VMEM((E, 1), F32)],
        compiler_params=_params(("arbitrary",)),
        name="router",
    )(logits_t, router_b.reshape(E, 1))


def _slot_pos_kernel(start_ref, eidx_ref, rank_ref, pos_ref, *, n_experts):
    eidx = eidx_ref[...]
    pos = rank_ref[...]
    for e in range(n_experts):
        pos = pos + jnp.where(eidx == e, start_ref[e], 0)
    pos_ref[...] = pos


def _slot_pos(start_pad, eidx, rank):
    K, T = eidx.shape
    tl = min(TL_SLOTPOS, T)
    return pl.pallas_call(
        functools.partial(_slot_pos_kernel, n_experts=start_pad.shape[0]),
        out_shape=jax.ShapeDtypeStruct((K, T), jnp.int32),
        grid=(T // tl,),
        in_specs=[pl.BlockSpec(memory_space=pltpu.SMEM),
                  pl.BlockSpec((K, tl), lambda i: (0, i)),
                  pl.BlockSpec((K, tl), lambda i: (0, i))],
        out_specs=pl.BlockSpec((K, tl), lambda i: (0, i)),
        compiler_params=_params(("arbitrary",)),
        name="slot_pos",
    )(start_pad, eidx, rank)


def _dispatch(eidx, rank, counts, rows):
    K, T = eidx.shape
    A = K * T
    E = counts.shape[0]
    padded = (counts + rows - 1) // rows * rows
    end_pad = jnp.cumsum(padded)
    start_pad = end_pad - padded
    nblk = -(-(A + E * (rows - 1)) // rows)
    n_slots = nblk * rows
    nact = end_pad[-1] // rows
    blk = jnp.arange(nblk, dtype=jnp.int32)
    blk_e = jnp.sum(end_pad[None, :] <= (blk * rows)[:, None], axis=1).astype(jnp.int32)
    last_e = jnp.sum(end_pad <= (nact - 1) * rows).astype(jnp.int32)
    blk_e = jnp.where(blk < nact, blk_e, last_e)
    pos = _slot_pos(start_pad.astype(jnp.int32), eidx, rank)
    pad_start = (start_pad + counts).astype(jnp.int32)
    pad_count = (padded - counts).astype(jnp.int32)
    return pos, blk_e, nact.astype(jnp.int32).reshape(1), pad_start, pad_count, n_slots


def _tile_major(pos, tile):
    K, T = pos.shape
    return pos.reshape(K, T // tile, tile).transpose(1, 0, 2)


def _final_kernel(pos_ref, posn_ref, x1s_ref, wt_ref, mod_ref, fn_ref, y_hbm,
                  oc_ref, ol_ref, ybuf0, ybuf1, sem, *, top_k, nctx_tiles):
    i = pl.program_id(0)
    nt = pl.num_programs(0)
    tm = x1s_ref.shape[0]

    def gather(idx_ref, buf, s):
        for k in range(top_k):
            for t in range(tm):
                pltpu.make_async_copy(y_hbm.at[pl.ds(idx_ref[0, k, t], 1)], buf.at[k, pl.ds(t, 1)], s).start()

    def gather_wait(buf, s):
        for k in range(top_k):
            pltpu.make_async_copy(y_hbm.at[pl.ds(0, tm)], buf.at[k], s).wait()

    @pl.when(i == 0)
    def _():
        gather(pos_ref, ybuf0, sem.at[0])

    def tile(cur, sem_cur, nxt, sem_nxt):
        @pl.when(i + 1 < nt)
        def _():
            gather(posn_ref, nxt, sem_nxt)

        gather_wait(cur, sem_cur)
        wt = wt_ref[...]
        lo = hi = None
        for k in range(top_k):
            w = cur[k]
            wk = wt[:, k:k + 1]
            lo_k = pltpu.bitcast(w << 16, F32) * wk
            hi_k = pltpu.bitcast(w & jnp.uint32(0xFFFF0000), F32) * wk
            lo = lo_k if lo is None else lo + lo_k
            hi = hi_k if hi is None else hi + hi_k
        x2 = x1s_ref[...] + mod_ref[0, 5:6, :] * jnp.concatenate([lo, hi], axis=1)
        out = x2 * lax.rsqrt(jnp.mean(x2 * x2, axis=-1, keepdims=True) + EPS) * fn_ref[...]

        @pl.when(i < nctx_tiles)
        def _():
            oc_ref[...] = out

        @pl.when(i >= nctx_tiles)
        def _():
            ol_ref[...] = out

    @pl.when(i % 2 == 0)
    def _():
        tile(ybuf0, sem.at[0], ybuf1, sem.at[1])

    @pl.when(i % 2 == 1)
    def _():
        tile(ybuf1, sem.at[1], ybuf0, sem.at[0])


def _final(pos3, x1s, wtok, ys, mods, fnorm, *, t_ctx, dec_seq):
    nt, K, tm = pos3.shape
    t_all, D = x1s.shape
    nctx = t_ctx // tm
    per_seq = dec_seq // tm

    def mod_row(i):
        return (jnp.where(i < nctx, 0, 1 + (i - nctx) // per_seq), 0, 0)

    const = lambda i: (0, 0)
    smem_blk = lambda f: pl.BlockSpec((1, K, tm), f, memory_space=pltpu.SMEM)
    return pl.pallas_call(
        functools.partial(_final_kernel, top_k=K, nctx_tiles=nctx),
        out_shape=(jax.ShapeDtypeStruct((t_ctx, D), F32), jax.ShapeDtypeStruct((t_all - t_ctx, D), F32)),
        grid=(nt,),
        in_specs=[smem_blk(lambda i: (i, 0, 0)),
                  smem_blk(lambda i: (jnp.minimum(i + 1, nt - 1), 0, 0)),
                  pl.BlockSpec((tm, D), lambda i: (i, 0)),
                  pl.BlockSpec((tm, LANES_V7X), lambda i: (i, 0)),
                  pl.BlockSpec((1, N_MOD, D), mod_row),
                  pl.BlockSpec((1, D), const),
                  pl.BlockSpec(memory_space=pl.ANY)],
        out_specs=(pl.BlockSpec((tm, D), lambda i: (jnp.minimum(i, nctx - 1), 0)),
                   pl.BlockSpec((tm, D), lambda i: (jnp.maximum(i - nctx, 0), 0))),
        scratch_shapes=[pltpu.VMEM((K, tm, D // 2), jnp.uint32), pltpu.VMEM((K, tm, D // 2), jnp.uint32),
                        pltpu.SemaphoreType.DMA((2,))],
        compiler_params=_params(("arbitrary",)),
        name="combine_final",
    )(pos3, pos3, x1s, wtok, mods, fnorm, ys)


def _rope_tables(dec_seq, hd):
    nf = hd // 4
    t = jnp.arange(dec_seq)
    inv = ROPE_BASE ** (-jnp.arange(nf, dtype=F32) / nf)
    ang_r = (t // GRID_W).astype(F32)[:, None] * inv
    ang_c = (t % GRID_W).astype(F32)[:, None] * inv
    cos = jnp.concatenate([jnp.cos(ang_r)] * 2 + [jnp.cos(ang_c)] * 2, axis=1)
    sin = jnp.concatenate([-jnp.sin(ang_r), jnp.sin(ang_r), -jnp.sin(ang_c), jnp.sin(ang_c)], axis=1)
    return cos, sin


def _scan_steps(batch, seq, dec_batch, dec_seq, L):
    fb, bb, sq, fi, la = [], [], [], [], []
    base = 0
    for sid, S in enumerate([seq] * batch + [dec_seq] * dec_batch):
        nc = S // L
        for c in range(nc):
            fb.append(base + c)
            bb.append(base + nc - 1 - c)
            sq.append(sid)
            fi.append(int(c == 0))
            la.append(int(c == nc - 1))
        base += nc
    return tuple(jnp.asarray(np.asarray(a, dtype=np.int32)) for a in (fb, bb, sq, fi, la))


def kernel(x_prompt, x_sample, cache_k, cache_v, state_mlstm_C, state_mlstm_n, state_mlstm_m, c, c_ctx,
           w_mod, b_mod, norm1_w, norm2_w, w_in, igate_b, fgate_b, mlstm_norm_w, attn_sink,
           w_branch_m, w_branch_a, w_out, router_w, router_b, expert_w1, expert_w3, expert_w2,
           shared_w1, shared_w3, shared_w2, final_norm_w):
    batch, seq, D = x_prompt.shape
    dec_batch, dec_seq, _ = x_sample.shape
    depth = w_in.shape[0]
    assert depth == 1, "single trunk layer"
    _, _, past, kvh, hd = cache_k.shape
    mh, dk, dv = state_mlstm_C.shape[3:]
    ah = attn_sink.shape[1]
    groups = ah // kvh
    E = router_w.shape[2]
    assert dk == dv == hd == LANES_V7X
    t_ctx, t_lat = batch * seq, dec_batch * dec_seq
    mw, qw, kw = mh * dk, ah * hd, kvh * hd
    ng = 4 * mh

    wi = w_in[0]
    o = 0
    seg = {}
    for name, width in (("qm", mw), ("km", mw), ("vm", mw), ("om", mw), ("im", 2 * mh), ("fm", 2 * mh),
                        ("qa", qw), ("ka", kw), ("va", kw), ("gm", D), ("ga", D)):
        seg[name] = wi[:, o:o + width]
        o += width
    order = ("qa", "gm", "ga", "qm", "km", "vm", "om", "ka", "va")
    w_main = jnp.concatenate([seg[nm] for nm in order], axis=1).astype(BF16)
    col = {}
    o = 0
    for nm in order:
        col[nm] = o
        o += seg[nm].shape[1]
    tn = 2 * kw
    for nm in order[:-2]:
        assert col[nm] % tn == 0 and seg[nm].shape[1] % tn == 0
    for nm, width in (("gm", D), ("ga", D), ("qm", mw), ("km", mw), ("vm", mw), ("om", mw), ("ka", kw), ("va", kw)):
        assert col[nm] % width == 0
    w_gate = jnp.pad(jnp.concatenate([seg["im"], seg["fm"]], axis=1), ((0, 0), (0, LANES_V7X - ng))).astype(BF16)

    R = -(-(1 + dec_batch) // 8) * 8
    cond = jnp.concatenate([c_ctx[None, :], c, jnp.zeros((R - 1 - dec_batch, D), F32)], axis=0)
    mods = _modulation(cond, w_mod[0], b_mod[0]).reshape(R, N_MOD, D)

    xp2 = x_prompt.reshape(t_ctx, D)
    xs2 = x_sample.reshape(t_lat, D)
    cos, sin = _rope_tables(dec_seq, hd)
    h1, gates, gates_t = _prenorm(xp2, xs2, mods, norm1_w, w_gate, t_ctx=t_ctx, dec_seq=dec_seq, n_gates=ng)
    z, kv32 = _inproj(h1, w_main, cos, sin, t_ctx=t_ctx, dec_seq=dec_seq, tn=tn, n_rope_tiles=qw // tn)

    nseq = batch + dec_batch
    C0 = jnp.concatenate([jnp.zeros((batch, 2, mh, dk, dv), F32), state_mlstm_C[:, 0]], axis=0)
    n0 = jnp.concatenate([jnp.zeros((batch, 2, mh, dk), F32), state_mlstm_n[:, 0]], axis=0)
    m0 = jnp.concatenate([jnp.zeros((batch, 2, mh), F32), state_mlstm_m[:, 0]], axis=0)
    s0 = jnp.concatenate([C0, jnp.broadcast_to(n0[..., None], (nseq, 2, mh, dk, dv))], axis=-1)
    s0 = s0.reshape(nseq, 2 * mh, dk, 2 * dv)
    m0 = jnp.broadcast_to(m0.reshape(nseq, 2 * mh, 1), (nseq, 2 * mh, LANES_V7X))
    gate_b = jnp.concatenate([igate_b[0].reshape(-1), fgate_b[0].reshape(-1)])
    steps = _scan_steps(batch, seq, dec_batch, dec_seq, M_CHUNK)
    hf, hb, s_fin, m_fin = _mlstm(z, gates, gates_t, gate_b.reshape(1, ng), gate_b.reshape(ng, 1), s0, m0, steps,
                                  mh=mh, dk=dk, qcol=col["qm"] // mw, kcol=col["km"] // mw, vcol=col["vm"] // mw)

    sink = attn_sink[0]
    ha_ctx = _ctx_attention(sink, z, batch=batch, seq=seq, kvh=kvh, groups=groups, hd=hd,
                            kcol=col["ka"] // kw, vcol=col["va"] // kw)
    ha_lat = _lat_attention(sink, z, cache_k.reshape(dec_batch, depth, past, kw),
                            cache_v.reshape(dec_batch, depth, past, kw), t_ctx=t_ctx, dec_batch=dec_batch,
                            dec_seq=dec_seq, kvh=kvh, groups=groups, hd=hd, kcol=col["ka"] // kw,
                            vcol=col["va"] // kw)

    x1, h2p, h2t, logits_t = _outproj(hf, hb, z, ha_ctx, ha_lat, xp2, xs2, mods, mlstm_norm_w, norm2_w,
                                w_branch_m[0].astype(BF16), w_branch_a[0].astype(BF16), w_out[0].astype(BF16),
                                router_w[0].T, t_ctx=t_ctx, dec_seq=dec_seq, mh=mh, dv=dv,
                                omcol=col["om"] // mw, gmcol=col["gm"] // D, gacol=col["ga"] // D)

    eidx, rank, wtok, counts = _router(logits_t, router_b[0])
    pos, blk_e, nact, pad_start, pad_count, n_slots = _dispatch(eidx, rank, counts[:, 0], EXPERT_ROWS)
    xs, x1s = _dispatch_rows(pad_start, pad_count, nact, _tile_major(pos, min(TD_DISPATCH, t_ctx, dec_seq)),
                             h2t, h2p, x1, mods, shared_w1[0].astype(BF16), shared_w3[0].astype(BF16),
                             shared_w2[0].astype(BF16), n_slots=n_slots, rows=EXPERT_ROWS, t_ctx=t_ctx,
                             dec_seq=dec_seq)
    ys = _moe(blk_e, nact, xs, expert_w1[0], expert_w3[0], expert_w2[0], rows=EXPERT_ROWS)

    y_ctx, y_lat = _final(_tile_major(pos, min(TM_FINAL, t_ctx, dec_seq)), x1s, wtok, ys, mods,
                          final_norm_w.reshape(1, D), t_ctx=t_ctx, dec_seq=dec_seq)

    y_prompt = y_ctx.reshape(batch, seq, D)
    y_sample = y_lat.reshape(dec_batch, dec_seq, D)
    new_k = kv32[:t_ctx, :kw].reshape(batch, 1, seq, kvh, hd)
    new_v = kv32[:t_ctx, kw:].reshape(batch, 1, seq, kvh, hd)
    s_ctx = s_fin[:batch].reshape(batch, 1, 2, mh, dk, 2 * dv)
    new_C = s_ctx[..., :dv]
    new_n = s_ctx[..., dv]
    new_m = m_fin[:batch, :, 0].reshape(batch, 1, 2, mh)
    return y_prompt, y_sample, new_k, new_v, new_C, new_n, new_m
```

```python
import functools

import numpy as np
import jax
import jax.numpy as jnp
from jax import lax
from jax.experimental import pallas as pl
from jax.experimental.pallas import tpu as pltpu

TOP_K = 6
N_GROUPS = 8
TOPK_GROUPS = 4
ROUTED_SCALE = 2.5
WINDOW = 128
Q_BLOCK = 128
GRID_W = 64
ROPE_BASE = 10000.0
M_CHUNK = 128
N_MOD = 6
EPS = 1e-6

LANES_V7X = 128
MXU_COLS_V7X = 256
VMEM_LIMIT_V7X = 56 * 1024 * 1024

TM_PRENORM = 512
TM_INPROJ = 1024
TM_OUTPROJ = 256
OUTPROJ_SPLIT = 2
WEIGHT_CHUNKS = 8
WEIGHT_RING = 4
TM_FINAL = 256
TN_MOD = 1024
EXPERT_ROWS = 256
TR_ROUTER = 512
TL_SLOTPOS = 2048
TD_DISPATCH = 256

F32 = jnp.float32
BF16 = jnp.bfloat16
_NT = (((1,), (1,)), ((), ()))


def _params(sem):
    return pltpu.CompilerParams(dimension_semantics=sem, vmem_limit_bytes=VMEM_LIMIT_V7X)


def _dot(a, b):
    return jnp.dot(a, b, preferred_element_type=F32)


def _dot_nt(a, b):
    return lax.dot_general(a, b, _NT, preferred_element_type=F32)


def _pack_bf16_pair(x):
    c = x.shape[1] // 2
    lo = pltpu.bitcast(x[:, :c].astype(BF16).astype(F32), jnp.uint32)
    hi = pltpu.bitcast(x[:, c:].astype(BF16).astype(F32), jnp.uint32)
    return (lo >> 16) | (hi & jnp.uint32(0xFFFF0000))


def _rows_to_tiles(ref, x):
    for a in range(ref.shape[1]):
        ref[:, a, :] = x[:, a * LANES_V7X:(a + 1) * LANES_V7X]


def _unpack_bf16_pair(w):
    lo = pltpu.bitcast(w << 16, F32).astype(BF16)
    hi = pltpu.bitcast(w & jnp.uint32(0xFFFF0000), F32).astype(BF16)
    return jnp.concatenate([lo, hi], axis=1)


def _mod_kernel(c_ref, w_ref, b_ref, o_ref):
    s = jax.nn.silu(c_ref[...]).astype(BF16)
    o_ref[...] = _dot(s, w_ref[...].astype(BF16)) + b_ref[...]


def _modulation(cond, w_mod, b_mod):
    R, D = cond.shape
    N = w_mod.shape[1]
    tn = min(TN_MOD, N)
    return pl.pallas_call(
        _mod_kernel,
        out_shape=jax.ShapeDtypeStruct((R, N), F32),
        grid=(N // tn,),
        in_specs=[pl.BlockSpec((R, D), lambda n: (0, 0)),
                  pl.BlockSpec((D, tn), lambda n: (0, n)),
                  pl.BlockSpec((1, tn), lambda n: (0, n))],
        out_specs=pl.BlockSpec((R, tn), lambda n: (0, n)),
        compiler_params=_params(("arbitrary",)),
        name="modulation",
    )(cond, w_mod, b_mod.reshape(1, N))


def _rope_slice(x, cos, sin_signed, first_half):
    swap = jnp.where(first_half, pltpu.roll(x, 96, 1), pltpu.roll(x, 32, 1))
    return x * cos + swap * sin_signed


def _prenorm_kernel(xp_ref, xs_ref, mod_ref, n1_ref, wg_ref, h_ref, g_ref, gt_ref, *, nctx_tiles, n_gates):
    i = pl.program_id(0)
    x = jnp.where(i < nctx_tiles, xp_ref[...], xs_ref[...])
    y = x * lax.rsqrt(jnp.mean(x * x, axis=-1, keepdims=True) + EPS) * n1_ref[...]
    h = (y * (1.0 + mod_ref[0, 1:2, :]) + mod_ref[0, 0:1, :]).astype(BF16)
    h_ref[...] = h
    g = _dot(h, wg_ref[...])
    g_ref[...] = g[:, :n_gates]
    gt_ref[...] = g.T[:n_gates, :]


def _prenorm(x_prompt2, x_sample2, mods, n1, w_gate, *, t_ctx, dec_seq, n_gates):
    t_lat, D = x_sample2.shape
    t_all = t_ctx + t_lat
    tm = min(TM_PRENORM, t_ctx, dec_seq)
    nctx = t_ctx // tm
    per_seq = dec_seq // tm
    return pl.pallas_call(
        functools.partial(_prenorm_kernel, nctx_tiles=nctx, n_gates=n_gates),
        out_shape=(jax.ShapeDtypeStruct((t_all, D), BF16),
                   jax.ShapeDtypeStruct((t_all, n_gates), F32),
                   jax.ShapeDtypeStruct((n_gates, t_all), F32)),
        grid=(t_all // tm,),
        in_specs=[pl.BlockSpec((tm, D), lambda i: (jnp.minimum(i, nctx - 1), 0)),
                  pl.BlockSpec((tm, D), lambda i: (jnp.maximum(i - nctx, 0), 0)),
                  pl.BlockSpec((1, N_MOD, D), lambda i: (jnp.where(i < nctx, 0, 1 + (i - nctx) // per_seq), 0, 0)),
                  pl.BlockSpec((1, D), lambda i: (0, 0)),
                  pl.BlockSpec((D, LANES_V7X), lambda i: (0, 0))],
        out_specs=(pl.BlockSpec((tm, D), lambda i: (i, 0)),
                   pl.BlockSpec((tm, n_gates), lambda i: (i, 0)),
                   pl.BlockSpec((n_gates, tm), lambda i: (0, i))),
        compiler_params=_params(("arbitrary",)),
        name="prenorm",
    )(x_prompt2, x_sample2, mods, n1, w_gate)


def _inproj_kernel(h_ref, w_ref, cos_ref, sin_ref, z_ref, kv_ref, *, nctx_tiles, n_rope_tiles, kv_tile):
    i = pl.program_id(0)
    n = pl.program_id(1)
    is_ctx = i < nctx_tiles
    tn = z_ref.shape[1]
    acc = _dot(h_ref[...], w_ref[...])

    def rope_cols(ncols):
        cos = jnp.where(is_ctx, 1.0, cos_ref[...])
        sin = jnp.where(is_ctx, 0.0, sin_ref[...])
        lane = lax.broadcasted_iota(jnp.int32, cos.shape, 1)
        first_half = (lane % 64) < 32
        return [_rope_slice(acc[:, c:c + LANES_V7X], cos, sin, first_half)
                for c in range(0, ncols, LANES_V7X)]

    @pl.when(n < n_rope_tiles)
    def _():
        z_ref[...] = jnp.concatenate(rope_cols(tn), axis=1).astype(BF16)

    @pl.when(n == kv_tile)
    def _():
        r = jnp.concatenate(rope_cols(tn // 2) + [acc[:, tn // 2:]], axis=1)
        z_ref[...] = r.astype(BF16)
        kv_ref[...] = r

    @pl.when(jnp.logical_and(n >= n_rope_tiles, n != kv_tile))
    def _():
        z_ref[...] = acc.astype(BF16)


def _inproj(h, w_main, cos, sin, *, t_ctx, dec_seq, tn, n_rope_tiles):
    t_all, D = h.shape
    tm = min(TM_INPROJ, t_ctx, dec_seq)
    nctx = t_ctx // tm
    per_seq = dec_seq // tm
    ncols = w_main.shape[1]
    ntile = ncols // tn
    kv_tile = ntile - 1

    def pos_blk(i):
        return jnp.where(i < nctx, 0, (i - nctx) % per_seq)

    kernel = functools.partial(_inproj_kernel, nctx_tiles=nctx, n_rope_tiles=n_rope_tiles, kv_tile=kv_tile)
    return pl.pallas_call(
        kernel,
        out_shape=(jax.ShapeDtypeStruct((t_all, ncols), BF16),
                   jax.ShapeDtypeStruct((t_all, tn), F32)),
        grid=(t_all // tm, ntile),
        in_specs=[pl.BlockSpec((tm, D), lambda i, n: (i, 0)),
                  pl.BlockSpec((D, tn), lambda i, n: (0, n)),
                  pl.BlockSpec((tm, LANES_V7X), lambda i, n: (pos_blk(i), 0)),
                  pl.BlockSpec((tm, LANES_V7X), lambda i, n: (pos_blk(i), 0))],
        out_specs=(pl.BlockSpec((tm, tn), lambda i, n: (i, n)),
                   pl.BlockSpec((tm, tn), lambda i, n: (i, 0))),
        compiler_params=_params(("arbitrary", "arbitrary")),
        name="inproj",
    )(h, w_main, cos, sin)


def _mlstm_kernel(fb, bb, sq, fi, la,
                  qf, kf, vf, qb, kb, vb, gf, gb, gtf, gtb, brow, bcol, s0, m0,
                  hf, hb, s_out, m_out, s_scr, m_scr, *, mh, dk, chunk):
    s = pl.program_id(0)
    L = chunk
    assert L == dk
    scale = dk ** -0.5

    @pl.when(fi[s] == 1)
    def _():
        s_scr[...] = s0[0]
        m_scr[...] = m0[0]

    ri = lax.broadcasted_iota(jnp.int32, (L, L), 0)
    ci = lax.broadcasted_iota(jnp.int32, (L, L), 1)
    low = ri >= ci
    upp = ri <= ci
    low_f = low.astype(F32)
    upp_f = upp.astype(F32)
    ones_blk = jnp.ones((L, dk), BF16)
    hi = lax.Precision.HIGHEST
    refs = ((qf, kf, vf, gf, gtf, hf), (qb, kb, vb, gb, gtb, hb))
    units = [(dr, h) for dr in range(2) for h in range(mh)]
    sl = lambda h: slice(h * dk, (h + 1) * dk)

    gate = []
    for dr, (_, _, _, g_ref, gt_ref, _) in enumerate(refs):
        G = g_ref[...] + brow[...]
        GT = gt_ref[...] + bcol[...]
        ic_col = G[:, dr * mh:(dr + 1) * mh]
        lf_col = jax.nn.log_sigmoid(G[:, (2 + dr) * mh:(3 + dr) * mh])
        ic_row = GT[dr * mh:(dr + 1) * mh, :]
        lf_row = jax.nn.log_sigmoid(GT[(2 + dr) * mh:(3 + dr) * mh, :])
        b_col = jnp.dot(low_f if dr == 0 else upp_f, lf_col, precision=hi, preferred_element_type=F32)
        b_row = jnp.dot(lf_row, upp_f if dr == 0 else low_f, precision=hi, preferred_element_type=F32)
        gate.append((ic_col, ic_row, b_col, b_row))

    S_prev = [s_scr[dr * mh + h] for dr, h in units]
    m_prev = [m_scr[dr * mh + h:dr * mh + h + 1, 0:1] for dr, h in units]
    q = [refs[dr][0][:, sl(h)] for dr, h in units]
    k = [refs[dr][1][:, sl(h)] for dr, h in units]
    v_aug = [jnp.concatenate([refs[dr][2][:, sl(h)], ones_blk], axis=1) for dr, h in units]
    qk = [_dot_nt(q[u], k[u]) for u in range(len(units))]
    qs = [_dot(q[u], S_prev[u].astype(BF16)) for u in range(len(units))]

    sm, w_inter, floor, b_rep = [], [], [], []
    for u, (dr, h) in enumerate(units):
        ic_col, ic_row, b_col, b_row = gate[dr]
        bc = jnp.broadcast_to(b_col[:, h:h + 1], (L, L))
        d = jnp.where(low if dr == 0 else upp, bc - b_row[h:h + 1, :] + ic_row[h:h + 1, :], -jnp.inf)
        inter = bc + m_prev[u]
        m_t = jnp.maximum(inter, jnp.broadcast_to(jnp.max(d, axis=-1, keepdims=True), (L, L)))
        sm.append((qk[u] * scale * jnp.exp(d - m_t)).astype(BF16))
        w_inter.append(jnp.exp(inter - m_t))
        floor.append(jnp.exp(-m_t))
        b_rep.append(bc)

    sv = [_dot(sm[u], v_aug[u]) for u in range(len(units))]
    for u, (dr, h) in enumerate(units):
        num = sv[u][:, :dk] + w_inter[u] * qs[u][:, :dk]
        den = sv[u][:, dk:] + w_inter[u] * qs[u][:, dk:]
        refs[dr][5][:, sl(h)] = (num / jnp.maximum(jnp.abs(den), floor[u])).astype(BF16)

    kw_t, wc, m_new = [], [], []
    for u, (dr, h) in enumerate(units):
        ic_col = gate[dr][0]
        bc = b_rep[u]
        b_last = bc[L - 1:L, :] if dr == 0 else bc[0:1, :]
        g = b_last - bc + jnp.broadcast_to(ic_col[:, h:h + 1], (L, L))
        mn = jnp.maximum(b_last + m_prev[u], jnp.max(g, axis=0, keepdims=True))
        kw_t.append((k[u].astype(F32) * (jnp.exp(g - mn) * scale)).T.astype(BF16))
        wc.append(jnp.exp(b_last + m_prev[u] - mn))
        m_new.append(mn)

    upd = [_dot(kw_t[u], v_aug[u]) for u in range(len(units))]
    for u, (dr, h) in enumerate(units):
        r = dr * mh + h
        s_scr[r] = jnp.concatenate([wc[u], wc[u]], axis=1) * S_prev[u] + upd[u]
        m_scr[r:r + 1, :] = m_new[u]

    @pl.when(la[s] == 1)
    def _():
        s_out[0] = s_scr[...]
        m_out[0] = m_scr[...]


def _mlstm(z, gates, gates_t, brow, bcol, s0, m0, steps, *, mh, dk, qcol, kcol, vcol):
    t_all = z.shape[0]
    L = M_CHUNK
    mw = mh * dk
    ng = gates.shape[1]
    nseq = s0.shape[0]
    fb, bb, sq, fi, la = steps
    nsteps = fb.shape[0]

    def zspec(which, col):
        return pl.BlockSpec((L, mw), lambda s, fb, bb, sq, fi, la: ((fb, bb)[which][s], col))

    def gspec(which):
        return pl.BlockSpec((L, ng), lambda s, fb, bb, sq, fi, la: ((fb, bb)[which][s], 0))

    def gtspec(which):
        return pl.BlockSpec((ng, L), lambda s, fb, bb, sq, fi, la: (0, (fb, bb)[which][s]))

    grid_spec = pltpu.PrefetchScalarGridSpec(
        num_scalar_prefetch=5,
        grid=(nsteps,),
        in_specs=[zspec(0, qcol), zspec(0, kcol), zspec(0, vcol),
                  zspec(1, qcol), zspec(1, kcol), zspec(1, vcol),
                  gspec(0), gspec(1), gtspec(0), gtspec(1),
                  pl.BlockSpec((1, ng), lambda s, *_: (0, 0)),
                  pl.BlockSpec((ng, 1), lambda s, *_: (0, 0)),
                  pl.BlockSpec((1, 2 * mh, dk, 2 * dk), lambda s, fb, bb, sq, fi, la: (sq[s], 0, 0, 0)),
                  pl.BlockSpec((1, 2 * mh, LANES_V7X), lambda s, fb, bb, sq, fi, la: (sq[s], 0, 0))],
        out_specs=(pl.BlockSpec((L, mw), lambda s, fb, bb, sq, fi, la: (fb[s], 0)),
                   pl.BlockSpec((L, mw), lambda s, fb, bb, sq, fi, la: (bb[s], 0)),
                   pl.BlockSpec((1, 2 * mh, dk, 2 * dk), lambda s, fb, bb, sq, fi, la: (sq[s], 0, 0, 0)),
                   pl.BlockSpec((1, 2 * mh, LANES_V7X), lambda s, fb, bb, sq, fi, la: (sq[s], 0, 0))),
        scratch_shapes=[pltpu.VMEM((2 * mh, dk, 2 * dk), F32), pltpu.VMEM((2 * mh, LANES_V7X), F32)],
    )
    return pl.pallas_call(
        functools.partial(_mlstm_kernel, mh=mh, dk=dk, chunk=L),
        out_shape=(jax.ShapeDtypeStruct((t_all, mw), BF16),
                   jax.ShapeDtypeStruct((t_all, mw), BF16),
                   jax.ShapeDtypeStruct((nseq, 2 * mh, dk, 2 * dk), F32),
                   jax.ShapeDtypeStruct((nseq, 2 * mh, LANES_V7X), F32)),
        grid_spec=grid_spec,
        compiler_params=_params(("arbitrary",)),
        name="mlstm",
    )(fb, bb, sq, fi, la, z, z, z, z, z, z, gates, gates, gates_t, gates_t, brow, bcol, s0, m0)


def _sink_column(sink_ref, kv, groups, rows_per_group):
    shape = (groups * rows_per_group, LANES_V7X)
    row_g = lax.broadcasted_iota(jnp.int32, shape, 0) // rows_per_group
    col = jnp.full(shape, sink_ref[kv * groups], F32)
    for g in range(1, groups):
        col = jnp.where(row_g == g, sink_ref[kv * groups + g], col)
    return col


def _softmax_probs(scores, sink):
    hd = sink.shape[1]
    s = jnp.concatenate(scores, axis=1)
    mx = jnp.maximum(jnp.broadcast_to(jnp.max(s, axis=-1, keepdims=True), sink.shape), sink)
    return jnp.exp(s - jnp.concatenate([mx] * (s.shape[1] // hd), axis=1)).astype(BF16), mx


def _weighted_values(p, mx, values, sink):
    hd = sink.shape[1]
    acc = None
    off = 0
    for v in values:
        v_aug = jnp.concatenate([v, jnp.ones(v.shape, BF16)], axis=1)
        pv = _dot(p[:, off:off + v.shape[0]], v_aug)
        acc = pv if acc is None else acc + pv
        off += v.shape[0]
    return acc[:, :hd] / (acc[:, hd:] + jnp.exp(sink - mx))


def _ctx_attn_kernel(sink_ref, q_ref, k_ref, v_ref, o_ref, *, kvh, groups, hd):
    S = q_ref.shape[0]
    scale = hd ** -0.5
    for kv in range(kvh):
        k = k_ref[:, kv * hd:(kv + 1) * hd]
        v = v_ref[:, kv * hd:(kv + 1) * hd]
        q = jnp.concatenate([q_ref[:, (kv * groups + g) * hd:(kv * groups + g + 1) * hd]
                             for g in range(groups)], axis=0)
        sink = _sink_column(sink_ref, kv, groups, S)
        p, mx = _softmax_probs([_dot_nt(q, k) * scale], sink)
        o = _weighted_values(p, mx, [v], sink)
        for g in range(groups):
            o_ref[:, (kv * groups + g) * hd:(kv * groups + g + 1) * hd] = o[g * S:(g + 1) * S].astype(BF16)


def _ctx_attention(sink, z, *, batch, seq, kvh, groups, hd, kcol, vcol):
    qw = kvh * groups * hd
    kw = kvh * hd
    return pl.pallas_call(
        functools.partial(_ctx_attn_kernel, kvh=kvh, groups=groups, hd=hd),
        out_shape=jax.ShapeDtypeStruct((batch * seq, qw), BF16),
        grid=(batch,),
        in_specs=[pl.BlockSpec(memory_space=pltpu.SMEM),
                  pl.BlockSpec((seq, qw), lambda b: (b, 0)),
                  pl.BlockSpec((seq, kw), lambda b: (b, kcol)),
                  pl.BlockSpec((seq, kw), lambda b: (b, vcol))],
        out_specs=pl.BlockSpec((seq, qw), lambda b: (b, 0)),
        compiler_params=_params(("arbitrary",)),
        name="ctx_attention",
    )(sink, z, z, z)


def _lat_attn_kernel(sink_ref, q_ref, kp_ref, kc_ref, kn_ref, vp_ref, vc_ref, vn_ref, ck_ref, cv_ref, o_ref,
                     *, kvh, groups, hd):
    j = pl.program_id(1)
    nb = pl.num_programs(1)
    Q = q_ref.shape[0]
    scale = hd ** -0.5
    R = groups * Q
    rq = lax.broadcasted_iota(jnp.int32, (R, Q), 0) % Q
    cc = lax.broadcasted_iota(jnp.int32, (R, Q), 1)
    mask_prev = jnp.logical_and(cc >= rq, j > 0)
    mask_next = jnp.logical_and(cc <= rq, j < nb - 1)
    heads = [slice(kv * hd, (kv + 1) * hd) for kv in range(kvh)]
    sinks = [_sink_column(sink_ref, kv, groups, Q) for kv in range(kvh)]
    scores = []
    for kv, sl in enumerate(heads):
        q = jnp.concatenate([q_ref[:, (kv * groups + g) * hd:(kv * groups + g + 1) * hd]
                             for g in range(groups)], axis=0)
        scores.append([jnp.where(mask_prev, _dot_nt(q, kp_ref[:, sl]) * scale, -jnp.inf),
                       _dot_nt(q, kc_ref[:, sl]) * scale,
                       jnp.where(mask_next, _dot_nt(q, kn_ref[:, sl]) * scale, -jnp.inf),
                       _dot_nt(q, ck_ref[0, 0, :, sl].astype(BF16)) * scale])
    probs = [_softmax_probs(scores[kv], sinks[kv]) for kv in range(kvh)]
    for kv, sl in enumerate(heads):
        p, mx = probs[kv]
        o = _weighted_values(p, mx, [vp_ref[:, sl], vc_ref[:, sl], vn_ref[:, sl],
                                     cv_ref[0, 0, :, sl].astype(BF16)], sinks[kv])
        for g in range(groups):
            o_ref[:, (kv * groups + g) * hd:(kv * groups + g + 1) * hd] = o[g * Q:(g + 1) * Q].astype(BF16)


def _lat_attention(sink, z, cache_k, cache_v, *, t_ctx, dec_batch, dec_seq, kvh, groups, hd, kcol, vcol):
    assert WINDOW == Q_BLOCK
    Q = Q_BLOCK
    nb = dec_seq // Q
    base = t_ctx // Q
    qw = kvh * groups * hd
    kw = kvh * hd
    past = cache_k.shape[2]

    def kvspec(col, shift):
        return pl.BlockSpec((Q, kw), lambda b, j: (base + b * nb + jnp.clip(j + shift, 0, nb - 1), col))

    cspec = pl.BlockSpec((1, 1, past, kw), lambda b, j: (b, 0, 0, 0))
    return pl.pallas_call(
        functools.partial(_lat_attn_kernel, kvh=kvh, groups=groups, hd=hd),
        out_shape=jax.ShapeDtypeStruct((dec_batch * dec_seq, qw), BF16),
        grid=(dec_batch, nb),
        in_specs=[pl.BlockSpec(memory_space=pltpu.SMEM),
                  pl.BlockSpec((Q, qw), lambda b, j: (base + b * nb + j, 0)),
                  kvspec(kcol, -1), kvspec(kcol, 0), kvspec(kcol, 1),
                  kvspec(vcol, -1), kvspec(vcol, 0), kvspec(vcol, 1),
                  cspec, cspec],
        out_specs=pl.BlockSpec((Q, qw), lambda b, j: (b * nb + j, 0)),
        compiler_params=_params(("arbitrary", "arbitrary")),
        name="lat_attention",
    )(sink, z, z, z, z, z, z, z, cache_k, cache_v)


def _outproj_kernel(hf_ref, hb_ref, om_ref, hac_ref, hal_ref, gm_ref, ga_ref, xp_ref, xs_ref, mod_ref,
                    mn_ref, n2_ref, wm_ref, wa_ref, wo_ref, rw_ref,
                    x1_ref, h2_ref, h2t_ref, lg_ref, *, nctx_tiles, mh, dv):
    i = pl.program_id(0)
    is_ctx = i < nctx_tiles
    tm = x1_ref.shape[0]
    for r0 in range(0, tm, tm // OUTPROJ_SPLIT):
        rs = pl.ds(r0, tm // OUTPROJ_SPLIT)
        hm = hf_ref[rs, :].astype(F32) + hb_ref[rs, :].astype(F32)
        parts = []
        for h in range(mh):
            sl = hm[:, h * dv:(h + 1) * dv]
            parts.append(sl * lax.rsqrt(jnp.mean(sl * sl, axis=-1, keepdims=True) + EPS))
        hmn = jnp.concatenate(parts, axis=1) * mn_ref[...] * jax.nn.sigmoid(om_ref[rs, :].astype(F32))
        ha = jnp.where(is_ctx, hac_ref[rs, :], hal_ref[rs, :])
        y = (jax.nn.sigmoid(gm_ref[rs, :].astype(F32)) * _dot(hmn.astype(BF16), wm_ref[...])
             + jax.nn.sigmoid(ga_ref[rs, :].astype(F32)) * _dot(ha, wa_ref[...]))
        x = jnp.where(is_ctx, xp_ref[rs, :], xs_ref[rs, :])
        x1 = x + mod_ref[0, 2:3, :] * _dot(y.astype(BF16), wo_ref[...])
        x1_ref[rs, :] = x1
        n = x1 * lax.rsqrt(jnp.mean(x1 * x1, axis=-1, keepdims=True) + EPS) * n2_ref[...]
        h2 = n * (1.0 + mod_ref[0, 4:5, :]) + mod_ref[0, 3:4, :]
        h2p = _pack_bf16_pair(h2)
        h2_ref[rs, :] = h2p
        _rows_to_tiles(h2t_ref.at[rs], h2p)
        lg_ref[:, rs] = lax.dot_general(rw_ref[...], h2, _NT, precision=lax.Precision.HIGHEST,
                                        preferred_element_type=F32)


def _outproj(hf, hb, z, ha_ctx, ha_lat, x_prompt2, x_sample2, mods, mnorm, n2, wm, wa, wo, rw_t,
             *, t_ctx, dec_seq, mh, dv, omcol, gmcol, gacol):
    t_all = hf.shape[0]
    D = x_prompt2.shape[1]
    mw = mh * dv
    qw = ha_ctx.shape[1]
    E = rw_t.shape[0]
    tm = min(TM_OUTPROJ, t_ctx, dec_seq)
    nctx = t_ctx // tm
    per_seq = dec_seq // tm

    def ctx_blk(i):
        return (jnp.minimum(i, nctx - 1), 0)

    def lat_blk(i):
        return (jnp.maximum(i - nctx, 0), 0)

    def mod_row(i):
        return (jnp.where(i < nctx, 0, 1 + (i - nctx) // per_seq), 0, 0)

    const = lambda i: (0, 0)
    single = pl.Buffered(1)
    return pl.pallas_call(
        functools.partial(_outproj_kernel, nctx_tiles=nctx, mh=mh, dv=dv),
        out_shape=(jax.ShapeDtypeStruct((t_all, D), F32),
                   jax.ShapeDtypeStruct((t_all, D // 2), jnp.uint32),
                   jax.ShapeDtypeStruct((t_all, D // 2 // LANES_V7X, LANES_V7X), jnp.uint32),
                   jax.ShapeDtypeStruct((E, t_all), F32)),
        grid=(t_all // tm,),
        in_specs=[pl.BlockSpec((tm, mw), lambda i: (i, 0)),
                  pl.BlockSpec((tm, mw), lambda i: (i, 0)),
                  pl.BlockSpec((tm, mw), lambda i: (i, omcol)),
                  pl.BlockSpec((tm, qw), ctx_blk),
                  pl.BlockSpec((tm, qw), lat_blk),
                  pl.BlockSpec((tm, D), lambda i: (i, gmcol)),
                  pl.BlockSpec((tm, D), lambda i: (i, gacol)),
                  pl.BlockSpec((tm, D), ctx_blk),
                  pl.BlockSpec((tm, D), lat_blk),
                  pl.BlockSpec((1, N_MOD, D), mod_row),
                  pl.BlockSpec((1, mw), const),
                  pl.BlockSpec((1, D), const),
                  pl.BlockSpec((mw, D), const, pipeline_mode=single),
                  pl.BlockSpec((qw, D), const, pipeline_mode=single),
                  pl.BlockSpec((D, D), const, pipeline_mode=single),
                  pl.BlockSpec((E, D), const, pipeline_mode=single)],
        out_specs=(pl.BlockSpec((tm, D), lambda i: (i, 0)),
                   pl.BlockSpec((tm, D // 2), lambda i: (i, 0)),
                   pl.BlockSpec((tm, D // 2 // LANES_V7X, LANES_V7X), lambda i: (i, 0, 0)),
                   pl.BlockSpec((E, tm), lambda i: (0, i))),
        compiler_params=_params(("arbitrary",)),
        name="outproj",
    )(hf, hb, z, ha_ctx, ha_lat, z, z, x_prompt2, x_sample2, mods, mnorm, n2, wm, wa, wo, rw_t)


def _dispatch_kernel(pstart_ref, pcount_ref, nact_ref, pos_ref, h2t_ref, h2p_ref, x1_ref, mod_ref,
                     w1_ref, w3_ref, w2_ref, xs_hbm, x1s_ref, zblk, sem, psem, *, top_k):
    i = pl.program_id(0)
    td = h2t_ref.shape[0]
    rows = zblk.shape[0]

    @pl.when(i == 0)
    def _():
        zblk[...] = jnp.zeros_like(zblk)
        zrow = zblk.at[0]

        def per_expert(e, total):
            def fill(r, carry):
                pltpu.make_async_copy(zrow, xs_hbm.at[pstart_ref[e] + r], psem).start()
                return carry

            lax.fori_loop(0, pcount_ref[e], fill, 0)
            return total + pcount_ref[e]

        total = lax.fori_loop(0, pstart_ref.shape[0], per_expert, 0)

        def drain(j, carry):
            pltpu.make_async_copy(zrow, xs_hbm.at[0], psem).wait()
            return carry

        lax.fori_loop(0, total, drain, 0)

        def empty_block(b, carry):
            fill = pltpu.make_async_copy(zblk, xs_hbm.at[pl.ds(b * rows, rows)], psem)
            fill.start()
            fill.wait()
            return carry

        lax.fori_loop(nact_ref[0], xs_hbm.shape[0] // rows, empty_block, 0)

    F = w1_ref.shape[1]
    edges = [F * k // top_k // LANES_V7X * LANES_V7X for k in range(top_k)] + [F]
    x = _unpack_bf16_pair(h2p_ref[...])
    hmid = []
    for k in range(top_k):
        for t in range(td):
            pltpu.make_async_copy(h2t_ref.at[t], xs_hbm.at[pos_ref[0, k, t]], sem).start()
        if edges[k + 1] > edges[k]:
            cols = pl.ds(edges[k], edges[k + 1] - edges[k])
            hmid.append((jax.nn.silu(_dot(x, w1_ref[:, cols])) * _dot(x, w3_ref[:, cols])).astype(BF16))
    x1s_ref[...] = x1_ref[...] + mod_ref[0, 5:6, :] * _dot(jnp.concatenate(hmid, axis=1), w2_ref[...])

    for k in range(top_k):
        pltpu.make_async_copy(h2t_ref, xs_hbm.at[pl.ds(0, td)], sem).wait()


def _dispatch_rows(pad_start, pad_count, nact, pos3, h2t, h2p, x1, mods, sw1, sw3, sw2, *, n_slots, rows,
                   t_ctx, dec_seq):
    nt, K, td = pos3.shape
    T, c, _ = h2t.shape
    D = x1.shape[1]
    F = sw1.shape[1]
    nctx = t_ctx // td
    per_seq = dec_seq // td
    const = lambda i, *_: (0, 0)
    single = pl.Buffered(1)
    grid_spec = pltpu.PrefetchScalarGridSpec(
        num_scalar_prefetch=3,
        grid=(nt,),
        in_specs=[pl.BlockSpec((1, K, td), lambda i, *_: (i, 0, 0), memory_space=pltpu.SMEM),
                  pl.BlockSpec((td, c, LANES_V7X), lambda i, *_: (i, 0, 0)),
                  pl.BlockSpec((td, D // 2), lambda i, *_: (i, 0)),
                  pl.BlockSpec((td, D), lambda i, *_: (i, 0)),
                  pl.BlockSpec((1, N_MOD, D),
                               lambda i, *_: (jnp.where(i < nctx, 0, 1 + (i - nctx) // per_seq), 0, 0)),
                  pl.BlockSpec((D, F), const, pipeline_mode=single),
                  pl.BlockSpec((D, F), const, pipeline_mode=single),
                  pl.BlockSpec((F, D), const, pipeline_mode=single)],
        out_specs=(pl.BlockSpec(memory_space=pl.ANY),
                   pl.BlockSpec((td, D), lambda i, *_: (i, 0))),
        scratch_shapes=[pltpu.VMEM((rows, c, LANES_V7X), jnp.uint32), pltpu.SemaphoreType.DMA(()),
                        pltpu.SemaphoreType.DMA(())],
    )
    return pl.pallas_call(
        functools.partial(_dispatch_kernel, top_k=K),
        out_shape=(jax.ShapeDtypeStruct((n_slots, c, LANES_V7X), jnp.uint32),
                   jax.ShapeDtypeStruct((T, D), F32)),
        grid_spec=grid_spec,
        compiler_params=_params(("arbitrary",)),
        name="dispatch_shared",
    )(pad_start, pad_count, nact, pos3, h2t, h2p, x1, mods, sw1, sw3, sw2)


def _moe_kernel(blk_e, nact_ref, w1_hbm, w3_hbm, w2_hbm, xs_hbm, y_ref,
                xbuf0, xbuf1, w1b, w3b, w2b, stg_in, stg_out, sem, wsem, *, rows):
    i = pl.program_id(0)
    nact = nact_ref[0]
    c = xs_hbm.shape[1]
    e = blk_e[i]

    def fetch(b, buf, s):
        for a in range(c):
            pltpu.make_async_copy(xs_hbm.at[pl.ds(b * rows, rows), a, :],
                                  buf.at[:, pl.ds(a * LANES_V7X, LANES_V7X)], s).start()

    def fetch_wait(buf, s):
        for a in range(c):
            pltpu.make_async_copy(xs_hbm.at[pl.ds(0, rows), a, :],
                                  buf.at[:, pl.ds(a * LANES_V7X, LANES_V7X)], s).wait()

    @pl.when(i == 0)
    def _():
        fetch(0, xbuf0, sem.at[0])

    @pl.when(jnp.logical_and(i < nact, jnp.logical_or(i == 0, e != blk_e[jnp.maximum(i - 1, 0)])))
    def _():
        rc_in, rc_out = stg_in.shape[1], stg_out.shape[1]
        plan = ([(w1_hbm, w1b, stg_in, r, rc_in) for r in range(0, w1b.shape[0], rc_in)]
                + [(w3_hbm, w3b, stg_in, r, rc_in) for r in range(0, w3b.shape[0], rc_in)]
                + [(w2_hbm, w2b, stg_out, r, rc_out) for r in range(0, w2b.shape[0], rc_out)])

        depth = stg_in.shape[0]

        def chunk_copy(n):
            src, _, stg, r, rc = plan[n]
            return pltpu.make_async_copy(src.at[e, pl.ds(r, rc), :], stg.at[n % depth], wsem.at[n % depth])

        for n in range(depth - 1):
            chunk_copy(n).start()
        for n, (_, dst, stg, r, rc) in enumerate(plan):
            if n + depth - 1 < len(plan):
                chunk_copy(n + depth - 1).start()
            chunk_copy(n).wait()
            dst[pl.ds(r, rc), :] = stg[n % depth].astype(BF16)

    def block(cur, sem_cur, nxt, sem_nxt):
        fetch_wait(cur, sem_cur)

        @pl.when(i + 1 < nact)
        def _():
            fetch(i + 1, nxt, sem_nxt)

        x = _unpack_bf16_pair(cur[...])
        hmid = (jax.nn.silu(_dot(x, w1b[...])) * _dot(x, w3b[...])).astype(BF16)
        y_ref[...] = _pack_bf16_pair(_dot(hmid, w2b[...]))

    @pl.when(jnp.logical_and(i < nact, i % 2 == 0))
    def _():
        block(xbuf0, sem.at[0], xbuf1, sem.at[1])

    @pl.when(jnp.logical_and(i < nact, i % 2 == 1))
    def _():
        block(xbuf1, sem.at[1], xbuf0, sem.at[0])

    @pl.when(i >= nact)
    def _():
        y_ref[...] = jnp.zeros_like(y_ref)


def _moe(blk_e, nact, xs, w1, w3, w2, *, rows):
    nblk = blk_e.shape[0]
    E, D, F = w1.shape
    hbm = pl.BlockSpec(memory_space=pl.ANY)
    grid_spec = pltpu.PrefetchScalarGridSpec(
        num_scalar_prefetch=2,
        grid=(nblk,),
        in_specs=[hbm, hbm, hbm, hbm],
        out_specs=pl.BlockSpec((rows, D // 2), lambda i, be, na: (i, 0)),
        scratch_shapes=[pltpu.VMEM((rows, D // 2), jnp.uint32), pltpu.VMEM((rows, D // 2), jnp.uint32),
                        pltpu.VMEM((D, F), BF16), pltpu.VMEM((D, F), BF16), pltpu.VMEM((F, D), BF16),
                        pltpu.VMEM((WEIGHT_RING, D // WEIGHT_CHUNKS, F), F32),
                        pltpu.VMEM((WEIGHT_RING, F // WEIGHT_CHUNKS, D), F32),
                        pltpu.SemaphoreType.DMA((2,)), pltpu.SemaphoreType.DMA((WEIGHT_RING,))],
    )
    return pl.pallas_call(
        functools.partial(_moe_kernel, rows=rows),
        out_shape=jax.ShapeDtypeStruct((nblk * rows, D // 2), jnp.uint32),
        grid_spec=grid_spec,
        compiler_params=_params(("arbitrary",)),
        name="routed_experts",
    )(blk_e, nact, w1, w3, w2, xs)


def _router_kernel(lg_ref, rb_ref, eidx_ref, rank_ref, wtok_ref, cnt_ref, carry, *, top_k, n_groups, topk_groups):
    i = pl.program_id(0)

    @pl.when(i == 0)
    def _():
        carry[...] = jnp.zeros_like(carry)

    E, tr = lg_ref.shape
    gs = E // n_groups
    scores = jax.nn.sigmoid(lg_ref[...])
    biased = scores + rb_ref[...]
    b3 = biased.reshape(n_groups, gs, tr)
    io3 = lax.broadcasted_iota(jnp.int32, b3.shape, 1)
    m1 = jnp.max(b3, axis=1, keepdims=True)
    i1 = jnp.min(jnp.where(b3 == m1, io3, gs), axis=1, keepdims=True)
    m2 = jnp.max(jnp.where(io3 == i1, -jnp.inf, b3), axis=1, keepdims=True)
    grp = (m1 + m2).reshape(n_groups, tr)
    iog = lax.broadcasted_iota(jnp.int32, grp.shape, 0)
    sel = jnp.zeros(grp.shape, jnp.bool_)
    for _ in range(topk_groups):
        mx = jnp.max(grp, axis=0, keepdims=True)
        hit = iog == jnp.min(jnp.where(grp == mx, iog, n_groups), axis=0, keepdims=True)
        sel = jnp.logical_or(sel, hit)
        grp = jnp.where(hit, -jnp.inf, grp)
    masked = jnp.where(sel.reshape(n_groups, 1, tr), b3, -jnp.inf).reshape(E, tr)
    ioe = lax.broadcasted_iota(jnp.int32, (E, tr), 0)
    onehot = jnp.zeros((E, tr), F32)
    hits, idxs, ws = [], [], []
    for _ in range(top_k):
        mx = jnp.max(masked, axis=0, keepdims=True)
        ix = jnp.min(jnp.where(masked == mx, ioe, E), axis=0, keepdims=True)
        hit = ioe == ix
        hits.append(hit)
        idxs.append(ix)
        ws.append(jnp.sum(jnp.where(hit, scores, 0.0), axis=0, keepdims=True))
        onehot = onehot + hit.astype(F32)
        masked = jnp.where(hit, -jnp.inf, masked)
    wsum = ws[0]
    for w in ws[1:]:
        wsum = wsum + w
    ri = lax.broadcasted_iota(jnp.int32, (tr, tr), 0)
    ci = lax.broadcasted_iota(jnp.int32, (tr, tr), 1)
    before = _dot(onehot.astype(BF16), (ri < ci).astype(BF16)) + carry[...]
    ranks = [jnp.sum(jnp.where(hit, before, 0.0), axis=0, keepdims=True) for hit in hits]
    carry[...] = carry[...] + jnp.sum(onehot, axis=1, keepdims=True)
    cnt_ref[...] = carry[...].astype(jnp.int32)
    eidx_ref[...] = jnp.concatenate(idxs, axis=0)
    rank_ref[...] = jnp.concatenate(ranks, axis=0).astype(jnp.int32)
    wrows = jnp.concatenate([w / wsum * ROUTED_SCALE for w in ws]
                            + [jnp.zeros((LANES_V7X - top_k, tr), F32)], axis=0)
    wtok_ref[...] = wrows.T


def _router(logits_t, router_b):
    E, T = logits_t.shape
    tr = min(TR_ROUTER, T)
    return pl.pallas_call(
        functools.partial(_router_kernel, top_k=TOP_K, n_groups=N_GROUPS, topk_groups=TOPK_GROUPS),
        out_shape=(jax.ShapeDtypeStruct((TOP_K, T), jnp.int32),
                   jax.ShapeDtypeStruct((TOP_K, T), jnp.int32),
                   jax.ShapeDtypeStruct((T, LANES_V7X), F32),
                   jax.ShapeDtypeStruct((E, 1), jnp.int32)),
        grid=(T // tr,),
        in_specs=[pl.BlockSpec((E, tr), lambda i: (0, i)),
                  pl.BlockSpec((E, 1), lambda i: (0, 0))],
        out_specs=(pl.BlockSpec((TOP_K, tr), lambda i: (0, i)),
                   pl.BlockSpec((TOP_K, tr), lambda i: (0, i)),
                   pl.BlockSpec((tr, LANES_V7X), lambda i: (i, 0)),
                   pl.BlockSpec((E, 1), lambda i: (0, 0))),
        scratch_shapes=[pltpu.VMEM((E, 1), F32)],
        compiler_params=_params(("arbitrary",)),
        name="router",
    )(logits_t, router_b.reshape(E, 1))


def _slot_pos_kernel(start_ref, eidx_ref, rank_ref, pos_ref, *, n_experts):
    eidx = eidx_ref[...]
    pos = rank_ref[...]
    for e in range(n_experts):
        pos = pos + jnp.where(eidx == e, start_ref[e], 0)
    pos_ref[...] = pos


def _slot_pos(start_pad, eidx, rank):
    K, T = eidx.shape
    tl = min(TL_SLOTPOS, T)
    return pl.pallas_call(
        functools.partial(_slot_pos_kernel, n_experts=start_pad.shape[0]),
        out_shape=jax.ShapeDtypeStruct((K, T), jnp.int32),
        grid=(T // tl,),
        in_specs=[pl.BlockSpec(memory_space=pltpu.SMEM),
                  pl.BlockSpec((K, tl), lambda i: (0, i)),
                  pl.BlockSpec((K, tl), lambda i: (0, i))],
        out_specs=pl.BlockSpec((K, tl), lambda i: (0, i)),
        compiler_params=_params(("arbitrary",)),
        name="slot_pos",
    )(start_pad, eidx, rank)


def _dispatch(eidx, rank, counts, rows):
    K, T = eidx.shape
    A = K * T
    E = counts.shape[0]
    padded = (counts + rows - 1) // rows * rows
    end_pad = jnp.cumsum(padded)
    start_pad = end_pad - padded
    nblk = -(-(A + E * (rows - 1)) // rows)
    n_slots = nblk * rows
    nact = end_pad[-1] // rows
    blk = jnp.arange(nblk, dtype=jnp.int32)
    blk_e = jnp.sum(end_pad[None, :] <= (blk * rows)[:, None], axis=1).astype(jnp.int32)
    last_e = jnp.sum(end_pad <= (nact - 1) * rows).astype(jnp.int32)
    blk_e = jnp.where(blk < nact, blk_e, last_e)
    pos = _slot_pos(start_pad.astype(jnp.int32), eidx, rank)
    pad_start = (start_pad + counts).astype(jnp.int32)
    pad_count = (padded - counts).astype(jnp.int32)
    return pos, blk_e, nact.astype(jnp.int32).reshape(1), pad_start, pad_count, n_slots


def _tile_major(pos, tile):
    K, T = pos.shape
    return pos.reshape(K, T // tile, tile).transpose(1, 0, 2)


def _final_kernel(pos_ref, posn_ref, x1s_ref, wt_ref, mod_ref, fn_ref, y_hbm,
                  oc_ref, ol_ref, ybuf0, ybuf1, sem, *, top_k, nctx_tiles):
    i = pl.program_id(0)
    nt = pl.num_programs(0)
    tm = x1s_ref.shape[0]

    def gather(idx_ref, buf, s):
        for k in range(top_k):
            for t in range(tm):
                pltpu.make_async_copy(y_hbm.at[pl.ds(idx_ref[0, k, t], 1)], buf.at[k, pl.ds(t, 1)], s).start()

    def gather_wait(buf, s):
        for k in range(top_k):
            pltpu.make_async_copy(y_hbm.at[pl.ds(0, tm)], buf.at[k], s).wait()

    @pl.when(i == 0)
    def _():
        gather(pos_ref, ybuf0, sem.at[0])

    def tile(cur, sem_cur, nxt, sem_nxt):
        @pl.when(i + 1 < nt)
        def _():
            gather(posn_ref, nxt, sem_nxt)

        gather_wait(cur, sem_cur)
        wt = wt_ref[...]
        lo = hi = None
        for k in range(top_k):
            w = cur[k]
            wk = wt[:, k:k + 1]
            lo_k = pltpu.bitcast(w << 16, F32) * wk
            hi_k = pltpu.bitcast(w & jnp.uint32(0xFFFF0000), F32) * wk
            lo = lo_k if lo is None else lo + lo_k
            hi = hi_k if hi is None else hi + hi_k
        x2 = x1s_ref[...] + mod_ref[0, 5:6, :] * jnp.concatenate([lo, hi], axis=1)
        out = x2 * lax.rsqrt(jnp.mean(x2 * x2, axis=-1, keepdims=True) + EPS) * fn_ref[...]

        @pl.when(i < nctx_tiles)
        def _():
            oc_ref[...] = out

        @pl.when(i >= nctx_tiles)
        def _():
            ol_ref[...] = out

    @pl.when(i % 2 == 0)
    def _():
        tile(ybuf0, sem.at[0], ybuf1, sem.at[1])

    @pl.when(i % 2 == 1)
    def _():
        tile(ybuf1, sem.at[1], ybuf0, sem.at[0])


def _final(pos3, x1s, wtok, ys, mods, fnorm, *, t_ctx, dec_seq):
    nt, K, tm = pos3.shape
    t_all, D = x1s.shape
    nctx = t_ctx // tm
    per_seq = dec_seq // tm

    def mod_row(i):
        return (jnp.where(i < nctx, 0, 1 + (i - nctx) // per_seq), 0, 0)

    const = lambda i: (0, 0)
    smem_blk = lambda f: pl.BlockSpec((1, K, tm), f, memory_space=pltpu.SMEM)
    return pl.pallas_call(
        functools.partial(_final_kernel, top_k=K, nctx_tiles=nctx),
        out_shape=(jax.ShapeDtypeStruct((t_ctx, D), F32), jax.ShapeDtypeStruct((t_all - t_ctx, D), F32)),
        grid=(nt,),
        in_specs=[smem_blk(lambda i: (i, 0, 0)),
                  smem_blk(lambda i: (jnp.minimum(i + 1, nt - 1), 0, 0)),
                  pl.BlockSpec((tm, D), lambda i: (i, 0)),
                  pl.BlockSpec((tm, LANES_V7X), lambda i: (i, 0)),
                  pl.BlockSpec((1, N_MOD, D), mod_row),
                  pl.BlockSpec((1, D), const),
                  pl.BlockSpec(memory_space=pl.ANY)],
        out_specs=(pl.BlockSpec((tm, D), lambda i: (jnp.minimum(i, nctx - 1), 0)),
                   pl.BlockSpec((tm, D), lambda i: (jnp.maximum(i - nctx, 0), 0))),
        scratch_shapes=[pltpu.VMEM((K, tm, D // 2), jnp.uint32), pltpu.VMEM((K, tm, D // 2), jnp.uint32),
                        pltpu.SemaphoreType.DMA((2,))],
        compiler_params=_params(("arbitrary",)),
        name="combine_final",
    )(pos3, pos3, x1s, wtok, mods, fnorm, ys)


def _rope_tables(dec_seq, hd):
    nf = hd // 4
    t = jnp.arange(dec_seq)
    inv = ROPE_BASE ** (-jnp.arange(nf, dtype=F32) / nf)
    ang_r = (t // GRID_W).astype(F32)[:, None] * inv
    ang_c = (t % GRID_W).astype(F32)[:, None] * inv
    cos = jnp.concatenate([jnp.cos(ang_r)] * 2 + [jnp.cos(ang_c)] * 2, axis=1)
    sin = jnp.concatenate([-jnp.sin(ang_r), jnp.sin(ang_r), -jnp.sin(ang_c), jnp.sin(ang_c)], axis=1)
    return cos, sin


def _scan_steps(batch, seq, dec_batch, dec_seq, L):
    fb, bb, sq, fi, la = [], [], [], [], []
    base = 0
    for sid, S in enumerate([seq] * batch + [dec_seq] * dec_batch):
        nc = S // L
        for c in range(nc):
            fb.append(base + c)
            bb.append(base + nc - 1 - c)
            sq.append(sid)
            fi.append(int(c == 0))
            la.append(int(c == nc - 1))
        base += nc
    return tuple(jnp.asarray(np.asarray(a, dtype=np.int32)) for a in (fb, bb, sq, fi, la))


def kernel(x_prompt, x_sample, cache_k, cache_v, state_mlstm_C, state_mlstm_n, state_mlstm_m, c, c_ctx,
           w_mod, b_mod, norm1_w, norm2_w, w_in, igate_b, fgate_b, mlstm_norm_w, attn_sink,
           w_branch_m, w_branch_a, w_out, router_w, router_b, expert_w1, expert_w3, expert_w2,
           shared_w1, shared_w3, shared_w2, final_norm_w):
    batch, seq, D = x_prompt.shape
    dec_batch, dec_seq, _ = x_sample.shape
    depth = w_in.shape[0]
    assert depth == 1, "single trunk layer"
    _, _, past, kvh, hd = cache_k.shape
    mh, dk, dv = state_mlstm_C.shape[3:]
    ah = attn_sink.shape[1]
    groups = ah // kvh
    E = router_w.shape[2]
    assert dk == dv == hd == LANES_V7X
    t_ctx, t_lat = batch * seq, dec_batch * dec_seq
    mw, qw, kw = mh * dk, ah * hd, kvh * hd
    ng = 4 * mh

    wi = w_in[0]
    o = 0
    seg = {}
    for name, width in (("qm", mw), ("km", mw), ("vm", mw), ("om", mw), ("im", 2 * mh), ("fm", 2 * mh),
                        ("qa", qw), ("ka", kw), ("va", kw), ("gm", D), ("ga", D)):
        seg[name] = wi[:, o:o + width]
        o += width
    order = ("qa", "gm", "ga", "qm", "km", "vm", "om", "ka", "va")
    w_main = jnp.concatenate([seg[nm] for nm in order], axis=1).astype(BF16)
    col = {}
    o = 0
    for nm in order:
        col[nm] = o
        o += seg[nm].shape[1]
    tn = 2 * kw
    for nm in order[:-2]:
        assert col[nm] % tn == 0 and seg[nm].shape[1] % tn == 0
    for nm, width in (("gm", D), ("ga", D), ("qm", mw), ("km", mw), ("vm", mw), ("om", mw), ("ka", kw), ("va", kw)):
        assert col[nm] % width == 0
    w_gate = jnp.pad(jnp.concatenate([seg["im"], seg["fm"]], axis=1), ((0, 0), (0, LANES_V7X - ng))).astype(BF16)

    R = -(-(1 + dec_batch) // 8) * 8
    cond = jnp.concatenate([c_ctx[None, :], c, jnp.zeros((R - 1 - dec_batch, D), F32)], axis=0)
    mods = _modulation(cond, w_mod[0], b_mod[0]).reshape(R, N_MOD, D)

    xp2 = x_prompt.reshape(t_ctx, D)
    xs2 = x_sample.reshape(t_lat, D)
    cos, sin = _rope_tables(dec_seq, hd)
    h1, gates, gates_t = _prenorm(xp2, xs2, mods, norm1_w, w_gate, t_ctx=t_ctx, dec_seq=dec_seq, n_gates=ng)
    z, kv32 = _inproj(h1, w_main, cos, sin, t_ctx=t_ctx, dec_seq=dec_seq, tn=tn, n_rope_tiles=qw // tn)

    nseq = batch + dec_batch
    C0 = jnp.concatenate([jnp.zeros((batch, 2, mh, dk, dv), F32), state_mlstm_C[:, 0]], axis=0)
    n0 = jnp.concatenate([jnp.zeros((batch, 2, mh, dk), F32), state_mlstm_n[:, 0]], axis=0)
    m0 = jnp.concatenate([jnp.zeros((batch, 2, mh), F32), state_mlstm_m[:, 0]], axis=0)
    s0 = jnp.concatenate([C0, jnp.broadcast_to(n0[..., None], (nseq, 2, mh, dk, dv))], axis=-1)
    s0 = s0.reshape(nseq, 2 * mh, dk, 2 * dv)
    m0 = jnp.broadcast_to(m0.reshape(nseq, 2 * mh, 1), (nseq, 2 * mh, LANES_V7X))
    gate_b = jnp.concatenate([igate_b[0].reshape(-1), fgate_b[0].reshape(-1)])
    steps = _scan_steps(batch, seq, dec_batch, dec_seq, M_CHUNK)
    hf, hb, s_fin, m_fin = _mlstm(z, gates, gates_t, gate_b.reshape(1, ng), gate_b.reshape(ng, 1), s0, m0, steps,
                                  mh=mh, dk=dk, qcol=col["qm"] // mw, kcol=col["km"] // mw, vcol=col["vm"] // mw)

    sink = attn_sink[0]
    ha_ctx = _ctx_attention(sink, z, batch=batch, seq=seq, kvh=kvh, groups=groups, hd=hd,
                            kcol=col["ka"] // kw, vcol=col["va"] // kw)
    ha_lat = _lat_attention(sink, z, cache_k.reshape(dec_batch, depth, past, kw),
                            cache_v.reshape(dec_batch, depth, past, kw), t_ctx=t_ctx, dec_batch=dec_batch,
                            dec_seq=dec_seq, kvh=kvh, groups=groups, hd=hd, kcol=col["ka"] // kw,
                            vcol=col["va"] // kw)

    x1, h2p, h2t, logits_t = _outproj(hf, hb, z, ha_ctx, ha_lat, xp2, xs2, mods, mlstm_norm_w, norm2_w,
                                w_branch_m[0].astype(BF16), w_branch_a[0].astype(BF16), w_out[0].astype(BF16),
                                router_w[0].T, t_ctx=t_ctx, dec_seq=dec_seq, mh=mh, dv=dv,
                                omcol=col["om"] // mw, gmcol=col["gm"] // D, gacol=col["ga"] // D)

    eidx, rank, wtok, counts = _router(logits_t, router_b[0])
    pos, blk_e, nact, pad_start, pad_count, n_slots = _dispatch(eidx, rank, counts[:, 0], EXPERT_ROWS)
    xs, x1s = _dispatch_rows(pad_start, pad_count, nact, _tile_major(pos, min(TD_DISPATCH, t_ctx, dec_seq)),
                             h2t, h2p, x1, mods, shared_w1[0].astype(BF16), shared_w3[0].astype(BF16),
                             shared_w2[0].astype(BF16), n_slots=n_slots, rows=EXPERT_ROWS, t_ctx=t_ctx,
                             dec_seq=dec_seq)
    ys = _moe(blk_e, nact, xs, expert_w1[0], expert_w3[0], expert_w2[0], rows=EXPERT_ROWS)

    y_ctx, y_lat = _final(_tile_major(pos, min(TM_FINAL, t_ctx, dec_seq)), x1s, wtok, ys, mods,
                          final_norm_w.reshape(1, D), t_ctx=t_ctx, dec_seq=dec_seq)

    y_prompt = y_ctx.reshape(batch, seq, D)
    y_sample = y_lat.reshape(dec_batch, dec_seq, D)
    new_k = kv32[:t_ctx, :kw].reshape(batch, 1, seq, kvh, hd)
    new_v = kv32[:t_ctx, kw:].reshape(batch, 1, seq, kvh, hd)
    s_ctx = s_fin[:batch].reshape(batch, 1, 2, mh, dk, 2 * dv)
    new_C = s_ctx[..., :dv]
    new_n = s_ctx[..., dv]
    new_m = m_fin[:batch, :, 0].reshape(batch, 1, 2, mh)
    return y_prompt, y_sample, new_k, new_v, new_C, new_n, new_m
```

```python
import functools

import numpy as np
import jax
import jax.numpy as jnp
from jax import lax
from jax.experimental import pallas as pl
from jax.experimental.pallas import tpu as pltpu

TOP_K = 6
N_GROUPS = 8
TOPK_GROUPS = 4
ROUTED_SCALE = 2.5
WINDOW = 128
Q_BLOCK = 128
GRID_W = 64
ROPE_BASE = 10000.0
M_CHUNK = 128
N_MOD = 6
EPS = 1e-6

LANES_V7X = 128
MXU_COLS_V7X = 256
VMEM_LIMIT_V7X = 56 * 1024 * 1024

TM_PRENORM = 512
TM_INPROJ = 1024
TM_OUTPROJ = 256
OUTPROJ_SPLIT = 2
WEIGHT_CHUNK_ROWS = 128
WEIGHT_RING = 8
TM_FINAL = 256
TN_MOD = 1024
EXPERT_ROWS = 256
TR_ROUTER = 512
TL_SLOTPOS = 2048
TD_DISPATCH = 256

F32 = jnp.float32
BF16 = jnp.bfloat16
_NT = (((1,), (1,)), ((), ()))


def _params(sem):
    return pltpu.CompilerParams(dimension_semantics=sem, vmem_limit_bytes=VMEM_LIMIT_V7X)


def _dot(a, b):
    return jnp.dot(a, b, preferred_element_type=F32)


def _dot_nt(a, b):
    return lax.dot_general(a, b, _NT, preferred_element_type=F32)


def _pack_bf16_pair(x):
    c = x.shape[1] // 2
    lo = pltpu.bitcast(x[:, :c].astype(BF16).astype(F32), jnp.uint32)
    hi = pltpu.bitcast(x[:, c:].astype(BF16).astype(F32), jnp.uint32)
    return (lo >> 16) | (hi & jnp.uint32(0xFFFF0000))


def _rows_to_tiles(ref, x):
    for a in range(ref.shape[1]):
        ref[:, a, :] = x[:, a * LANES_V7X:(a + 1) * LANES_V7X]


def _unpack_bf16_pair(w):
    lo = pltpu.bitcast(w << 16, F32).astype(BF16)
    hi = pltpu.bitcast(w & jnp.uint32(0xFFFF0000), F32).astype(BF16)
    return jnp.concatenate([lo, hi], axis=1)


def _mod_kernel(c_ref, w_ref, b_ref, o_ref):
    s = jax.nn.silu(c_ref[...]).astype(BF16)
    o_ref[...] = _dot(s, w_ref[...].astype(BF16)) + b_ref[...]


def _modulation(cond, w_mod, b_mod):
    R, D = cond.shape
    N = w_mod.shape[1]
    tn = min(TN_MOD, N)
    return pl.pallas_call(
        _mod_kernel,
        out_shape=jax.ShapeDtypeStruct((R, N), F32),
        grid=(N // tn,),
        in_specs=[pl.BlockSpec((R, D), lambda n: (0, 0)),
                  pl.BlockSpec((D, tn), lambda n: (0, n)),
                  pl.BlockSpec((1, tn), lambda n: (0, n))],
        out_specs=pl.BlockSpec((R, tn), lambda n: (0, n)),
        compiler_params=_params(("arbitrary",)),
        name="modulation",
    )(cond, w_mod, b_mod.reshape(1, N))


def _rope_slice(x, cos, sin_signed, first_half):
    swap = jnp.where(first_half, pltpu.roll(x, 96, 1), pltpu.roll(x, 32, 1))
    return x * cos + swap * sin_signed


def _prenorm_kernel(xp_ref, xs_ref, mod_ref, n1_ref, wg_ref, h_ref, g_ref, gt_ref, *, nctx_tiles, n_gates):
    i = pl.program_id(0)
    x = jnp.where(i < nctx_tiles, xp_ref[...], xs_ref[...])
    y = x * lax.rsqrt(jnp.mean(x * x, axis=-1, keepdims=True) + EPS) * n1_ref[...]
    h = (y * (1.0 + mod_ref[0, 1:2, :]) + mod_ref[0, 0:1, :]).astype(BF16)
    h_ref[...] = h
    g = _dot(h, wg_ref[...])
    g_ref[...] = g[:, :n_gates]
    gt_ref[...] = g.T[:n_gates, :]


def _prenorm(x_prompt2, x_sample2, mods, n1, w_gate, *, t_ctx, dec_seq, n_gates):
    t_lat, D = x_sample2.shape
    t_all = t_ctx + t_lat
    tm = min(TM_PRENORM, t_ctx, dec_seq)
    nctx = t_ctx // tm
    per_seq = dec_seq // tm
    return pl.pallas_call(
        functools.partial(_prenorm_kernel, nctx_tiles=nctx, n_gates=n_gates),
        out_shape=(jax.ShapeDtypeStruct((t_all, D), BF16),
                   jax.ShapeDtypeStruct((t_all, n_gates), F32),
                   jax.ShapeDtypeStruct((n_gates, t_all), F32)),
        grid=(t_all // tm,),
        in_specs=[pl.BlockSpec((tm, D), lambda i: (jnp.minimum(i, nctx - 1), 0)),
                  pl.BlockSpec((tm, D), lambda i: (jnp.maximum(i - nctx, 0), 0)),
                  pl.BlockSpec((1, N_MOD, D), lambda i: (jnp.where(i < nctx, 0, 1 + (i - nctx) // per_seq), 0, 0)),
                  pl.BlockSpec((1, D), lambda i: (0, 0)),
                  pl.BlockSpec((D, LANES_V7X), lambda i: (0, 0))],
        out_specs=(pl.BlockSpec((tm, D), lambda i: (i, 0)),
                   pl.BlockSpec((tm, n_gates), lambda i: (i, 0)),
                   pl.BlockSpec((n_gates, tm), lambda i: (0, i))),
        compiler_params=_params(("arbitrary",)),
        name="prenorm",
    )(x_prompt2, x_sample2, mods, n1, w_gate)


def _inproj_kernel(h_ref, w_ref, cos_ref, sin_ref, z_ref, kv_ref, *, nctx_tiles, n_rope_tiles, kv_tile):
    i = pl.program_id(0)
    n = pl.program_id(1)
    is_ctx = i < nctx_tiles
    tn = z_ref.shape[1]
    acc = _dot(h_ref[...], w_ref[...])

    def rope_cols(ncols):
        cos = jnp.where(is_ctx, 1.0, cos_ref[...])
        sin = jnp.where(is_ctx, 0.0, sin_ref[...])
        lane = lax.broadcasted_iota(jnp.int32, cos.shape, 1)
        first_half = (lane % 64) < 32
        return [_rope_slice(acc[:, c:c + LANES_V7X], cos, sin, first_half)
                for c in range(0, ncols, LANES_V7X)]

    @pl.when(n < n_rope_tiles)
    def _():
        z_ref[...] = jnp.concatenate(rope_cols(tn), axis=1).astype(BF16)

    @pl.when(n == kv_tile)
    def _():
        r = jnp.concatenate(rope_cols(tn // 2) + [acc[:, tn // 2:]], axis=1)
        z_ref[...] = r.astype(BF16)
        kv_ref[...] = r

    @pl.when(jnp.logical_and(n >= n_rope_tiles, n != kv_tile))
    def _():
        z_ref[...] = acc.astype(BF16)


def _inproj(h, w_main, cos, sin, *, t_ctx, dec_seq, tn, n_rope_tiles):
    t_all, D = h.shape
    tm = min(TM_INPROJ, t_ctx, dec_seq)
    nctx = t_ctx // tm
    per_seq = dec_seq // tm
    ncols = w_main.shape[1]
    ntile = ncols // tn
    kv_tile = ntile - 1

    def pos_blk(i):
        return jnp.where(i < nctx, 0, (i - nctx) % per_seq)

    kernel = functools.partial(_inproj_kernel, nctx_tiles=nctx, n_rope_tiles=n_rope_tiles, kv_tile=kv_tile)
    return pl.pallas_call(
        kernel,
        out_shape=(jax.ShapeDtypeStruct((t_all, ncols), BF16),
                   jax.ShapeDtypeStruct((t_all, tn), F32)),
        grid=(t_all // tm, ntile),
        in_specs=[pl.BlockSpec((tm, D), lambda i, n: (i, 0)),
                  pl.BlockSpec((D, tn), lambda i, n: (0, n)),
                  pl.BlockSpec((tm, LANES_V7X), lambda i, n: (pos_blk(i), 0)),
                  pl.BlockSpec((tm, LANES_V7X), lambda i, n: (pos_blk(i), 0))],
        out_specs=(pl.BlockSpec((tm, tn), lambda i, n: (i, n)),
                   pl.BlockSpec((tm, tn), lambda i, n: (i, 0))),
        compiler_params=_params(("arbitrary", "arbitrary")),
        name="inproj",
    )(h, w_main, cos, sin)


def _mlstm_kernel(fb, bb, sq, fi, la,
                  qf, kf, vf, qb, kb, vb, gf, gb, gtf, gtb, brow, bcol, s0, m0,
                  hf, hb, s_out, m_out, s_scr, m_scr, *, mh, dk, chunk):
    s = pl.program_id(0)
    L = chunk
    assert L == dk
    scale = dk ** -0.5

    @pl.when(fi[s] == 1)
    def _():
        s_scr[...] = s0[0]
        m_scr[...] = m0[0]

    ri = lax.broadcasted_iota(jnp.int32, (L, L), 0)
    ci = lax.broadcasted_iota(jnp.int32, (L, L), 1)
    low = ri >= ci
    upp = ri <= ci
    low_f = low.astype(F32)
    upp_f = upp.astype(F32)
    ones_blk = jnp.ones((L, dk), BF16)
    hi = lax.Precision.HIGHEST
    refs = ((qf, kf, vf, gf, gtf, hf), (qb, kb, vb, gb, gtb, hb))
    units = [(dr, h) for dr in range(2) for h in range(mh)]
    sl = lambda h: slice(h * dk, (h + 1) * dk)

    gate = []
    for dr, (_, _, _, g_ref, gt_ref, _) in enumerate(refs):
        G = g_ref[...] + brow[...]
        GT = gt_ref[...] + bcol[...]
        ic_col = G[:, dr * mh:(dr + 1) * mh]
        lf_col = jax.nn.log_sigmoid(G[:, (2 + dr) * mh:(3 + dr) * mh])
        ic_row = GT[dr * mh:(dr + 1) * mh, :]
        lf_row = jax.nn.log_sigmoid(GT[(2 + dr) * mh:(3 + dr) * mh, :])
        b_col = jnp.dot(low_f if dr == 0 else upp_f, lf_col, precision=hi, preferred_element_type=F32)
        b_row = jnp.dot(lf_row, upp_f if dr == 0 else low_f, precision=hi, preferred_element_type=F32)
        gate.append((ic_col, ic_row, b_col, b_row))

    S_prev = [s_scr[dr * mh + h] for dr, h in units]
    m_prev = [m_scr[dr * mh + h:dr * mh + h + 1, 0:1] for dr, h in units]
    q = [refs[dr][0][:, sl(h)] for dr, h in units]
    k = [refs[dr][1][:, sl(h)] for dr, h in units]
    v_aug = [jnp.concatenate([refs[dr][2][:, sl(h)], ones_blk], axis=1) for dr, h in units]
    qk = [_dot_nt(q[u], k[u]) for u in range(len(units))]
    qs = [_dot(q[u], S_prev[u].astype(BF16)) for u in range(len(units))]

    sm, w_inter, floor, b_rep = [], [], [], []
    for u, (dr, h) in enumerate(units):
        ic_col, ic_row, b_col, b_row = gate[dr]
        bc = jnp.broadcast_to(b_col[:, h:h + 1], (L, L))
        d = jnp.where(low if dr == 0 else upp, bc - b_row[h:h + 1, :] + ic_row[h:h + 1, :], -jnp.inf)
        inter = bc + m_prev[u]
        m_t = jnp.maximum(inter, jnp.broadcast_to(jnp.max(d, axis=-1, keepdims=True), (L, L)))
        sm.append((qk[u] * scale * jnp.exp(d - m_t)).astype(BF16))
        w_inter.append(jnp.exp(inter - m_t))
        floor.append(jnp.exp(-m_t))
        b_rep.append(bc)

    sv = [_dot(sm[u], v_aug[u]) for u in range(len(units))]
    for u, (dr, h) in enumerate(units):
        num = sv[u][:, :dk] + w_inter[u] * qs[u][:, :dk]
        den = sv[u][:, dk:] + w_inter[u] * qs[u][:, dk:]
        refs[dr][5][:, sl(h)] = (num / jnp.maximum(jnp.abs(den), floor[u])).astype(BF16)

    kw_t, wc, m_new = [], [], []
    for u, (dr, h) in enumerate(units):
        ic_col = gate[dr][0]
        bc = b_rep[u]
        b_last = bc[L - 1:L, :] if dr == 0 else bc[0:1, :]
        g = b_last - bc + jnp.broadcast_to(ic_col[:, h:h + 1], (L, L))
        mn = jnp.maximum(b_last + m_prev[u], jnp.max(g, axis=0, keepdims=True))
        kw_t.append((k[u].astype(F32) * (jnp.exp(g - mn) * scale)).T.astype(BF16))
        wc.append(jnp.exp(b_last + m_prev[u] - mn))
        m_new.append(mn)

    upd = [_dot(kw_t[u], v_aug[u]) for u in range(len(units))]
    for u, (dr, h) in enumerate(units):
        r = dr * mh + h
        s_scr[r] = jnp.concatenate([wc[u], wc[u]], axis=1) * S_prev[u] + upd[u]
        m_scr[r:r + 1, :] = m_new[u]

    @pl.when(la[s] == 1)
    def _():
        s_out[0] = s_scr[...]
        m_out[0] = m_scr[...]


def _mlstm(z, gates, gates_t, brow, bcol, s0, m0, steps, *, mh, dk, qcol, kcol, vcol):
    t_all = z.shape[0]
    L = M_CHUNK
    mw = mh * dk
    ng = gates.shape[1]
    nseq = s0.shape[0]
    fb, bb, sq, fi, la = steps
    nsteps = fb.shape[0]

    def zspec(which, col):
        return pl.BlockSpec((L, mw), lambda s, fb, bb, sq, fi, la: ((fb, bb)[which][s], col))

    def gspec(which):
        return pl.BlockSpec((L, ng), lambda s, fb, bb, sq, fi, la: ((fb, bb)[which][s], 0))

    def gtspec(which):
        return pl.BlockSpec((ng, L), lambda s, fb, bb, sq, fi, la: (0, (fb, bb)[which][s]))

    grid_spec = pltpu.PrefetchScalarGridSpec(
        num_scalar_prefetch=5,
        grid=(nsteps,),
        in_specs=[zspec(0, qcol), zspec(0, kcol), zspec(0, vcol),
                  zspec(1, qcol), zspec(1, kcol), zspec(1, vcol),
                  gspec(0), gspec(1), gtspec(0), gtspec(1),
                  pl.BlockSpec((1, ng), lambda s, *_: (0, 0)),
                  pl.BlockSpec((ng, 1), lambda s, *_: (0, 0)),
                  pl.BlockSpec((1, 2 * mh, dk, 2 * dk), lambda s, fb, bb, sq, fi, la: (sq[s], 0, 0, 0)),
                  pl.BlockSpec((1, 2 * mh, LANES_V7X), lambda s, fb, bb, sq, fi, la: (sq[s], 0, 0))],
        out_specs=(pl.BlockSpec((L, mw), lambda s, fb, bb, sq, fi, la: (fb[s], 0)),
                   pl.BlockSpec((L, mw), lambda s, fb, bb, sq, fi, la: (bb[s], 0)),
                   pl.BlockSpec((1, 2 * mh, dk, 2 * dk), lambda s, fb, bb, sq, fi, la: (sq[s], 0, 0, 0)),
                   pl.BlockSpec((1, 2 * mh, LANES_V7X), lambda s, fb, bb, sq, fi, la: (sq[s], 0, 0))),
        scratch_shapes=[pltpu.VMEM((2 * mh, dk, 2 * dk), F32), pltpu.VMEM((2 * mh, LANES_V7X), F32)],
    )
    return pl.pallas_call(
        functools.partial(_mlstm_kernel, mh=mh, dk=dk, chunk=L),
        out_shape=(jax.ShapeDtypeStruct((t_all, mw), BF16),
                   jax.ShapeDtypeStruct((t_all, mw), BF16),
                   jax.ShapeDtypeStruct((nseq, 2 * mh, dk, 2 * dk), F32),
                   jax.ShapeDtypeStruct((nseq, 2 * mh, LANES_V7X), F32)),
        grid_spec=grid_spec,
        compiler_params=_params(("arbitrary",)),
        name="mlstm",
    )(fb, bb, sq, fi, la, z, z, z, z, z, z, gates, gates, gates_t, gates_t, brow, bcol, s0, m0)


def _sink_column(sink_ref, kv, groups, rows_per_group):
    shape = (groups * rows_per_group, LANES_V7X)
    row_g = lax.broadcasted_iota(jnp.int32, shape, 0) // rows_per_group
    col = jnp.full(shape, sink_ref[kv * groups], F32)
    for g in range(1, groups):
        col = jnp.where(row_g == g, sink_ref[kv * groups + g], col)
    return col


def _softmax_probs(scores, sink):
    hd = sink.shape[1]
    s = jnp.concatenate(scores, axis=1)
    mx = jnp.maximum(jnp.broadcast_to(jnp.max(s, axis=-1, keepdims=True), sink.shape), sink)
    return jnp.exp(s - jnp.concatenate([mx] * (s.shape[1] // hd), axis=1)).astype(BF16), mx


def _weighted_values(p, mx, values, sink):
    hd = sink.shape[1]
    acc = None
    off = 0
    for v in values:
        v_aug = jnp.concatenate([v, jnp.ones(v.shape, BF16)], axis=1)
        pv = _dot(p[:, off:off + v.shape[0]], v_aug)
        acc = pv if acc is None else acc + pv
        off += v.shape[0]
    return acc[:, :hd] / (acc[:, hd:] + jnp.exp(sink - mx))


def _ctx_attn_kernel(sink_ref, q_ref, k_ref, v_ref, o_ref, *, kvh, groups, hd):
    S = q_ref.shape[0]
    scale = hd ** -0.5
    for kv in range(kvh):
        k = k_ref[:, kv * hd:(kv + 1) * hd]
        v = v_ref[:, kv * hd:(kv + 1) * hd]
        q = jnp.concatenate([q_ref[:, (kv * groups + g) * hd:(kv * groups + g + 1) * hd]
                             for g in range(groups)], axis=0)
        sink = _sink_column(sink_ref, kv, groups, S)
        p, mx = _softmax_probs([_dot_nt(q, k) * scale], sink)
        o = _weighted_values(p, mx, [v], sink)
        for g in range(groups):
            o_ref[:, (kv * groups + g) * hd:(kv * groups + g + 1) * hd] = o[g * S:(g + 1) * S].astype(BF16)


def _ctx_attention(sink, z, *, batch, seq, kvh, groups, hd, kcol, vcol):
    qw = kvh * groups * hd
    kw = kvh * hd
    return pl.pallas_call(
        functools.partial(_ctx_attn_kernel, kvh=kvh, groups=groups, hd=hd),
        out_shape=jax.ShapeDtypeStruct((batch * seq, qw), BF16),
        grid=(batch,),
        in_specs=[pl.BlockSpec(memory_space=pltpu.SMEM),
                  pl.BlockSpec((seq, qw), lambda b: (b, 0)),
                  pl.BlockSpec((seq, kw), lambda b: (b, kcol)),
                  pl.BlockSpec((seq, kw), lambda b: (b, vcol))],
        out_specs=pl.BlockSpec((seq, qw), lambda b: (b, 0)),
        compiler_params=_params(("arbitrary",)),
        name="ctx_attention",
    )(sink, z, z, z)


def _lat_attn_kernel(sink_ref, q_ref, kp_ref, kc_ref, kn_ref, vp_ref, vc_ref, vn_ref, ck_ref, cv_ref, o_ref,
                     *, kvh, groups, hd):
    j = pl.program_id(1)
    nb = pl.num_programs(1)
    Q = q_ref.shape[0]
    scale = hd ** -0.5
    R = groups * Q
    rq = lax.broadcasted_iota(jnp.int32, (R, Q), 0) % Q
    cc = lax.broadcasted_iota(jnp.int32, (R, Q), 1)
    mask_prev = jnp.logical_and(cc >= rq, j > 0)
    mask_next = jnp.logical_and(cc <= rq, j < nb - 1)
    heads = [slice(kv * hd, (kv + 1) * hd) for kv in range(kvh)]
    sinks = [_sink_column(sink_ref, kv, groups, Q) for kv in range(kvh)]
    scores = []
    for kv, sl in enumerate(heads):
        q = jnp.concatenate([q_ref[:, (kv * groups + g) * hd:(kv * groups + g + 1) * hd]
                             for g in range(groups)], axis=0)
        scores.append([jnp.where(mask_prev, _dot_nt(q, kp_ref[:, sl]) * scale, -jnp.inf),
                       _dot_nt(q, kc_ref[:, sl]) * scale,
                       jnp.where(mask_next, _dot_nt(q, kn_ref[:, sl]) * scale, -jnp.inf),
                       _dot_nt(q, ck_ref[0, 0, :, sl].astype(BF16)) * scale])
    probs = [_softmax_probs(scores[kv], sinks[kv]) for kv in range(kvh)]
    for kv, sl in enumerate(heads):
        p, mx = probs[kv]
        o = _weighted_values(p, mx, [vp_ref[:, sl], vc_ref[:, sl], vn_ref[:, sl],
                                     cv_ref[0, 0, :, sl].astype(BF16)], sinks[kv])
        for g in range(groups):
            o_ref[:, (kv * groups + g) * hd:(kv * groups + g + 1) * hd] = o[g * Q:(g + 1) * Q].astype(BF16)


def _lat_attention(sink, z, cache_k, cache_v, *, t_ctx, dec_batch, dec_seq, kvh, groups, hd, kcol, vcol):
    assert WINDOW == Q_BLOCK
    Q = Q_BLOCK
    nb = dec_seq // Q
    base = t_ctx // Q
    qw = kvh * groups * hd
    kw = kvh * hd
    past = cache_k.shape[2]

    def kvspec(col, shift):
        return pl.BlockSpec((Q, kw), lambda b, j: (base + b * nb + jnp.clip(j + shift, 0, nb - 1), col))

    cspec = pl.BlockSpec((1, 1, past, kw), lambda b, j: (b, 0, 0, 0))
    return pl.pallas_call(
        functools.partial(_lat_attn_kernel, kvh=kvh, groups=groups, hd=hd),
        out_shape=jax.ShapeDtypeStruct((dec_batch * dec_seq, qw), BF16),
        grid=(dec_batch, nb),
        in_specs=[pl.BlockSpec(memory_space=pltpu.SMEM),
                  pl.BlockSpec((Q, qw), lambda b, j: (base + b * nb + j, 0)),
                  kvspec(kcol, -1), kvspec(kcol, 0), kvspec(kcol, 1),
                  kvspec(vcol, -1), kvspec(vcol, 0), kvspec(vcol, 1),
                  cspec, cspec],
        out_specs=pl.BlockSpec((Q, qw), lambda b, j: (b * nb + j, 0)),
        compiler_params=_params(("arbitrary", "arbitrary")),
        name="lat_attention",
    )(sink, z, z, z, z, z, z, z, cache_k, cache_v)


def _outproj_kernel(hf_ref, hb_ref, om_ref, hac_ref, hal_ref, gm_ref, ga_ref, xp_ref, xs_ref, mod_ref,
                    mn_ref, n2_ref, wm_ref, wa_ref, wo_ref, rw_ref,
                    x1_ref, h2_ref, h2t_ref, lg_ref, *, nctx_tiles, mh, dv):
    i = pl.program_id(0)
    is_ctx = i < nctx_tiles
    tm = x1_ref.shape[0]
    for r0 in range(0, tm, tm // OUTPROJ_SPLIT):
        rs = pl.ds(r0, tm // OUTPROJ_SPLIT)
        hm = hf_ref[rs, :].astype(F32) + hb_ref[rs, :].astype(F32)
        parts = []
        for h in range(mh):
            sl = hm[:, h * dv:(h + 1) * dv]
            parts.append(sl * lax.rsqrt(jnp.mean(sl * sl, axis=-1, keepdims=True) + EPS))
        hmn = jnp.concatenate(parts, axis=1) * mn_ref[...] * jax.nn.sigmoid(om_ref[rs, :].astype(F32))
        ha = jnp.where(is_ctx, hac_ref[rs, :], hal_ref[rs, :])
        y = (jax.nn.sigmoid(gm_ref[rs, :].astype(F32)) * _dot(hmn.astype(BF16), wm_ref[...])
             + jax.nn.sigmoid(ga_ref[rs, :].astype(F32)) * _dot(ha, wa_ref[...]))
        x = jnp.where(is_ctx, xp_ref[rs, :], xs_ref[rs, :])
        x1 = x + mod_ref[0, 2:3, :] * _dot(y.astype(BF16), wo_ref[...])
        x1_ref[rs, :] = x1
        n = x1 * lax.rsqrt(jnp.mean(x1 * x1, axis=-1, keepdims=True) + EPS) * n2_ref[...]
        h2 = n * (1.0 + mod_ref[0, 4:5, :]) + mod_ref[0, 3:4, :]
        h2p = _pack_bf16_pair(h2)
        h2_ref[rs, :] = h2p
        _rows_to_tiles(h2t_ref.at[rs], h2p)
        lg_ref[:, rs] = lax.dot_general(rw_ref[...], h2, _NT, precision=lax.Precision.HIGHEST,
                                        preferred_element_type=F32)


def _outproj(hf, hb, z, ha_ctx, ha_lat, x_prompt2, x_sample2, mods, mnorm, n2, wm, wa, wo, rw_t,
             *, t_ctx, dec_seq, mh, dv, omcol, gmcol, gacol):
    t_all = hf.shape[0]
    D = x_prompt2.shape[1]
    mw = mh * dv
    qw = ha_ctx.shape[1]
    E = rw_t.shape[0]
    tm = min(TM_OUTPROJ, t_ctx, dec_seq)
    nctx = t_ctx // tm
    per_seq = dec_seq // tm

    def ctx_blk(i):
        return (jnp.minimum(i, nctx - 1), 0)

    def lat_blk(i):
        return (jnp.maximum(i - nctx, 0), 0)

    def mod_row(i):
        return (jnp.where(i < nctx, 0, 1 + (i - nctx) // per_seq), 0, 0)

    const = lambda i: (0, 0)
    single = pl.Buffered(1)
    return pl.pallas_call(
        functools.partial(_outproj_kernel, nctx_tiles=nctx, mh=mh, dv=dv),
        out_shape=(jax.ShapeDtypeStruct((t_all, D), F32),
                   jax.ShapeDtypeStruct((t_all, D // 2), jnp.uint32),
                   jax.ShapeDtypeStruct((t_all, D // 2 // LANES_V7X, LANES_V7X), jnp.uint32),
                   jax.ShapeDtypeStruct((E, t_all), F32)),
        grid=(t_all // tm,),
        in_specs=[pl.BlockSpec((tm, mw), lambda i: (i, 0)),
                  pl.BlockSpec((tm, mw), lambda i: (i, 0)),
                  pl.BlockSpec((tm, mw), lambda i: (i, omcol)),
                  pl.BlockSpec((tm, qw), ctx_blk),
                  pl.BlockSpec((tm, qw), lat_blk),
                  pl.BlockSpec((tm, D), lambda i: (i, gmcol)),
                  pl.BlockSpec((tm, D), lambda i: (i, gacol)),
                  pl.BlockSpec((tm, D), ctx_blk),
                  pl.BlockSpec((tm, D), lat_blk),
                  pl.BlockSpec((1, N_MOD, D), mod_row),
                  pl.BlockSpec((1, mw), const),
                  pl.BlockSpec((1, D), const),
                  pl.BlockSpec((mw, D), const, pipeline_mode=single),
                  pl.BlockSpec((qw, D), const, pipeline_mode=single),
                  pl.BlockSpec((D, D), const, pipeline_mode=single),
                  pl.BlockSpec((E, D), const, pipeline_mode=single)],
        out_specs=(pl.BlockSpec((tm, D), lambda i: (i, 0)),
                   pl.BlockSpec((tm, D // 2), lambda i: (i, 0)),
                   pl.BlockSpec((tm, D // 2 // LANES_V7X, LANES_V7X), lambda i: (i, 0, 0)),
                   pl.BlockSpec((E, tm), lambda i: (0, i))),
        compiler_params=_params(("arbitrary",)),
        name="outproj",
    )(hf, hb, z, ha_ctx, ha_lat, z, z, x_prompt2, x_sample2, mods, mnorm, n2, wm, wa, wo, rw_t)


def _dispatch_kernel(pstart_ref, pcount_ref, nact_ref, pos_ref, h2t_ref, h2p_ref, x1_ref, mod_ref,
                     w1_ref, w3_ref, w2_ref, xs_hbm, x1s_ref, zblk, sem, psem, *, top_k):
    i = pl.program_id(0)
    td = h2t_ref.shape[0]
    rows = zblk.shape[0]

    @pl.when(i == 0)
    def _():
        zblk[...] = jnp.zeros_like(zblk)
        zrow = zblk.at[0]

        def per_expert(e, total):
            def fill(r, carry):
                pltpu.make_async_copy(zrow, xs_hbm.at[pstart_ref[e] + r], psem).start()
                return carry

            lax.fori_loop(0, pcount_ref[e], fill, 0)
            return total + pcount_ref[e]

        total = lax.fori_loop(0, pstart_ref.shape[0], per_expert, 0)

        def drain(j, carry):
            pltpu.make_async_copy(zrow, xs_hbm.at[0], psem).wait()
            return carry

        lax.fori_loop(0, total, drain, 0)

        def empty_block(b, carry):
            fill = pltpu.make_async_copy(zblk, xs_hbm.at[pl.ds(b * rows, rows)], psem)
            fill.start()
            fill.wait()
            return carry

        lax.fori_loop(nact_ref[0], xs_hbm.shape[0] // rows, empty_block, 0)

    F = w1_ref.shape[1]
    edges = [F * k // top_k // LANES_V7X * LANES_V7X for k in range(top_k)] + [F]
    x = _unpack_bf16_pair(h2p_ref[...])
    hmid = []
    for k in range(top_k):
        for t in range(td):
            pltpu.make_async_copy(h2t_ref.at[t], xs_hbm.at[pos_ref[0, k, t]], sem).start()
        if edges[k + 1] > edges[k]:
            cols = pl.ds(edges[k], edges[k + 1] - edges[k])
            hmid.append((jax.nn.silu(_dot(x, w1_ref[:, cols])) * _dot(x, w3_ref[:, cols])).astype(BF16))
    x1s_ref[...] = x1_ref[...] + mod_ref[0, 5:6, :] * _dot(jnp.concatenate(hmid, axis=1), w2_ref[...])

    for k in range(top_k):
        pltpu.make_async_copy(h2t_ref, xs_hbm.at[pl.ds(0, td)], sem).wait()


def _dispatch_rows(pad_start, pad_count, nact, pos3, h2t, h2p, x1, mods, sw1, sw3, sw2, *, n_slots, rows,
                   t_ctx, dec_seq):
    nt, K, td = pos3.shape
    T, c, _ = h2t.shape
    D = x1.shape[1]
    F = sw1.shape[1]
    nctx = t_ctx // td
    per_seq = dec_seq // td
    const = lambda i, *_: (0, 0)
    single = pl.Buffered(1)
    grid_spec = pltpu.PrefetchScalarGridSpec(
        num_scalar_prefetch=3,
        grid=(nt,),
        in_specs=[pl.BlockSpec((1, K, td), lambda i, *_: (i, 0, 0), memory_space=pltpu.SMEM),
                  pl.BlockSpec((td, c, LANES_V7X), lambda i, *_: (i, 0, 0)),
                  pl.BlockSpec((td, D // 2), lambda i, *_: (i, 0)),
                  pl.BlockSpec((td, D), lambda i, *_: (i, 0)),
                  pl.BlockSpec((1, N_MOD, D),
                               lambda i, *_: (jnp.where(i < nctx, 0, 1 + (i - nctx) // per_seq), 0, 0)),
                  pl.BlockSpec((D, F), const, pipeline_mode=single),
                  pl.BlockSpec((D, F), const, pipeline_mode=single),
                  pl.BlockSpec((F, D), const, pipeline_mode=single)],
        out_specs=(pl.BlockSpec(memory_space=pl.ANY),
                   pl.BlockSpec((td, D), lambda i, *_: (i, 0))),
        scratch_shapes=[pltpu.VMEM((rows, c, LANES_V7X), jnp.uint32), pltpu.SemaphoreType.DMA(()),
                        pltpu.SemaphoreType.DMA(())],
    )
    return pl.pallas_call(
        functools.partial(_dispatch_kernel, top_k=K),
        out_shape=(jax.ShapeDtypeStruct((n_slots, c, LANES_V7X), jnp.uint32),
                   jax.ShapeDtypeStruct((T, D), F32)),
        grid_spec=grid_spec,
        compiler_params=_params(("arbitrary",)),
        name="dispatch_shared",
    )(pad_start, pad_count, nact, pos3, h2t, h2p, x1, mods, sw1, sw3, sw2)


def _moe_kernel(blk_e, nact_ref, w1_hbm, w3_hbm, w2_hbm, xs_hbm, y_ref,
                xbuf0, xbuf1, w1b, w3b, w2b, stg_in, stg_out, sem, wsem, *, rows):
    i = pl.program_id(0)
    nact = nact_ref[0]
    c = xs_hbm.shape[1]
    e = blk_e[i]

    def fetch(b, buf, s):
        for a in range(c):
            pltpu.make_async_copy(xs_hbm.at[pl.ds(b * rows, rows), a, :],
                                  buf.at[:, pl.ds(a * LANES_V7X, LANES_V7X)], s).start()

    def fetch_wait(buf, s):
        for a in range(c):
            pltpu.make_async_copy(xs_hbm.at[pl.ds(0, rows), a, :],
                                  buf.at[:, pl.ds(a * LANES_V7X, LANES_V7X)], s).wait()

    @pl.when(i == 0)
    def _():
        fetch(0, xbuf0, sem.at[0])

    rc_in, rc_out = stg_in.shape[1], stg_out.shape[1]
    plan = ([(w1_hbm, w1b, stg_in, r, rc_in) for r in range(0, w1b.shape[0], rc_in)]
            + [(w3_hbm, w3b, stg_in, r, rc_in) for r in range(0, w3b.shape[0], rc_in)]
            + [(w2_hbm, w2b, stg_out, r, rc_out) for r in range(0, w2b.shape[0], rc_out)])
    depth = stg_in.shape[0]

    def chunk_copy(ex, n):
        src, _, stg, r, rc = plan[n]
        return pltpu.make_async_copy(src.at[ex, pl.ds(r, rc), :], stg.at[n % depth], wsem.at[n % depth])

    @pl.when(jnp.logical_and(i < nact, jnp.logical_or(i == 0, e != blk_e[jnp.maximum(i - 1, 0)])))
    def _():
        @pl.when(i == 0)
        def _():
            for n in range(depth - 1):
                chunk_copy(e, n).start()

        for n, (_, dst, stg, r, rc) in enumerate(plan):
            if n + depth - 1 < len(plan):
                chunk_copy(e, n + depth - 1).start()
            chunk_copy(e, n).wait()
            dst[pl.ds(r, rc), :] = stg[n % depth].astype(BF16)

    e_next = blk_e[jnp.minimum(i + 1, pl.num_programs(0) - 1)]

    @pl.when(jnp.logical_and(i + 1 < nact, e_next != e))
    def _():
        for n in range(depth - 1):
            chunk_copy(e_next, n).start()

    def block(cur, sem_cur, nxt, sem_nxt):
        fetch_wait(cur, sem_cur)

        @pl.when(i + 1 < nact)
        def _():
            fetch(i + 1, nxt, sem_nxt)

        x = _unpack_bf16_pair(cur[...])
        hmid = (jax.nn.silu(_dot(x, w1b[...])) * _dot(x, w3b[...])).astype(BF16)
        y_ref[...] = _pack_bf16_pair(_dot(hmid, w2b[...]))

    @pl.when(jnp.logical_and(i < nact, i % 2 == 0))
    def _():
        block(xbuf0, sem.at[0], xbuf1, sem.at[1])

    @pl.when(jnp.logical_and(i < nact, i % 2 == 1))
    def _():
        block(xbuf1, sem.at[1], xbuf0, sem.at[0])

    @pl.when(i >= nact)
    def _():
        y_ref[...] = jnp.zeros_like(y_ref)


def _moe(blk_e, nact, xs, w1, w3, w2, *, rows):
    nblk = blk_e.shape[0]
    E, D, F = w1.shape
    hbm = pl.BlockSpec(memory_space=pl.ANY)
    grid_spec = pltpu.PrefetchScalarGridSpec(
        num_scalar_prefetch=2,
        grid=(nblk,),
        in_specs=[hbm, hbm, hbm, hbm],
        out_specs=pl.BlockSpec((rows, D // 2), lambda i, be, na: (i, 0)),
        scratch_shapes=[pltpu.VMEM((rows, D // 2), jnp.uint32), pltpu.VMEM((rows, D // 2), jnp.uint32),
                        pltpu.VMEM((D, F), BF16), pltpu.VMEM((D, F), BF16), pltpu.VMEM((F, D), BF16),
                        pltpu.VMEM((WEIGHT_RING, WEIGHT_CHUNK_ROWS, F), F32),
                        pltpu.VMEM((WEIGHT_RING, WEIGHT_CHUNK_ROWS, D), F32),
                        pltpu.SemaphoreType.DMA((2,)), pltpu.SemaphoreType.DMA((WEIGHT_RING,))],
    )
    return pl.pallas_call(
        functools.partial(_moe_kernel, rows=rows),
        out_shape=jax.ShapeDtypeStruct((nblk * rows, D // 2), jnp.uint32),
        grid_spec=grid_spec,
        compiler_params=_params(("arbitrary",)),
        name="routed_experts",
    )(blk_e, nact, w1, w3, w2, xs)


def _router_kernel(lg_ref, rb_ref, eidx_ref, rank_ref, wtok_ref, cnt_ref, carry, *, top_k, n_groups, topk_groups):
    i = pl.program_id(0)

    @pl.when(i == 0)
    def _():
        carry[...] = jnp.zeros_like(carry)

    E, tr = lg_ref.shape
    gs = E // n_groups
    scores = jax.nn.sigmoid(lg_ref[...])
    biased = scores + rb_ref[...]
    b3 = biased.reshape(n_groups, gs, tr)
    io3 = lax.broadcasted_iota(jnp.int32, b3.shape, 1)
    m1 = jnp.max(b3, axis=1, keepdims=True)
    i1 = jnp.min(jnp.where(b3 == m1, io3, gs), axis=1, keepdims=True)
    m2 = jnp.max(jnp.where(io3 == i1, -jnp.inf, b3), axis=1, keepdims=True)
    grp = (m1 + m2).reshape(n_groups, tr)
    iog = lax.broadcasted_iota(jnp.int32, grp.shape, 0)
    sel = jnp.zeros(grp.shape, jnp.bool_)
    for _ in range(topk_groups):
        mx = jnp.max(grp, axis=0, keepdims=True)
        hit = iog == jnp.min(jnp.where(grp == mx, iog, n_groups), axis=0, keepdims=True)
        sel = jnp.logical_or(sel, hit)
        grp = jnp.where(hit, -jnp.inf, grp)
    masked = jnp.where(sel.reshape(n_groups, 1, tr), b3, -jnp.inf).reshape(E, tr)
    ioe = lax.broadcasted_iota(jnp.int32, (E, tr), 0)
    onehot = jnp.zeros((E, tr), F32)
    hits, idxs, ws = [], [], []
    for _ in range(top_k):
        mx = jnp.max(masked, axis=0, keepdims=True)
        ix = jnp.min(jnp.where(masked == mx, ioe, E), axis=0, keepdims=True)
        hit = ioe == ix
        hits.append(hit)
        idxs.append(ix)
        ws.append(jnp.sum(jnp.where(hit, scores, 0.0), axis=0, keepdims=True))
        onehot = onehot + hit.astype(F32)
        masked = jnp.where(hit, -jnp.inf, masked)
    wsum = ws[0]
    for w in ws[1:]:
        wsum = wsum + w
    ri = lax.broadcasted_iota(jnp.int32, (tr, tr), 0)
    ci = lax.broadcasted_iota(jnp.int32, (tr, tr), 1)
    before = _dot(onehot.astype(BF16), (ri < ci).astype(BF16)) + carry[...]
    ranks = [jnp.sum(jnp.where(hit, before, 0.0), axis=0, keepdims=True) for hit in hits]
    carry[...] = carry[...] + jnp.sum(onehot, axis=1, keepdims=True)
    cnt_ref[...] = carry[...].astype(jnp.int32)
    eidx_ref[...] = jnp.concatenate(idxs, axis=0)
    rank_ref[...] = jnp.concatenate(ranks, axis=0).astype(jnp.int32)
    wrows = jnp.concatenate([w / wsum * ROUTED_SCALE for w in ws]
                            + [jnp.zeros((LANES_V7X - top_k, tr), F32)], axis=0)
    wtok_ref[...] = wrows.T


def _router(logits_t, router_b):
    E, T = logits_t.shape
    tr = min(TR_ROUTER, T)
    return pl.pallas_call(
        functools.partial(_router_kernel, top_k=TOP_K, n_groups=N_GROUPS, topk_groups=TOPK_GROUPS),
        out_shape=(jax.ShapeDtypeStruct((TOP_K, T), jnp.int32),
                   jax.ShapeDtypeStruct((TOP_K, T), jnp.int32),
                   jax.ShapeDtypeStruct((T, LANES_V7X), F32),
                   jax.ShapeDtypeStruct((E, 1), jnp.int32)),
        grid=(T // tr,),
        in_specs=[pl.BlockSpec((E, tr), lambda i: (0, i)),
                  pl.BlockSpec((E, 1), lambda i: (0, 0))],
        out_specs=(pl.BlockSpec((TOP_K, tr), lambda i: (0, i)),
                   pl.BlockSpec((TOP_K, tr), lambda i: (0, i)),
                   pl.BlockSpec((tr, LANES_V7X), lambda i: (i, 0)),
                   pl.BlockSpec((E, 1), lambda i: (0, 0))),
        scratch_shapes=[pltpu.VMEM((E, 1), F32)],
        compiler_params=_params(("arbitrary",)),
        name="router",
    )(logits_t, router_b.reshape(E, 1))


def _slot_pos_kernel(start_ref, eidx_ref, rank_ref, pos_ref, *, n_experts):
    eidx = eidx_ref[...]
    pos = rank_ref[...]
    for e in range(n_experts):
        pos = pos + jnp.where(eidx == e, start_ref[e], 0)
    pos_ref[...] = pos


def _slot_pos(start_pad, eidx, rank):
    K, T = eidx.shape
    tl = min(TL_SLOTPOS, T)
    return pl.pallas_call(
        functools.partial(_slot_pos_kernel, n_experts=start_pad.shape[0]),
        out_shape=jax.ShapeDtypeStruct((K, T), jnp.int32),
        grid=(T // tl,),
        in_specs=[pl.BlockSpec(memory_space=pltpu.SMEM),
                  pl.BlockSpec((K, tl), lambda i: (0, i)),
                  pl.BlockSpec((K, tl), lambda i: (0, i))],
        out_specs=pl.BlockSpec((K, tl), lambda i: (0, i)),
        compiler_params=_params(("arbitrary",)),
        name="slot_pos",
    )(start_pad, eidx, rank)


def _dispatch(eidx, rank, counts, rows):
    K, T = eidx.shape
    A = K * T
    E = counts.shape[0]
    padded = (counts + rows - 1) // rows * rows
    end_pad = jnp.cumsum(padded)
    start_pad = end_pad - padded
    nblk = -(-(A + E * (rows - 1)) // rows)
    n_slots = nblk * rows
    nact = end_pad[-1] // rows
    blk = jnp.arange(nblk, dtype=jnp.int32)
    blk_e = jnp.sum(end_pad[None, :] <= (blk * rows)[:, None], axis=1).astype(jnp.int32)
    last_e = jnp.sum(end_pad <= (nact - 1) * rows).astype(jnp.int32)
    blk_e = jnp.where(blk < nact, blk_e, last_e)
    pos = _slot_pos(start_pad.astype(jnp.int32), eidx, rank)
    pad_start = (start_pad + counts).astype(jnp.int32)
    pad_count = (padded - counts).astype(jnp.int32)
    return pos, blk_e, nact.astype(jnp.int32).reshape(1), pad_start, pad_count, n_slots


def _tile_major(pos, tile):
    K, T = pos.shape
    return pos.reshape(K, T // tile, tile).transpose(1, 0, 2)


def _final_kernel(pos_ref, posn_ref, x1s_ref, wt_ref, mod_ref, fn_ref, y_hbm,
                  oc_ref, ol_ref, ybuf0, ybuf1, sem, *, top_k, nctx_tiles):
    i = pl.program_id(0)
    nt = pl.num_programs(0)
    tm = x1s_ref.shape[0]

    def gather(idx_ref, buf, s):
        for k in range(top_k):
            for t in range(tm):
                pltpu.make_async_copy(y_hbm.at[pl.ds(idx_ref[0, k, t], 1)], buf.at[k, pl.ds(t, 1)], s).start()

    def gather_wait(buf, s):
        for k in range(top_k):
            pltpu.make_async_copy(y_hbm.at[pl.ds(0, tm)], buf.at[k], s).wait()

    @pl.when(i == 0)
    def _():
        gather(pos_ref, ybuf0, sem.at[0])

    def tile(cur, sem_cur, nxt, sem_nxt):
        @pl.when(i + 1 < nt)
        def _():
            gather(posn_ref, nxt, sem_nxt)

        gather_wait(cur, sem_cur)
        wt = wt_ref[...]
        lo = hi = None
        for k in range(top_k):
            w = cur[k]
            wk = wt[:, k:k + 1]
            lo_k = pltpu.bitcast(w << 16, F32) * wk
            hi_k = pltpu.bitcast(w & jnp.uint32(0xFFFF0000), F32) * wk
            lo = lo_k if lo is None else lo + lo_k
            hi = hi_k if hi is None else hi + hi_k
        x2 = x1s_ref[...] + mod_ref[0, 5:6, :] * jnp.concatenate([lo, hi], axis=1)
        out = x2 * lax.rsqrt(jnp.mean(x2 * x2, axis=-1, keepdims=True) + EPS) * fn_ref[...]

        @pl.when(i < nctx_tiles)
        def _():
            oc_ref[...] = out

        @pl.when(i >= nctx_tiles)
        def _():
            ol_ref[...] = out

    @pl.when(i % 2 == 0)
    def _():
        tile(ybuf0, sem.at[0], ybuf1, sem.at[1])

    @pl.when(i % 2 == 1)
    def _():
        tile(ybuf1, sem.at[1], ybuf0, sem.at[0])


def _final(pos3, x1s, wtok, ys, mods, fnorm, *, t_ctx, dec_seq):
    nt, K, tm = pos3.shape
    t_all, D = x1s.shape
    nctx = t_ctx // tm
    per_seq = dec_seq // tm

    def mod_row(i):
        return (jnp.where(i < nctx, 0, 1 + (i - nctx) // per_seq), 0, 0)

    const = lambda i: (0, 0)
    smem_blk = lambda f: pl.BlockSpec((1, K, tm), f, memory_space=pltpu.SMEM)
    return pl.pallas_call(
        functools.partial(_final_kernel, top_k=K, nctx_tiles=nctx),
        out_shape=(jax.ShapeDtypeStruct((t_ctx, D), F32), jax.ShapeDtypeStruct((t_all - t_ctx, D), F32)),
        grid=(nt,),
        in_specs=[smem_blk(lambda i: (i, 0, 0)),
                  smem_blk(lambda i: (jnp.minimum(i + 1, nt - 1), 0, 0)),
                  pl.BlockSpec((tm, D), lambda i: (i, 0)),
                  pl.BlockSpec((tm, LANES_V7X), lambda i: (i, 0)),
                  pl.BlockSpec((1, N_MOD, D), mod_row),
                  pl.BlockSpec((1, D), const),
                  pl.BlockSpec(memory_space=pl.ANY)],
        out_specs=(pl.BlockSpec((tm, D), lambda i: (jnp.minimum(i, nctx - 1), 0)),
                   pl.BlockSpec((tm, D), lambda i: (jnp.maximum(i - nctx, 0), 0))),
        scratch_shapes=[pltpu.VMEM((K, tm, D // 2), jnp.uint32), pltpu.VMEM((K, tm, D // 2), jnp.uint32),
                        pltpu.SemaphoreType.DMA((2,))],
        compiler_params=_params(("arbitrary",)),
        name="combine_final",
    )(pos3, pos3, x1s, wtok, mods, fnorm, ys)


def _rope_tables(dec_seq, hd):
    nf = hd // 4
    t = jnp.arange(dec_seq)
    inv = ROPE_BASE ** (-jnp.arange(nf, dtype=F32) / nf)
    ang_r = (t // GRID_W).astype(F32)[:, None] * inv
    ang_c = (t % GRID_W).astype(F32)[:, None] * inv
    cos = jnp.concatenate([jnp.cos(ang_r)] * 2 + [jnp.cos(ang_c)] * 2, axis=1)
    sin = jnp.concatenate([-jnp.sin(ang_r), jnp.sin(ang_r), -jnp.sin(ang_c), jnp.sin(ang_c)], axis=1)
    return cos, sin


def _scan_steps(batch, seq, dec_batch, dec_seq, L):
    fb, bb, sq, fi, la = [], [], [], [], []
    base = 0
    for sid, S in enumerate([seq] * batch + [dec_seq] * dec_batch):
        nc = S // L
        for c in range(nc):
            fb.append(base + c)
            bb.append(base + nc - 1 - c)
            sq.append(sid)
            fi.append(int(c == 0))
            la.append(int(c == nc - 1))
        base += nc
    return tuple(jnp.asarray(np.asarray(a, dtype=np.int32)) for a in (fb, bb, sq, fi, la))


def kernel(x_prompt, x_sample, cache_k, cache_v, state_mlstm_C, state_mlstm_n, state_mlstm_m, c, c_ctx,
           w_mod, b_mod, norm1_w, norm2_w, w_in, igate_b, fgate_b, mlstm_norm_w, attn_sink,
           w_branch_m, w_branch_a, w_out, router_w, router_b, expert_w1, expert_w3, expert_w2,
           shared_w1, shared_w3, shared_w2, final_norm_w):
    batch, seq, D = x_prompt.shape
    dec_batch, dec_seq, _ = x_sample.shape
    depth = w_in.shape[0]
    assert depth == 1, "single trunk layer"
    _, _, past, kvh, hd = cache_k.shape
    mh, dk, dv = state_mlstm_C.shape[3:]
    ah = attn_sink.shape[1]
    groups = ah // kvh
    E = router_w.shape[2]
    assert dk == dv == hd == LANES_V7X
    t_ctx, t_lat = batch * seq, dec_batch * dec_seq
    mw, qw, kw = mh * dk, ah * hd, kvh * hd
    ng = 4 * mh

    wi = w_in[0]
    o = 0
    seg = {}
    for name, width in (("qm", mw), ("km", mw), ("vm", mw), ("om", mw), ("im", 2 * mh), ("fm", 2 * mh),
                        ("qa", qw), ("ka", kw), ("va", kw), ("gm", D), ("ga", D)):
        seg[name] = wi[:, o:o + width]
        o += width
    order = ("qa", "gm", "ga", "qm", "km", "vm", "om", "ka", "va")
    w_main = jnp.concatenate([seg[nm] for nm in order], axis=1).astype(BF16)
    col = {}
    o = 0
    for nm in order:
        col[nm] = o
        o += seg[nm].shape[1]
    tn = 2 * kw
    for nm in order[:-2]:
        assert col[nm] % tn == 0 and seg[nm].shape[1] % tn == 0
    for nm, width in (("gm", D), ("ga", D), ("qm", mw), ("km", mw), ("vm", mw), ("om", mw), ("ka", kw), ("va", kw)):
        assert col[nm] % width == 0
    w_gate = jnp.pad(jnp.concatenate([seg["im"], seg["fm"]], axis=1), ((0, 0), (0, LANES_V7X - ng))).astype(BF16)

    R = -(-(1 + dec_batch) // 8) * 8
    cond = jnp.concatenate([c_ctx[None, :], c, jnp.zeros((R - 1 - dec_batch, D), F32)], axis=0)
    mods = _modulation(cond, w_mod[0], b_mod[0]).reshape(R, N_MOD, D)

    xp2 = x_prompt.reshape(t_ctx, D)
    xs2 = x_sample.reshape(t_lat, D)
    cos, sin = _rope_tables(dec_seq, hd)
    h1, gates, gates_t = _prenorm(xp2, xs2, mods, norm1_w, w_gate, t_ctx=t_ctx, dec_seq=dec_seq, n_gates=ng)
    z, kv32 = _inproj(h1, w_main, cos, sin, t_ctx=t_ctx, dec_seq=dec_seq, tn=tn, n_rope_tiles=qw // tn)

    nseq = batch + dec_batch
    C0 = jnp.concatenate([jnp.zeros((batch, 2, mh, dk, dv), F32), state_mlstm_C[:, 0]], axis=0)
    n0 = jnp.concatenate([jnp.zeros((batch, 2, mh, dk), F32), state_mlstm_n[:, 0]], axis=0)
    m0 = jnp.concatenate([jnp.zeros((batch, 2, mh), F32), state_mlstm_m[:, 0]], axis=0)
    s0 = jnp.concatenate([C0, jnp.broadcast_to(n0[..., None], (nseq, 2, mh, dk, dv))], axis=-1)
    s0 = s0.reshape(nseq, 2 * mh, dk, 2 * dv)
    m0 = jnp.broadcast_to(m0.reshape(nseq, 2 * mh, 1), (nseq, 2 * mh, LANES_V7X))
    gate_b = jnp.concatenate([igate_b[0].reshape(-1), fgate_b[0].reshape(-1)])
    steps = _scan_steps(batch, seq, dec_batch, dec_seq, M_CHUNK)
    hf, hb, s_fin, m_fin = _mlstm(z, gates, gates_t, gate_b.reshape(1, ng), gate_b.reshape(ng, 1), s0, m0, steps,
                                  mh=mh, dk=dk, qcol=col["qm"] // mw, kcol=col["km"] // mw, vcol=col["vm"] // mw)

    sink = attn_sink[0]
    ha_ctx = _ctx_attention(sink, z, batch=batch, seq=seq, kvh=kvh, groups=groups, hd=hd,
                            kcol=col["ka"] // kw, vcol=col["va"] // kw)
    ha_lat = _lat_attention(sink, z, cache_k.reshape(dec_batch, depth, past, kw),
                            cache_v.reshape(dec_batch, depth, past, kw), t_ctx=t_ctx, dec_batch=dec_batch,
                            dec_seq=dec_seq, kvh=kvh, groups=groups, hd=hd, kcol=col["ka"] // kw,
                            vcol=col["va"] // kw)

    x1, h2p, h2t, logits_t = _outproj(hf, hb, z, ha_ctx, ha_lat, xp2, xs2, mods, mlstm_norm_w, norm2_w,
                                w_branch_m[0].astype(BF16), w_branch_a[0].astype(BF16), w_out[0].astype(BF16),
                                router_w[0].T, t_ctx=t_ctx, dec_seq=dec_seq, mh=mh, dv=dv,
                                omcol=col["om"] // mw, gmcol=col["gm"] // D, gacol=col["ga"] // D)

    eidx, rank, wtok, counts = _router(logits_t, router_b[0])
    pos, blk_e, nact, pad_start, pad_count, n_slots = _dispatch(eidx, rank, counts[:, 0], EXPERT_ROWS)
    xs, x1s = _dispatch_rows(pad_start, pad_count, nact, _tile_major(pos, min(TD_DISPATCH, t_ctx, dec_seq)),
                             h2t, h2p, x1, mods, shared_w1[0].astype(BF16), shared_w3[0].astype(BF16),
                             shared_w2[0].astype(BF16), n_slots=n_slots, rows=EXPERT_ROWS, t_ctx=t_ctx,
                             dec_seq=dec_seq)
    ys = _moe(blk_e, nact, xs, expert_w1[0], expert_w3[0], expert_w2[0], rows=EXPERT_ROWS)

    y_ctx, y_lat = _final(_tile_major(pos, min(TM_FINAL, t_ctx, dec_seq)), x1s, wtok, ys, mods,
                          final_norm_w.reshape(1, D), t_ctx=t_ctx, dec_seq=dec_seq)

    y_prompt = y_ctx.reshape(batch, seq, D)
    y_sample = y_lat.reshape(dec_batch, dec_seq, D)
    new_k = kv32[:t_ctx, :kw].reshape(batch, 1, seq, kvh, hd)
    new_v = kv32[:t_ctx, kw:].reshape(batch, 1, seq, kvh, hd)
    s_ctx = s_fin[:batch].reshape(batch, 1, 2, mh, dk, 2 * dv)
    new_C = s_ctx[..., :dv]
    new_n = s_ctx[..., dv]
    new_m = m_fin[:batch, :, 0].reshape(batch, 1, 2, mh)
    return y_prompt, y_sample, new_k, new_v, new_C, new_n, new_m
```

```python
import functools

import numpy as np
import jax
import jax.numpy as jnp
from jax import lax
from jax.experimental import pallas as pl
from jax.experimental.pallas import tpu as pltpu

TOP_K = 6
N_GROUPS = 8
TOPK_GROUPS = 4
ROUTED_SCALE = 2.5
WINDOW = 128
Q_BLOCK = 128
GRID_W = 64
ROPE_BASE = 10000.0
M_CHUNK = 128
N_MOD = 6
EPS = 1e-6

LANES_V7X = 128
MXU_COLS_V7X = 256
VMEM_LIMIT_V7X = 56 * 1024 * 1024

TM_PRENORM = 512
TM_INPROJ = 1024
TM_OUTPROJ = 256
OUTPROJ_SPLIT = 2
WEIGHT_CHUNK_ROWS = 128
WEIGHT_RING = 8
TM_FINAL = 256
TN_MOD = 1024
EXPERT_ROWS = 256
TR_ROUTER = 512
TL_SLOTPOS = 2048
TD_DISPATCH = 256

F32 = jnp.float32
BF16 = jnp.bfloat16
_NT = (((1,), (1,)), ((), ()))


def _params(sem):
    return pltpu.CompilerParams(dimension_semantics=sem, vmem_limit_bytes=VMEM_LIMIT_V7X)


def _dot(a, b):
    return jnp.dot(a, b, preferred_element_type=F32)


def _dot_nt(a, b):
    return lax.dot_general(a, b, _NT, preferred_element_type=F32)


def _pack_bf16_pair(x):
    c = x.shape[1] // 2
    lo = pltpu.bitcast(x[:, :c].astype(BF16).astype(F32), jnp.uint32)
    hi = pltpu.bitcast(x[:, c:].astype(BF16).astype(F32), jnp.uint32)
    return (lo >> 16) | (hi & jnp.uint32(0xFFFF0000))


def _unpack_bf16_pair(w):
    lo = pltpu.bitcast(w << 16, F32).astype(BF16)
    hi = pltpu.bitcast(w & jnp.uint32(0xFFFF0000), F32).astype(BF16)
    return jnp.concatenate([lo, hi], axis=1)


def _mod_kernel(c_ref, w_ref, b_ref, o_ref):
    s = jax.nn.silu(c_ref[...]).astype(BF16)
    o_ref[...] = _dot(s, w_ref[...].astype(BF16)) + b_ref[...]


def _modulation(cond, w_mod, b_mod):
    R, D = cond.shape
    N = w_mod.shape[1]
    tn = min(TN_MOD, N)
    return pl.pallas_call(
        _mod_kernel,
        out_shape=jax.ShapeDtypeStruct((R, N), F32),
        grid=(N // tn,),
        in_specs=[pl.BlockSpec((R, D), lambda n: (0, 0)),
                  pl.BlockSpec((D, tn), lambda n: (0, n)),
                  pl.BlockSpec((1, tn), lambda n: (0, n))],
        out_specs=pl.BlockSpec((R, tn), lambda n: (0, n)),
        compiler_params=_params(("arbitrary",)),
        name="modulation",
    )(cond, w_mod, b_mod.reshape(1, N))


def _rope_slice(x, cos, sin_signed, first_half):
    swap = jnp.where(first_half, pltpu.roll(x, 96, 1), pltpu.roll(x, 32, 1))
    return x * cos + swap * sin_signed


def _prenorm_kernel(xp_ref, xs_ref, mod_ref, n1_ref, wg_ref, h_ref, g_ref, gt_ref, *, nctx_tiles, n_gates):
    i = pl.program_id(0)
    x = jnp.where(i < nctx_tiles, xp_ref[...], xs_ref[...])
    y = x * lax.rsqrt(jnp.mean(x * x, axis=-1, keepdims=True) + EPS) * n1_ref[...]
    h = (y * (1.0 + mod_ref[0, 1:2, :]) + mod_ref[0, 0:1, :]).astype(BF16)
    h_ref[...] = h
    g = _dot(h, wg_ref[...])
    g_ref[...] = g[:, :n_gates]
    gt_ref[...] = g.T[:n_gates, :]


def _prenorm(x_prompt2, x_sample2, mods, n1, w_gate, *, t_ctx, dec_seq, n_gates):
    t_lat, D = x_sample2.shape
    t_all = t_ctx + t_lat
    tm = min(TM_PRENORM, t_ctx, dec_seq)
    nctx = t_ctx // tm
    per_seq = dec_seq // tm
    return pl.pallas_call(
        functools.partial(_prenorm_kernel, nctx_tiles=nctx, n_gates=n_gates),
        out_shape=(jax.ShapeDtypeStruct((t_all, D), BF16),
                   jax.ShapeDtypeStruct((t_all, n_gates), F32),
                   jax.ShapeDtypeStruct((n_gates, t_all), F32)),
        grid=(t_all // tm,),
        in_specs=[pl.BlockSpec((tm, D), lambda i: (jnp.minimum(i, nctx - 1), 0)),
                  pl.BlockSpec((tm, D), lambda i: (jnp.maximum(i - nctx, 0), 0)),
                  pl.BlockSpec((1, N_MOD, D), lambda i: (jnp.where(i < nctx, 0, 1 + (i - nctx) // per_seq), 0, 0)),
                  pl.BlockSpec((1, D), lambda i: (0, 0)),
                  pl.BlockSpec((D, LANES_V7X), lambda i: (0, 0))],
        out_specs=(pl.BlockSpec((tm, D), lambda i: (i, 0)),
                   pl.BlockSpec((tm, n_gates), lambda i: (i, 0)),
                   pl.BlockSpec((n_gates, tm), lambda i: (0, i))),
        compiler_params=_params(("arbitrary",)),
        name="prenorm",
    )(x_prompt2, x_sample2, mods, n1, w_gate)


def _inproj_kernel(h_ref, w_ref, cos_ref, sin_ref, z_ref, kv_ref, *, nctx_tiles, n_rope_tiles, kv_tile):
    i = pl.program_id(0)
    n = pl.program_id(1)
    is_ctx = i < nctx_tiles
    tn = z_ref.shape[1]
    acc = _dot(h_ref[...], w_ref[...])

    def rope_cols(ncols):
        cos = jnp.where(is_ctx, 1.0, cos_ref[...])
        sin = jnp.where(is_ctx, 0.0, sin_ref[...])
        lane = lax.broadcasted_iota(jnp.int32, cos.shape, 1)
        first_half = (lane % 64) < 32
        return [_rope_slice(acc[:, c:c + LANES_V7X], cos, sin, first_half)
                for c in range(0, ncols, LANES_V7X)]

    @pl.when(n < n_rope_tiles)
    def _():
        z_ref[...] = jnp.concatenate(rope_cols(tn), axis=1).astype(BF16)

    @pl.when(n == kv_tile)
    def _():
        r = jnp.concatenate(rope_cols(tn // 2) + [acc[:, tn // 2:]], axis=1)
        z_ref[...] = r.astype(BF16)
        kv_ref[...] = r

    @pl.when(jnp.logical_and(n >= n_rope_tiles, n != kv_tile))
    def _():
        z_ref[...] = acc.astype(BF16)


def _inproj(h, w_main, cos, sin, *, t_ctx, dec_seq, tn, n_rope_tiles):
    t_all, D = h.shape
    tm = min(TM_INPROJ, t_ctx, dec_seq)
    nctx = t_ctx // tm
    per_seq = dec_seq // tm
    ncols = w_main.shape[1]
    ntile = ncols // tn
    kv_tile = ntile - 1

    def pos_blk(i):
        return jnp.where(i < nctx, 0, (i - nctx) % per_seq)

    kernel = functools.partial(_inproj_kernel, nctx_tiles=nctx, n_rope_tiles=n_rope_tiles, kv_tile=kv_tile)
    return pl.pallas_call(
        kernel,
        out_shape=(jax.ShapeDtypeStruct((t_all, ncols), BF16),
                   jax.ShapeDtypeStruct((t_all, tn), F32)),
        grid=(t_all // tm, ntile),
        in_specs=[pl.BlockSpec((tm, D), lambda i, n: (i, 0)),
                  pl.BlockSpec((D, tn), lambda i, n: (0, n)),
                  pl.BlockSpec((tm, LANES_V7X), lambda i, n: (pos_blk(i), 0)),
                  pl.BlockSpec((tm, LANES_V7X), lambda i, n: (pos_blk(i), 0))],
        out_specs=(pl.BlockSpec((tm, tn), lambda i, n: (i, n)),
                   pl.BlockSpec((tm, tn), lambda i, n: (i, 0))),
        compiler_params=_params(("arbitrary", "arbitrary")),
        name="inproj",
    )(h, w_main, cos, sin)


def _mlstm_kernel(fb, bb, sq, fi, la,
                  qf, kf, vf, qb, kb, vb, gf, gb, gtf, gtb, brow, bcol, s0, m0,
                  hf, hb, s_out, m_out, s_scr, m_scr, *, mh, dk, chunk):
    s = pl.program_id(0)
    L = chunk
    assert L == dk
    scale = dk ** -0.5

    @pl.when(fi[s] == 1)
    def _():
        s_scr[...] = s0[0]
        m_scr[...] = m0[0]

    ri = lax.broadcasted_iota(jnp.int32, (L, L), 0)
    ci = lax.broadcasted_iota(jnp.int32, (L, L), 1)
    low = ri >= ci
    upp = ri <= ci
    low_f = low.astype(F32)
    upp_f = upp.astype(F32)
    ones_blk = jnp.ones((L, dk), BF16)
    hi = lax.Precision.HIGHEST
    refs = ((qf, kf, vf, gf, gtf, hf), (qb, kb, vb, gb, gtb, hb))
    units = [(dr, h) for dr in range(2) for h in range(mh)]
    sl = lambda h: slice(h * dk, (h + 1) * dk)

    gate = []
    for dr, (_, _, _, g_ref, gt_ref, _) in enumerate(refs):
        G = g_ref[...] + brow[...]
        GT = gt_ref[...] + bcol[...]
        ic_col = G[:, dr * mh:(dr + 1) * mh]
        lf_col = jax.nn.log_sigmoid(G[:, (2 + dr) * mh:(3 + dr) * mh])
        ic_row = GT[dr * mh:(dr + 1) * mh, :]
        lf_row = jax.nn.log_sigmoid(GT[(2 + dr) * mh:(3 + dr) * mh, :])
        b_col = jnp.dot(low_f if dr == 0 else upp_f, lf_col, precision=hi, preferred_element_type=F32)
        b_row = jnp.dot(lf_row, upp_f if dr == 0 else low_f, precision=hi, preferred_element_type=F32)
        gate.append((ic_col, ic_row, b_col, b_row))

    S_prev = [s_scr[dr * mh + h] for dr, h in units]
    m_prev = [m_scr[dr * mh + h:dr * mh + h + 1, 0:1] for dr, h in units]
    q = [refs[dr][0][:, sl(h)] for dr, h in units]
    k = [refs[dr][1][:, sl(h)] for dr, h in units]
    v_aug = [jnp.concatenate([refs[dr][2][:, sl(h)], ones_blk], axis=1) for dr, h in units]
    qk = [_dot_nt(q[u], k[u]) for u in range(len(units))]
    qs = [_dot(q[u], S_prev[u].astype(BF16)) for u in range(len(units))]

    sm, w_inter, floor, b_rep = [], [], [], []
    for u, (dr, h) in enumerate(units):
        ic_col, ic_row, b_col, b_row = gate[dr]
        bc = jnp.broadcast_to(b_col[:, h:h + 1], (L, L))
        d = jnp.where(low if dr == 0 else upp, bc - b_row[h:h + 1, :] + ic_row[h:h + 1, :], -jnp.inf)
        inter = bc + m_prev[u]
        m_t = jnp.maximum(inter, jnp.broadcast_to(jnp.max(d, axis=-1, keepdims=True), (L, L)))
        sm.append((qk[u] * scale * jnp.exp(d - m_t)).astype(BF16))
        w_inter.append(jnp.exp(inter - m_t))
        floor.append(jnp.exp(-m_t))
        b_rep.append(bc)

    sv = [_dot(sm[u], v_aug[u]) for u in range(len(units))]
    for u, (dr, h) in enumerate(units):
        num = sv[u][:, :dk] + w_inter[u] * qs[u][:, :dk]
        den = sv[u][:, dk:] + w_inter[u] * qs[u][:, dk:]
        refs[dr][5][:, sl(h)] = (num / jnp.maximum(jnp.abs(den), floor[u])).astype(BF16)

    kw_t, wc, m_new = [], [], []
    for u, (dr, h) in enumerate(units):
        ic_col = gate[dr][0]
        bc = b_rep[u]
        b_last = bc[L - 1:L, :] if dr == 0 else bc[0:1, :]
        g = b_last - bc + jnp.broadcast_to(ic_col[:, h:h + 1], (L, L))
        mn = jnp.maximum(b_last + m_prev[u], jnp.max(g, axis=0, keepdims=True))
        kw_t.append((k[u].astype(F32) * (jnp.exp(g - mn) * scale)).T.astype(BF16))
        wc.append(jnp.exp(b_last + m_prev[u] - mn))
        m_new.append(mn)

    upd = [_dot(kw_t[u], v_aug[u]) for u in range(len(units))]
    for u, (dr, h) in enumerate(units):
        r = dr * mh + h
        s_scr[r] = jnp.concatenate([wc[u], wc[u]], axis=1) * S_prev[u] + upd[u]
        m_scr[r:r + 1, :] = m_new[u]

    @pl.when(la[s] == 1)
    def _():
        s_out[0] = s_scr[...]
        m_out[0] = m_scr[...]


def _mlstm(z, gates, gates_t, brow, bcol, s0, m0, steps, *, mh, dk, qcol, kcol, vcol):
    t_all = z.shape[0]
    L = M_CHUNK
    mw = mh * dk
    ng = gates.shape[1]
    nseq = s0.shape[0]
    fb, bb, sq, fi, la = steps
    nsteps = fb.shape[0]

    def zspec(which, col):
        return pl.BlockSpec((L, mw), lambda s, fb, bb, sq, fi, la: ((fb, bb)[which][s], col))

    def gspec(which):
        return pl.BlockSpec((L, ng), lambda s, fb, bb, sq, fi, la: ((fb, bb)[which][s], 0))

    def gtspec(which):
        return pl.BlockSpec((ng, L), lambda s, fb, bb, sq, fi, la: (0, (fb, bb)[which][s]))

    grid_spec = pltpu.PrefetchScalarGridSpec(
        num_scalar_prefetch=5,
        grid=(nsteps,),
        in_specs=[zspec(0, qcol), zspec(0, kcol), zspec(0, vcol),
                  zspec(1, qcol), zspec(1, kcol), zspec(1, vcol),
                  gspec(0), gspec(1), gtspec(0), gtspec(1),
                  pl.BlockSpec((1, ng), lambda s, *_: (0, 0)),
                  pl.BlockSpec((ng, 1), lambda s, *_: (0, 0)),
                  pl.BlockSpec((1, 2 * mh, dk, 2 * dk), lambda s, fb, bb, sq, fi, la: (sq[s], 0, 0, 0)),
                  pl.BlockSpec((1, 2 * mh, LANES_V7X), lambda s, fb, bb, sq, fi, la: (sq[s], 0, 0))],
        out_specs=(pl.BlockSpec((L, mw), lambda s, fb, bb, sq, fi, la: (fb[s], 0)),
                   pl.BlockSpec((L, mw), lambda s, fb, bb, sq, fi, la: (bb[s], 0)),
                   pl.BlockSpec((1, 2 * mh, dk, 2 * dk), lambda s, fb, bb, sq, fi, la: (sq[s], 0, 0, 0)),
                   pl.BlockSpec((1, 2 * mh, LANES_V7X), lambda s, fb, bb, sq, fi, la: (sq[s], 0, 0))),
        scratch_shapes=[pltpu.VMEM((2 * mh, dk, 2 * dk), F32), pltpu.VMEM((2 * mh, LANES_V7X), F32)],
    )
    return pl.pallas_call(
        functools.partial(_mlstm_kernel, mh=mh, dk=dk, chunk=L),
        out_shape=(jax.ShapeDtypeStruct((t_all, mw), BF16),
                   jax.ShapeDtypeStruct((t_all, mw), BF16),
                   jax.ShapeDtypeStruct((nseq, 2 * mh, dk, 2 * dk), F32),
                   jax.ShapeDtypeStruct((nseq, 2 * mh, LANES_V7X), F32)),
        grid_spec=grid_spec,
        compiler_params=_params(("arbitrary",)),
        name="mlstm",
    )(fb, bb, sq, fi, la, z, z, z, z, z, z, gates, gates, gates_t, gates_t, brow, bcol, s0, m0)


def _sink_column(sink_ref, kv, groups, rows_per_group):
    shape = (groups * rows_per_group, LANES_V7X)
    row_g = lax.broadcasted_iota(jnp.int32, shape, 0) // rows_per_group
    col = jnp.full(shape, sink_ref[kv * groups], F32)
    for g in range(1, groups):
        col = jnp.where(row_g == g, sink_ref[kv * groups + g], col)
    return col


def _softmax_probs(scores, sink):
    hd = sink.shape[1]
    s = jnp.concatenate(scores, axis=1)
    mx = jnp.maximum(jnp.broadcast_to(jnp.max(s, axis=-1, keepdims=True), sink.shape), sink)
    return jnp.exp(s - jnp.concatenate([mx] * (s.shape[1] // hd), axis=1)).astype(BF16), mx


def _weighted_values(p, mx, values, sink):
    hd = sink.shape[1]
    acc = None
    off = 0
    for v in values:
        v_aug = jnp.concatenate([v, jnp.ones(v.shape, BF16)], axis=1)
        pv = _dot(p[:, off:off + v.shape[0]], v_aug)
        acc = pv if acc is None else acc + pv
        off += v.shape[0]
    return acc[:, :hd] / (acc[:, hd:] + jnp.exp(sink - mx))


def _ctx_attn_kernel(sink_ref, q_ref, k_ref, v_ref, o_ref, *, kvh, groups, hd):
    S = q_ref.shape[0]
    scale = hd ** -0.5
    for kv in range(kvh):
        k = k_ref[:, kv * hd:(kv + 1) * hd]
        v = v_ref[:, kv * hd:(kv + 1) * hd]
        q = jnp.concatenate([q_ref[:, (kv * groups + g) * hd:(kv * groups + g + 1) * hd]
                             for g in range(groups)], axis=0)
        sink = _sink_column(sink_ref, kv, groups, S)
        p, mx = _softmax_probs([_dot_nt(q, k) * scale], sink)
        o = _weighted_values(p, mx, [v], sink)
        for g in range(groups):
            o_ref[:, (kv * groups + g) * hd:(kv * groups + g + 1) * hd] = o[g * S:(g + 1) * S].astype(BF16)


def _ctx_attention(sink, z, *, batch, seq, kvh, groups, hd, kcol, vcol):
    qw = kvh * groups * hd
    kw = kvh * hd
    return pl.pallas_call(
        functools.partial(_ctx_attn_kernel, kvh=kvh, groups=groups, hd=hd),
        out_shape=jax.ShapeDtypeStruct((batch * seq, qw), BF16),
        grid=(batch,),
        in_specs=[pl.BlockSpec(memory_space=pltpu.SMEM),
                  pl.BlockSpec((seq, qw), lambda b: (b, 0)),
                  pl.BlockSpec((seq, kw), lambda b: (b, kcol)),
                  pl.BlockSpec((seq, kw), lambda b: (b, vcol))],
        out_specs=pl.BlockSpec((seq, qw), lambda b: (b, 0)),
        compiler_params=_params(("arbitrary",)),
        name="ctx_attention",
    )(sink, z, z, z)


def _lat_attn_kernel(sink_ref, q_ref, kp_ref, kc_ref, kn_ref, vp_ref, vc_ref, vn_ref, ck_ref, cv_ref, o_ref,
                     *, kvh, groups, hd):
    j = pl.program_id(1)
    nb = pl.num_programs(1)
    Q = q_ref.shape[0]
    scale = hd ** -0.5
    R = groups * Q
    rq = lax.broadcasted_iota(jnp.int32, (R, Q), 0) % Q
    cc = lax.broadcasted_iota(jnp.int32, (R, Q), 1)
    mask_prev = jnp.logical_and(cc >= rq, j > 0)
    mask_next = jnp.logical_and(cc <= rq, j < nb - 1)
    heads = [slice(kv * hd, (kv + 1) * hd) for kv in range(kvh)]
    sinks = [_sink_column(sink_ref, kv, groups, Q) for kv in range(kvh)]
    scores = []
    for kv, sl in enumerate(heads):
        q = jnp.concatenate([q_ref[:, (kv * groups + g) * hd:(kv * groups + g + 1) * hd]
                             for g in range(groups)], axis=0)
        scores.append([jnp.where(mask_prev, _dot_nt(q, kp_ref[:, sl]) * scale, -jnp.inf),
                       _dot_nt(q, kc_ref[:, sl]) * scale,
                       jnp.where(mask_next, _dot_nt(q, kn_ref[:, sl]) * scale, -jnp.inf),
                       _dot_nt(q, ck_ref[0, 0, :, sl].astype(BF16)) * scale])
    probs = [_softmax_probs(scores[kv], sinks[kv]) for kv in range(kvh)]
    for kv, sl in enumerate(heads):
        p, mx = probs[kv]
        o = _weighted_values(p, mx, [vp_ref[:, sl], vc_ref[:, sl], vn_ref[:, sl],
                                     cv_ref[0, 0, :, sl].astype(BF16)], sinks[kv])
        for g in range(groups):
            o_ref[:, (kv * groups + g) * hd:(kv * groups + g + 1) * hd] = o[g * Q:(g + 1) * Q].astype(BF16)


def _lat_attention(sink, z, cache_k, cache_v, *, t_ctx, dec_batch, dec_seq, kvh, groups, hd, kcol, vcol):
    assert WINDOW == Q_BLOCK
    Q = Q_BLOCK
    nb = dec_seq // Q
    base = t_ctx // Q
    qw = kvh * groups * hd
    kw = kvh * hd
    past = cache_k.shape[2]

    def kvspec(col, shift):
        return pl.BlockSpec((Q, kw), lambda b, j: (base + b * nb + jnp.clip(j + shift, 0, nb - 1), col))

    cspec = pl.BlockSpec((1, 1, past, kw), lambda b, j: (b, 0, 0, 0))
    return pl.pallas_call(
        functools.partial(_lat_attn_kernel, kvh=kvh, groups=groups, hd=hd),
        out_shape=jax.ShapeDtypeStruct((dec_batch * dec_seq, qw), BF16),
        grid=(dec_batch, nb),
        in_specs=[pl.BlockSpec(memory_space=pltpu.SMEM),
                  pl.BlockSpec((Q, qw), lambda b, j: (base + b * nb + j, 0)),
                  kvspec(kcol, -1), kvspec(kcol, 0), kvspec(kcol, 1),
                  kvspec(vcol, -1), kvspec(vcol, 0), kvspec(vcol, 1),
                  cspec, cspec],
        out_specs=pl.BlockSpec((Q, qw), lambda b, j: (b * nb + j, 0)),
        compiler_params=_params(("arbitrary", "arbitrary")),
        name="lat_attention",
    )(sink, z, z, z, z, z, z, z, cache_k, cache_v)


def _outproj_kernel(hf_ref, hb_ref, om_ref, hac_ref, hal_ref, gm_ref, ga_ref, xp_ref, xs_ref, mod_ref,
                    mn_ref, n2_ref, wm_ref, wa_ref, wo_ref, rw_ref,
                    x1_ref, h2_ref, lg_ref, *, nctx_tiles, mh, dv):
    i = pl.program_id(0)
    is_ctx = i < nctx_tiles
    tm = x1_ref.shape[0]
    for r0 in range(0, tm, tm // OUTPROJ_SPLIT):
        rs = pl.ds(r0, tm // OUTPROJ_SPLIT)
        hm = hf_ref[rs, :].astype(F32) + hb_ref[rs, :].astype(F32)
        parts = []
        for h in range(mh):
            sl = hm[:, h * dv:(h + 1) * dv]
            parts.append(sl * lax.rsqrt(jnp.mean(sl * sl, axis=-1, keepdims=True) + EPS))
        hmn = jnp.concatenate(parts, axis=1) * mn_ref[...] * jax.nn.sigmoid(om_ref[rs, :].astype(F32))
        ha = jnp.where(is_ctx, hac_ref[rs, :], hal_ref[rs, :])
        y = (jax.nn.sigmoid(gm_ref[rs, :].astype(F32)) * _dot(hmn.astype(BF16), wm_ref[...])
             + jax.nn.sigmoid(ga_ref[rs, :].astype(F32)) * _dot(ha, wa_ref[...]))
        x = jnp.where(is_ctx, xp_ref[rs, :], xs_ref[rs, :])
        x1 = x + mod_ref[0, 2:3, :] * _dot(y.astype(BF16), wo_ref[...])
        x1_ref[rs, :] = x1
        n = x1 * lax.rsqrt(jnp.mean(x1 * x1, axis=-1, keepdims=True) + EPS) * n2_ref[...]
        h2 = n * (1.0 + mod_ref[0, 4:5, :]) + mod_ref[0, 3:4, :]
        h2_ref[rs, :] = _pack_bf16_pair(h2)
        lg_ref[:, rs] = lax.dot_general(rw_ref[...], h2, _NT, precision=lax.Precision.HIGHEST,
                                        preferred_element_type=F32)


def _outproj(hf, hb, z, ha_ctx, ha_lat, x_prompt2, x_sample2, mods, mnorm, n2, wm, wa, wo, rw_t,
             *, t_ctx, dec_seq, mh, dv, omcol, gmcol, gacol):
    t_all = hf.shape[0]
    D = x_prompt2.shape[1]
    mw = mh * dv
    qw = ha_ctx.shape[1]
    E = rw_t.shape[0]
    tm = min(TM_OUTPROJ, t_ctx, dec_seq)
    nctx = t_ctx // tm
    per_seq = dec_seq // tm

    def ctx_blk(i):
        return (jnp.minimum(i, nctx - 1), 0)

    def lat_blk(i):
        return (jnp.maximum(i - nctx, 0), 0)

    def mod_row(i):
        return (jnp.where(i < nctx, 0, 1 + (i - nctx) // per_seq), 0, 0)

    const = lambda i: (0, 0)
    single = pl.Buffered(1)
    return pl.pallas_call(
        functools.partial(_outproj_kernel, nctx_tiles=nctx, mh=mh, dv=dv),
        out_shape=(jax.ShapeDtypeStruct((t_all, D), F32),
                   jax.ShapeDtypeStruct((t_all, D // 2), jnp.uint32),
                   jax.ShapeDtypeStruct((E, t_all), F32)),
        grid=(t_all // tm,),
        in_specs=[pl.BlockSpec((tm, mw), lambda i: (i, 0)),
                  pl.BlockSpec((tm, mw), lambda i: (i, 0)),
                  pl.BlockSpec((tm, mw), lambda i: (i, omcol)),
                  pl.BlockSpec((tm, qw), ctx_blk),
                  pl.BlockSpec((tm, qw), lat_blk),
                  pl.BlockSpec((tm, D), lambda i: (i, gmcol)),
                  pl.BlockSpec((tm, D), lambda i: (i, gacol)),
                  pl.BlockSpec((tm, D), ctx_blk),
                  pl.BlockSpec((tm, D), lat_blk),
                  pl.BlockSpec((1, N_MOD, D), mod_row),
                  pl.BlockSpec((1, mw), const),
                  pl.BlockSpec((1, D), const),
                  pl.BlockSpec((mw, D), const, pipeline_mode=single),
                  pl.BlockSpec((qw, D), const, pipeline_mode=single),
                  pl.BlockSpec((D, D), const, pipeline_mode=single),
                  pl.BlockSpec((E, D), const, pipeline_mode=single)],
        out_specs=(pl.BlockSpec((tm, D), lambda i: (i, 0)),
                   pl.BlockSpec((tm, D // 2), lambda i: (i, 0)),
                   pl.BlockSpec((E, tm), lambda i: (0, i))),
        compiler_params=_params(("arbitrary",)),
        name="outproj",
    )(hf, hb, z, ha_ctx, ha_lat, z, z, x_prompt2, x_sample2, mods, mnorm, n2, wm, wa, wo, rw_t)


def _dispatch_kernel(pstart_ref, pcount_ref, nact_ref, pos_ref, h2p_ref, x1_ref, mod_ref,
                     w1_ref, w3_ref, w2_ref, xs_hbm, x1s_ref, zblk, sem, psem, *, top_k):
    i = pl.program_id(0)
    td = h2p_ref.shape[0]
    rows = zblk.shape[0]

    @pl.when(i == 0)
    def _():
        zblk[...] = jnp.zeros_like(zblk)
        zrow = zblk.at[pl.ds(0, 1)]

        def per_expert(e, total):
            def fill(r, carry):
                pltpu.make_async_copy(zrow, xs_hbm.at[pl.ds(pstart_ref[e] + r, 1)], psem).start()
                return carry

            lax.fori_loop(0, pcount_ref[e], fill, 0)
            return total + pcount_ref[e]

        total = lax.fori_loop(0, pstart_ref.shape[0], per_expert, 0)

        def drain(j, carry):
            pltpu.make_async_copy(zrow, xs_hbm.at[pl.ds(0, 1)], psem).wait()
            return carry

        lax.fori_loop(0, total, drain, 0)

        def empty_block(b, carry):
            fill = pltpu.make_async_copy(zblk, xs_hbm.at[pl.ds(b * rows, rows)], psem)
            fill.start()
            fill.wait()
            return carry

        lax.fori_loop(nact_ref[0], xs_hbm.shape[0] // rows, empty_block, 0)

    F = w1_ref.shape[1]
    edges = [F * k // top_k // LANES_V7X * LANES_V7X for k in range(top_k)] + [F]
    x = _unpack_bf16_pair(h2p_ref[...])
    hmid = []
    for k in range(top_k):
        for t in range(td):
            pltpu.make_async_copy(h2p_ref.at[pl.ds(t, 1)], xs_hbm.at[pl.ds(pos_ref[0, k, t], 1)], sem).start()
        if edges[k + 1] > edges[k]:
            cols = pl.ds(edges[k], edges[k + 1] - edges[k])
            hmid.append((jax.nn.silu(_dot(x, w1_ref[:, cols])) * _dot(x, w3_ref[:, cols])).astype(BF16))
    x1s_ref[...] = x1_ref[...] + mod_ref[0, 5:6, :] * _dot(jnp.concatenate(hmid, axis=1), w2_ref[...])

    for k in range(top_k):
        pltpu.make_async_copy(h2p_ref, xs_hbm.at[pl.ds(0, td)], sem).wait()


def _dispatch_rows(pad_start, pad_count, nact, pos3, h2p, x1, mods, sw1, sw3, sw2, *, n_slots, rows,
                   t_ctx, dec_seq):
    nt, K, td = pos3.shape
    T, D = x1.shape
    F = sw1.shape[1]
    nctx = t_ctx // td
    per_seq = dec_seq // td
    const = lambda i, *_: (0, 0)
    single = pl.Buffered(1)
    grid_spec = pltpu.PrefetchScalarGridSpec(
        num_scalar_prefetch=3,
        grid=(nt,),
        in_specs=[pl.BlockSpec((1, K, td), lambda i, *_: (i, 0, 0), memory_space=pltpu.SMEM),
                  pl.BlockSpec((td, D // 2), lambda i, *_: (i, 0)),
                  pl.BlockSpec((td, D), lambda i, *_: (i, 0)),
                  pl.BlockSpec((1, N_MOD, D),
                               lambda i, *_: (jnp.where(i < nctx, 0, 1 + (i - nctx) // per_seq), 0, 0)),
                  pl.BlockSpec((D, F), const, pipeline_mode=single),
                  pl.BlockSpec((D, F), const, pipeline_mode=single),
                  pl.BlockSpec((F, D), const, pipeline_mode=single)],
        out_specs=(pl.BlockSpec(memory_space=pl.ANY),
                   pl.BlockSpec((td, D), lambda i, *_: (i, 0))),
        scratch_shapes=[pltpu.VMEM((rows, D // 2), jnp.uint32), pltpu.SemaphoreType.DMA(()),
                        pltpu.SemaphoreType.DMA(())],
    )
    return pl.pallas_call(
        functools.partial(_dispatch_kernel, top_k=K),
        out_shape=(jax.ShapeDtypeStruct((n_slots, D // 2), jnp.uint32),
                   jax.ShapeDtypeStruct((T, D), F32)),
        grid_spec=grid_spec,
        compiler_params=_params(("arbitrary",)),
        name="dispatch_shared",
    )(pad_start, pad_count, nact, pos3, h2p, x1, mods, sw1, sw3, sw2)


def _moe_kernel(blk_e, nact_ref, x_ref, w1_hbm, w3_hbm, w2_hbm, y_ref,
                w1b, w3b, w2b, stg_in, stg_out, wsem):
    i = pl.program_id(0)
    nact = nact_ref[0]
    e = blk_e[i]

    rc_in, rc_out = stg_in.shape[1], stg_out.shape[1]
    plan = ([(w1_hbm, w1b, stg_in, r, rc_in) for r in range(0, w1b.shape[0], rc_in)]
            + [(w3_hbm, w3b, stg_in, r, rc_in) for r in range(0, w3b.shape[0], rc_in)]
            + [(w2_hbm, w2b, stg_out, r, rc_out) for r in range(0, w2b.shape[0], rc_out)])
    depth = stg_in.shape[0]

    def chunk_copy(ex, n):
        src, _, stg, r, rc = plan[n]
        return pltpu.make_async_copy(src.at[ex, pl.ds(r, rc), :], stg.at[n % depth], wsem.at[n % depth])

    @pl.when(jnp.logical_and(i < nact, jnp.logical_or(i == 0, e != blk_e[jnp.maximum(i - 1, 0)])))
    def _():
        @pl.when(i == 0)
        def _():
            for n in range(depth - 1):
                chunk_copy(e, n).start()

        for n, (_, dst, stg, r, rc) in enumerate(plan):
            if n + depth - 1 < len(plan):
                chunk_copy(e, n + depth - 1).start()
            chunk_copy(e, n).wait()
            dst[pl.ds(r, rc), :] = stg[n % depth].astype(BF16)

    e_next = blk_e[jnp.minimum(i + 1, pl.num_programs(0) - 1)]

    @pl.when(jnp.logical_and(i + 1 < nact, e_next != e))
    def _():
        for n in range(depth - 1):
            chunk_copy(e_next, n).start()

    @pl.when(i < nact)
    def _():
        x = _unpack_bf16_pair(x_ref[...])
        hmid = (jax.nn.silu(_dot(x, w1b[...])) * _dot(x, w3b[...])).astype(BF16)
        y_ref[...] = _pack_bf16_pair(_dot(hmid, w2b[...]))

    @pl.when(i >= nact)
    def _():
        y_ref[...] = jnp.zeros_like(y_ref)


def _moe(blk_e, nact, xs, w1, w3, w2, *, rows):
    nblk = blk_e.shape[0]
    E, D, F = w1.shape
    hbm = pl.BlockSpec(memory_space=pl.ANY)
    grid_spec = pltpu.PrefetchScalarGridSpec(
        num_scalar_prefetch=2,
        grid=(nblk,),
        in_specs=[pl.BlockSpec((rows, D // 2), lambda i, be, na: (jnp.minimum(i, na[0] - 1), 0)),
                  hbm, hbm, hbm],
        out_specs=pl.BlockSpec((rows, D // 2), lambda i, be, na: (i, 0)),
        scratch_shapes=[pltpu.VMEM((D, F), BF16), pltpu.VMEM((D, F), BF16), pltpu.VMEM((F, D), BF16),
                        pltpu.VMEM((WEIGHT_RING, WEIGHT_CHUNK_ROWS, F), F32),
                        pltpu.VMEM((WEIGHT_RING, WEIGHT_CHUNK_ROWS, D), F32),
                        pltpu.SemaphoreType.DMA((WEIGHT_RING,))],
    )
    return pl.pallas_call(
        _moe_kernel,
        out_shape=jax.ShapeDtypeStruct((nblk * rows, D // 2), jnp.uint32),
        grid_spec=grid_spec,
        compiler_params=_params(("arbitrary",)),
        name="routed_experts",
    )(blk_e, nact, xs, w1, w3, w2)


def _router_kernel(lg_ref, rb_ref, eidx_ref, rank_ref, wtok_ref, cnt_ref, carry, *, top_k, n_groups, topk_groups):
    i = pl.program_id(0)

    @pl.when(i == 0)
    def _():
        carry[...] = jnp.zeros_like(carry)

    E, tr = lg_ref.shape
    gs = E // n_groups
    scores = jax.nn.sigmoid(lg_ref[...])
    biased = scores + rb_ref[...]
    b3 = biased.reshape(n_groups, gs, tr)
    io3 = lax.broadcasted_iota(jnp.int32, b3.shape, 1)
    m1 = jnp.max(b3, axis=1, keepdims=True)
    i1 = jnp.min(jnp.where(b3 == m1, io3, gs), axis=1, keepdims=True)
    m2 = jnp.max(jnp.where(io3 == i1, -jnp.inf, b3), axis=1, keepdims=True)
    grp = (m1 + m2).reshape(n_groups, tr)
    iog = lax.broadcasted_iota(jnp.int32, grp.shape, 0)
    sel = jnp.zeros(grp.shape, jnp.bool_)
    for _ in range(topk_groups):
        mx = jnp.max(grp, axis=0, keepdims=True)
        hit = iog == jnp.min(jnp.where(grp == mx, iog, n_groups), axis=0, keepdims=True)
        sel = jnp.logical_or(sel, hit)
        grp = jnp.where(hit, -jnp.inf, grp)
    masked = jnp.where(sel.reshape(n_groups, 1, tr), b3, -jnp.inf).reshape(E, tr)
    ioe = lax.broadcasted_iota(jnp.int32, (E, tr), 0)
    onehot = jnp.zeros((E, tr), F32)
    hits, idxs, ws = [], [], []
    for _ in range(top_k):
        mx = jnp.max(masked, axis=0, keepdims=True)
        ix = jnp.min(jnp.where(masked == mx, ioe, E), axis=0, keepdims=True)
        hit = ioe == ix
        hits.append(hit)
        idxs.append(ix)
        ws.append(jnp.sum(jnp.where(hit, scores, 0.0), axis=0, keepdims=True))
        onehot = onehot + hit.astype(F32)
        masked = jnp.where(hit, -jnp.inf, masked)
    wsum = ws[0]
    for w in ws[1:]:
        wsum = wsum + w
    ri = lax.broadcasted_iota(jnp.int32, (tr, tr), 0)
    ci = lax.broadcasted_iota(jnp.int32, (tr, tr), 1)
    before = _dot(onehot.astype(BF16), (ri < ci).astype(BF16)) + carry[...]
    ranks = [jnp.sum(jnp.where(hit, before, 0.0), axis=0, keepdims=True) for hit in hits]
    carry[...] = carry[...] + jnp.sum(onehot, axis=1, keepdims=True)
    cnt_ref[...] = carry[...].astype(jnp.int32)
    eidx_ref[...] = jnp.concatenate(idxs, axis=0)
    rank_ref[...] = jnp.concatenate(ranks, axis=0).astype(jnp.int32)
    wrows = jnp.concatenate([w / wsum * ROUTED_SCALE for w in ws]
                            + [jnp.zeros((LANES_V7X - top_k, tr), F32)], axis=0)
    wtok_ref[...] = wrows.T


def _router(logits_t, router_b):
    E, T = logits_t.shape
    tr = min(TR_ROUTER, T)
    return pl.pallas_call(
        functools.partial(_router_kernel, top_k=TOP_K, n_groups=N_GROUPS, topk_groups=TOPK_GROUPS),
        out_shape=(jax.ShapeDtypeStruct((TOP_K, T), jnp.int32),
                   jax.ShapeDtypeStruct((TOP_K, T), jnp.int32),
                   jax.ShapeDtypeStruct((T, LANES_V7X), F32),
                   jax.ShapeDtypeStruct((E, 1), jnp.int32)),
        grid=(T // tr,),
        in_specs=[pl.BlockSpec((E, tr), lambda i: (0, i)),
                  pl.BlockSpec((E, 1), lambda i: (0, 0))],
        out_specs=(pl.BlockSpec((TOP_K, tr), lambda i: (0, i)),
                   pl.BlockSpec((TOP_K, tr), lambda i: (0, i)),
                   pl.BlockSpec((tr, LANES_V7X), lambda i: (i, 0)),
                   pl.BlockSpec((E, 1), lambda i: (0, 0))),
        scratch_shapes=[pltpu.VMEM((E, 1), F32)],
        compiler_params=_params(("arbitrary",)),
        name="router",
    )(logits_t, router_b.reshape(E, 1))


def _slot_pos_kernel(start_ref, eidx_ref, rank_ref, pos_ref, *, n_experts):
    eidx = eidx_ref[...]
    pos = rank_ref[...]
    for e in range(n_experts):
        pos = pos + jnp.where(eidx == e, start_ref[e], 0)
    pos_ref[...] = pos


def _slot_pos(start_pad, eidx, rank):
    K, T = eidx.shape
    tl = min(TL_SLOTPOS, T)
    return pl.pallas_call(
        functools.partial(_slot_pos_kernel, n_experts=start_pad.shape[0]),
        out_shape=jax.ShapeDtypeStruct((K, T), jnp.int32),
        grid=(T // tl,),
        in_specs=[pl.BlockSpec(memory_space=pltpu.SMEM),
                  pl.BlockSpec((K, tl), lambda i: (0, i)),
                  pl.BlockSpec((K, tl), lambda i: (0, i))],
        out_specs=pl.BlockSpec((K, tl), lambda i: (0, i)),
        compiler_params=_params(("arbitrary",)),
        name="slot_pos",
    )(start_pad, eidx, rank)


def _dispatch(eidx, rank, counts, rows):
    K, T = eidx.shape
    A = K * T
    E = counts.shape[0]
    padded = (counts + rows - 1) // rows * rows
    end_pad = jnp.cumsum(padded)
    start_pad = end_pad - padded
    nblk = -(-(A + E * (rows - 1)) // rows)
    n_slots = nblk * rows
    nact = end_pad[-1] // rows
    blk = jnp.arange(nblk, dtype=jnp.int32)
    blk_e = jnp.sum(end_pad[None, :] <= (blk * rows)[:, None], axis=1).astype(jnp.int32)
    last_e = jnp.sum(end_pad <= (nact - 1) * rows).astype(jnp.int32)
    blk_e = jnp.where(blk < nact, blk_e, last_e)
    pos = _slot_pos(start_pad.astype(jnp.int32), eidx, rank)
    pad_start = (start_pad + counts).astype(jnp.int32)
    pad_count = (padded - counts).astype(jnp.int32)
    return pos, blk_e, nact.astype(jnp.int32).reshape(1), pad_start, pad_count, n_slots


def _tile_major(pos, tile):
    K, T = pos.shape
    return pos.reshape(K, T // tile, tile).transpose(1, 0, 2)


def _final_kernel(pos_ref, posn_ref, x1s_ref, wt_ref, mod_ref, fn_ref, y_hbm,
                  oc_ref, ol_ref, ybuf0, ybuf1, sem, *, top_k, nctx_tiles):
    i = pl.program_id(0)
    nt = pl.num_programs(0)
    tm = x1s_ref.shape[0]

    def gather(idx_ref, buf, s):
        for k in range(top_k):
            for t in range(tm):
                pltpu.make_async_copy(y_hbm.at[pl.ds(idx_ref[0, k, t], 1)], buf.at[k, pl.ds(t, 1)], s).start()

    def gather_wait(buf, s):
        for k in range(top_k):
            pltpu.make_async_copy(y_hbm.at[pl.ds(0, tm)], buf.at[k], s).wait()

    @pl.when(i == 0)
    def _():
        gather(pos_ref, ybuf0, sem.at[0])

    def tile(cur, sem_cur, nxt, sem_nxt):
        @pl.when(i + 1 < nt)
        def _():
            gather(posn_ref, nxt, sem_nxt)

        gather_wait(cur, sem_cur)
        wt = wt_ref[...]
        lo = hi = None
        for k in range(top_k):
            w = cur[k]
            wk = wt[:, k:k + 1]
            lo_k = pltpu.bitcast(w << 16, F32) * wk
            hi_k = pltpu.bitcast(w & jnp.uint32(0xFFFF0000), F32) * wk
            lo = lo_k if lo is None else lo + lo_k
            hi = hi_k if hi is None else hi + hi_k
        x2 = x1s_ref[...] + mod_ref[0, 5:6, :] * jnp.concatenate([lo, hi], axis=1)
        out = x2 * lax.rsqrt(jnp.mean(x2 * x2, axis=-1, keepdims=True) + EPS) * fn_ref[...]

        @pl.when(i < nctx_tiles)
        def _():
            oc_ref[...] = out

        @pl.when(i >= nctx_tiles)
        def _():
            ol_ref[...] = out

    @pl.when(i % 2 == 0)
    def _():
        tile(ybuf0, sem.at[0], ybuf1, sem.at[1])

    @pl.when(i % 2 == 1)
    def _():
        tile(ybuf1, sem.at[1], ybuf0, sem.at[0])


def _final(pos3, x1s, wtok, ys, mods, fnorm, *, t_ctx, dec_seq):
    nt, K, tm = pos3.shape
    t_all, D = x1s.shape
    nctx = t_ctx // tm
    per_seq = dec_seq // tm

    def mod_row(i):
        return (jnp.where(i < nctx, 0, 1 + (i - nctx) // per_seq), 0, 0)

    const = lambda i: (0, 0)
    smem_blk = lambda f: pl.BlockSpec((1, K, tm), f, memory_space=pltpu.SMEM)
    return pl.pallas_call(
        functools.partial(_final_kernel, top_k=K, nctx_tiles=nctx),
        out_shape=(jax.ShapeDtypeStruct((t_ctx, D), F32), jax.ShapeDtypeStruct((t_all - t_ctx, D), F32)),
        grid=(nt,),
        in_specs=[smem_blk(lambda i: (i, 0, 0)),
                  smem_blk(lambda i: (jnp.minimum(i + 1, nt - 1), 0, 0)),
                  pl.BlockSpec((tm, D), lambda i: (i, 0)),
                  pl.BlockSpec((tm, LANES_V7X), lambda i: (i, 0)),
                  pl.BlockSpec((1, N_MOD, D), mod_row),
                  pl.BlockSpec((1, D), const),
                  pl.BlockSpec(memory_space=pl.ANY)],
        out_specs=(pl.BlockSpec((tm, D), lambda i: (jnp.minimum(i, nctx - 1), 0)),
                   pl.BlockSpec((tm, D), lambda i: (jnp.maximum(i - nctx, 0), 0))),
        scratch_shapes=[pltpu.VMEM((K, tm, D // 2), jnp.uint32), pltpu.VMEM((K, tm, D // 2), jnp.uint32),
                        pltpu.SemaphoreType.DMA((2,))],
        compiler_params=_params(("arbitrary",)),
        name="combine_final",
    )(pos3, pos3, x1s, wtok, mods, fnorm, ys)


def _rope_tables(dec_seq, hd):
    nf = hd // 4
    t = jnp.arange(dec_seq)
    inv = ROPE_BASE ** (-jnp.arange(nf, dtype=F32) / nf)
    ang_r = (t // GRID_W).astype(F32)[:, None] * inv
    ang_c = (t % GRID_W).astype(F32)[:, None] * inv
    cos = jnp.concatenate([jnp.cos(ang_r)] * 2 + [jnp.cos(ang_c)] * 2, axis=1)
    sin = jnp.concatenate([-jnp.sin(ang_r), jnp.sin(ang_r), -jnp.sin(ang_c), jnp.sin(ang_c)], axis=1)
    return cos, sin


def _scan_steps(batch, seq, dec_batch, dec_seq, L):
    fb, bb, sq, fi, la = [], [], [], [], []
    base = 0
    for sid, S in enumerate([seq] * batch + [dec_seq] * dec_batch):
        nc = S // L
        for c in range(nc):
            fb.append(base + c)
            bb.append(base + nc - 1 - c)
            sq.append(sid)
            fi.append(int(c == 0))
            la.append(int(c == nc - 1))
        base += nc
    return tuple(jnp.asarray(np.asarray(a, dtype=np.int32)) for a in (fb, bb, sq, fi, la))


def kernel(x_prompt, x_sample, cache_k, cache_v, state_mlstm_C, state_mlstm_n, state_mlstm_m, c, c_ctx,
           w_mod, b_mod, norm1_w, norm2_w, w_in, igate_b, fgate_b, mlstm_norm_w, attn_sink,
           w_branch_m, w_branch_a, w_out, router_w, router_b, expert_w1, expert_w3, expert_w2,
           shared_w1, shared_w3, shared_w2, final_norm_w):
    batch, seq, D = x_prompt.shape
    dec_batch, dec_seq, _ = x_sample.shape
    depth = w_in.shape[0]
    assert depth == 1, "single trunk layer"
    _, _, past, kvh, hd = cache_k.shape
    mh, dk, dv = state_mlstm_C.shape[3:]
    ah = attn_sink.shape[1]
    groups = ah // kvh
    E = router_w.shape[2]
    assert dk == dv == hd == LANES_V7X
    t_ctx, t_lat = batch * seq, dec_batch * dec_seq
    mw, qw, kw = mh * dk, ah * hd, kvh * hd
    ng = 4 * mh

    wi = w_in[0]
    o = 0
    seg = {}
    for name, width in (("qm", mw), ("km", mw), ("vm", mw), ("om", mw), ("im", 2 * mh), ("fm", 2 * mh),
                        ("qa", qw), ("ka", kw), ("va", kw), ("gm", D), ("ga", D)):
        seg[name] = wi[:, o:o + width]
        o += width
    order = ("qa", "gm", "ga", "qm", "km", "vm", "om", "ka", "va")
    w_main = jnp.concatenate([seg[nm] for nm in order], axis=1).astype(BF16)
    col = {}
    o = 0
    for nm in order:
        col[nm] = o
        o += seg[nm].shape[1]
    tn = 2 * kw
    for nm in order[:-2]:
        assert col[nm] % tn == 0 and seg[nm].shape[1] % tn == 0
    for nm, width in (("gm", D), ("ga", D), ("qm", mw), ("km", mw), ("vm", mw), ("om", mw), ("ka", kw), ("va", kw)):
        assert col[nm] % width == 0
    w_gate = jnp.pad(jnp.concatenate([seg["im"], seg["fm"]], axis=1), ((0, 0), (0, LANES_V7X - ng))).astype(BF16)

    R = -(-(1 + dec_batch) // 8) * 8
    cond = jnp.concatenate([c_ctx[None, :], c, jnp.zeros((R - 1 - dec_batch, D), F32)], axis=0)
    mods = _modulation(cond, w_mod[0], b_mod[0]).reshape(R, N_MOD, D)

    xp2 = x_prompt.reshape(t_ctx, D)
    xs2 = x_sample.reshape(t_lat, D)
    cos, sin = _rope_tables(dec_seq, hd)
    h1, gates, gates_t = _prenorm(xp2, xs2, mods, norm1_w, w_gate, t_ctx=t_ctx, dec_seq=dec_seq, n_gates=ng)
    z, kv32 = _inproj(h1, w_main, cos, sin, t_ctx=t_ctx, dec_seq=dec_seq, tn=tn, n_rope_tiles=qw // tn)

    nseq = batch + dec_batch
    C0 = jnp.concatenate([jnp.zeros((batch, 2, mh, dk, dv), F32), state_mlstm_C[:, 0]], axis=0)
    n0 = jnp.concatenate([jnp.zeros((batch, 2, mh, dk), F32), state_mlstm_n[:, 0]], axis=0)
    m0 = jnp.concatenate([jnp.zeros((batch, 2, mh), F32), state_mlstm_m[:, 0]], axis=0)
    s0 = jnp.concatenate([C0, jnp.broadcast_to(n0[..., None], (nseq, 2, mh, dk, dv))], axis=-1)
    s0 = s0.reshape(nseq, 2 * mh, dk, 2 * dv)
    m0 = jnp.broadcast_to(m0.reshape(nseq, 2 * mh, 1), (nseq, 2 * mh, LANES_V7X))
    gate_b = jnp.concatenate([igate_b[0].reshape(-1), fgate_b[0].reshape(-1)])
    steps = _scan_steps(batch, seq, dec_batch, dec_seq, M_CHUNK)
    hf, hb, s_fin, m_fin = _mlstm(z, gates, gates_t, gate_b.reshape(1, ng), gate_b.reshape(ng, 1), s0, m0, steps,
                                  mh=mh, dk=dk, qcol=col["qm"] // mw, kcol=col["km"] // mw, vcol=col["vm"] // mw)

    sink = attn_sink[0]
    ha_ctx = _ctx_attention(sink, z, batch=batch, seq=seq, kvh=kvh, groups=groups, hd=hd,
                            kcol=col["ka"] // kw, vcol=col["va"] // kw)
    ha_lat = _lat_attention(sink, z, cache_k.reshape(dec_batch, depth, past, kw),
                            cache_v.reshape(dec_batch, depth, past, kw), t_ctx=t_ctx, dec_batch=dec_batch,
                            dec_seq=dec_seq, kvh=kvh, groups=groups, hd=hd, kcol=col["ka"] // kw,
                            vcol=col["va"] // kw)

    x1, h2p, logits_t = _outproj(hf, hb, z, ha_ctx, ha_lat, xp2, xs2, mods, mlstm_norm_w, norm2_w,
                                w_branch_m[0].astype(BF16), w_branch_a[0].astype(BF16), w_out[0].astype(BF16),
                                router_w[0].T, t_ctx=t_ctx, dec_seq=dec_seq, mh=mh, dv=dv,
                                omcol=col["om"] // mw, gmcol=col["gm"] // D, gacol=col["ga"] // D)

    eidx, rank, wtok, counts = _router(logits_t, router_b[0])
    pos, blk_e, nact, pad_start, pad_count, n_slots = _dispatch(eidx, rank, counts[:, 0], EXPERT_ROWS)
    xs, x1s = _dispatch_rows(pad_start, pad_count, nact, _tile_major(pos, min(TD_DISPATCH, t_ctx, dec_seq)),
                             h2p, x1, mods, shared_w1[0].astype(BF16), shared_w3[0].astype(BF16),
                             shared_w2[0].astype(BF16), n_slots=n_slots, rows=EXPERT_ROWS, t_ctx=t_ctx,
                             dec_seq=dec_seq)
    ys = _moe(blk_e, nact, xs, expert_w1[0], expert_w3[0], expert_w2[0], rows=EXPERT_ROWS)

    y_ctx, y_lat = _final(_tile_major(pos, min(TM_FINAL, t_ctx, dec_seq)), x1s, wtok, ys, mods,
                          final_norm_w.reshape(1, D), t_ctx=t_ctx, dec_seq=dec_seq)

    y_prompt = y_ctx.reshape(batch, seq, D)
    y_sample = y_lat.reshape(dec_batch, dec_seq, D)
    new_k = kv32[:t_ctx, :kw].reshape(batch, 1, seq, kvh, hd)
    new_v = kv32[:t_ctx, kw:].reshape(batch, 1, seq, kvh, hd)
    s_ctx = s_fin[:batch].reshape(batch, 1, 2, mh, dk, 2 * dv)
    new_C = s_ctx[..., :dv]
    new_n = s_ctx[..., dv]
    new_m = m_fin[:batch, :, 0].reshape(batch, 1, 2, mh)
    return y_prompt, y_sample, new_k, new_v, new_C, new_n, new_m
```
